```python
import math
import jax
import jax.numpy as jnp
from jax import lax
import numpy as np

D_MODEL = 2048
BATCH = 2
SEQ = 4096
DEPTH = 4
DEC_BATCH = 8
DEC_SEQ = 8
PAST_LEN = 16384
PAGE_SIZE = 128

N_EVEN = (DEPTH + 1) // 2
N_ODD = DEPTH // 2
EPS = 1e-6
NEG_INF = -1e30
Q_BLOCK = 128
FFN_RESIDUAL = 0.5

D_FF = 5632

SSM_HEADS = 32
SSM_HEAD_DIM = 64
SSM_D_INNER = SSM_HEADS * SSM_HEAD_DIM
SSM_GROUPS = 4
SSM_STATE = 128
SSM_CONV = 4
SSM_CONV_DIM = SSM_D_INNER + 2 * SSM_GROUPS * SSM_STATE
SSM_CHUNK = 128

NSA_HEADS = 16
NSA_GROUPS = 4
NSA_HPG = NSA_HEADS // NSA_GROUPS
NSA_HEAD_DIM = 128
NSA_Q = NSA_HEADS * NSA_HEAD_DIM
NSA_KV = NSA_GROUPS * NSA_HEAD_DIM
NSA_KV_PARTS = 4
CMP_BLOCK = 32
CMP_STRIDE = 16
SLC_BLOCK = 64
SLC_TOPN = 16
WINDOW = 512
FORCE_SCORE = 1e4
REL_BUCKETS = 32
REL_MAX_DIST = 128

EVEN_SIZES = (SSM_D_INNER, SSM_CONV_DIM, SSM_HEADS, NSA_Q, NSA_KV, NSA_KV, NSA_KV, NSA_KV, NSA_KV, NSA_KV, 3 * NSA_HEADS)
IN_EVEN = sum(EVEN_SIZES)
MIX_EVEN = SSM_D_INNER + NSA_Q

POOL_WINDOWS = (2, 4, 8, 16)
POOL_GROUPS = len(POOL_WINDOWS)
POOL_GROUP_DIM = 256
POOL_DIM = POOL_GROUPS * POOL_GROUP_DIM
POOL_BUF = max(POOL_WINDOWS) - 1

SB_HEADS = 16
SB_HEAD_DIM = 128
SB_DIM = SB_HEADS * SB_HEAD_DIM
ODD_SIZES = (POOL_DIM, SB_DIM, SB_DIM, SB_DIM)
IN_ODD = sum(ODD_SIZES)
MIX_ODD = POOL_DIM + SB_DIM

kernel_name = 'hybrid_ssd_nsa_pool_stickbreak_decode_step'


def split_cols(h, sizes):
    offs = np.cumsum(sizes)[:-1].tolist()
    return jnp.split(h, offs, axis=-1)


def rmsnorm(x, g):
    xf = x.astype(jnp.float32)
    y = xf * lax.rsqrt(jnp.mean(xf * xf, axis=-1, keepdims=True) + EPS)
    return (y * g.astype(jnp.float32)).astype(x.dtype)


def macaron_half(x, g, w_gate, w_up, w_down):
    h = rmsnorm(x, g)
    return x + FFN_RESIDUAL * ((jax.nn.silu(h @ w_gate) * (h @ w_up)) @ w_down)


def rel_bucket(dist):
    n = jnp.maximum(dist, 0)
    exact = REL_BUCKETS // 2
    nf = jnp.maximum(n, 1).astype(jnp.float32)
    log_b = exact + (jnp.log(nf / exact) / math.log(REL_MAX_DIST / exact) * (REL_BUCKETS - exact)).astype(jnp.int32)
    return jnp.where(n < exact, n, jnp.minimum(log_b, REL_BUCKETS - 1))


def group_bias(rel_table, dist):
    b = rel_table.astype(jnp.float32)[rel_bucket(dist)]
    return jnp.moveaxis(b, -1, 0).reshape((NSA_GROUPS, NSA_HPG) + dist.shape)


def ssd_scan(xh, dt, a, bm, cm, h0):
    f32 = jnp.float32
    bsz, L = xh.shape[:2]
    q = SSM_CHUNK if L % SSM_CHUNK == 0 else L
    nc = L // q
    hpg = SSM_HEADS // SSM_GROUPS
    x = xh.astype(f32).reshape(bsz, nc, q, SSM_GROUPS, hpg, SSM_HEAD_DIM)
    dtc = dt.reshape(bsz, nc, q, SSM_GROUPS, hpg)
    bc = bm.astype(f32).reshape(bsz, nc, q, SSM_GROUPS, SSM_STATE)
    cc = cm.astype(f32).reshape(bsz, nc, q, SSM_GROUPS, SSM_STATE)
    acum = jnp.cumsum(dtc * a.reshape(SSM_GROUPS, hpg), axis=2)
    xdt = x * dtc[..., None]
    causal = jnp.tril(jnp.ones((q, q), dtype=bool))[None, None, :, :, None, None]
    seg = acum[:, :, :, None] - acum[:, :, None, :]
    decay = jnp.exp(jnp.where(causal, seg, -jnp.inf))
    cb = jnp.einsum('bctgn,bcsgn->bctsg', cc, bc)
    y_diag = jnp.einsum('bctsgh,bcsghp->bctghp', cb[..., None] * decay, xdt)
    to_end = jnp.exp(acum[:, :, -1:] - acum)
    states = jnp.einsum('bcsgn,bcsghp->bcghpn', bc, xdt * to_end[..., None])
    chunk_decay = jnp.exp(acum[:, :, -1])

    def step(h, inp):
        st, dc = inp
        return h * dc[..., None, None] + st, h

    h_init = h0.astype(f32).reshape(bsz, SSM_GROUPS, hpg, SSM_HEAD_DIM, SSM_STATE)
    h_fin, h_start = lax.scan(step, h_init, (jnp.moveaxis(states, 1, 0), jnp.moveaxis(chunk_decay, 1, 0)))
    h_start = jnp.moveaxis(h_start, 0, 1)
    y_off = jnp.einsum('bctgn,bcghpn->bctghp', cc, h_start) * jnp.exp(acum)[..., None]
    y = (y_diag + y_off).reshape(bsz, L, SSM_HEADS, SSM_HEAD_DIM)
    return y, h_fin.reshape(bsz, SSM_HEADS, SSM_HEAD_DIM, SSM_STATE)


def mamba_mixer(z, xbc, dt_raw, conv_buf, h0, conv_w, conv_b, dt_bias, a_log, d_skip, norm_g):
    f32 = jnp.float32
    bsz, L = xbc.shape[:2]
    ext = jnp.concatenate([conv_buf.astype(xbc.dtype), xbc], axis=1)
    acc = conv_b
    for k in range(SSM_CONV):
        acc = acc + ext[:, k:k + L] * conv_w[k]
    u = jax.nn.silu(acc)
    xs, bm, cm = split_cols(u, (SSM_D_INNER, SSM_GROUPS * SSM_STATE, SSM_GROUPS * SSM_STATE))
    xh = xs.reshape(bsz, L, SSM_HEADS, SSM_HEAD_DIM)
    dt = jax.nn.softplus(dt_raw.astype(f32) + dt_bias.astype(f32))
    a = -jnp.exp(a_log.astype(f32))
    y, h_new = ssd_scan(xh, dt, a, bm.reshape(bsz, L, SSM_GROUPS, SSM_STATE),
                        cm.reshape(bsz, L, SSM_GROUPS, SSM_STATE), h0)
    y = y + d_skip.astype(f32)[:, None] * xh.astype(f32)
    y = y.reshape(bsz, L, SSM_D_INNER) * jax.nn.silu(z.astype(f32))
    y = rmsnorm(y.reshape(bsz, L, SSM_GROUPS, -1), norm_g.reshape(SSM_GROUPS, -1)).reshape(bsz, L, SSM_D_INNER)
    return y, ext[:, L:], h_new


def compress(k, alpha, w):
    bsz, L = k.shape[:2]
    n_sub = L // CMP_STRIDE
    sub = k[:, :n_sub * CMP_STRIDE].astype(jnp.float32).reshape(bsz, n_sub, CMP_STRIDE, NSA_GROUPS, NSA_HEAD_DIM)
    alpha = alpha.astype(jnp.float32)
    first = jnp.einsum('bnrgd,rd->bngd', sub, alpha[:CMP_STRIDE])
    second = jnp.einsum('bnrgd,rd->bngd', sub, alpha[CMP_STRIDE:])
    pooled = first[:, :-1] + second[:, 1:]
    return jnp.einsum('bngd,de->bnge', pooled, w.astype(jnp.float32))


def nsa_attend(q, q_pos, kcmp, vcmp, cmp_end, kslc, vslc, kwin, vwin, win_pos, gates, rel_table):
    f32 = jnp.float32
    bsz, tq = q.shape[:2]
    qg = q.astype(f32).reshape(bsz, tq, NSA_GROUPS, NSA_HPG, NSA_HEAD_DIM)
    dist_c = q_pos[:, None] - cmp_end[None, :]
    mask_c = dist_c >= 0
    s_c = jnp.einsum('btghd,bngd->bghtn', qg, kcmp.astype(f32)) + group_bias(rel_table, dist_c)
    p_c = jax.nn.softmax(jnp.where(mask_c, s_c, NEG_INF), axis=-1) * mask_c
    o_c = jnp.einsum('bghtn,bngd->btghd', p_c, vcmp.astype(f32))
    n_c = kcmp.shape[1]
    n_s = kslc.shape[2]
    ratio = SLC_BLOCK // CMP_STRIDE
    lo = CMP_BLOCK // CMP_STRIDE - 1
    n_off = ratio + lo
    imp = jnp.pad(p_c.sum(axis=2), ((0, 0), (0, 0), (0, 0), (lo, ratio * n_s + n_off - lo - n_c)))
    imp_s = imp[..., 0:ratio * n_s:ratio]
    for o in range(1, n_off):
        imp_s = imp_s + imp[..., o:o + ratio * n_s:ratio]
    cur = q_pos // SLC_BLOCK
    blk = jnp.arange(n_s)
    valid = blk[None, :] <= cur[:, None]
    forced = (blk[None, :] == 0) | (blk[None, :] == cur[:, None]) | (blk[None, :] == cur[:, None] - 1)
    score = jnp.where(forced, FORCE_SCORE, jnp.where(valid, imp_s, -1.0))
    _, sel = lax.top_k(score, min(SLC_TOPN, n_s))
    n_sel = sel.shape[-1]
    bidx = jnp.arange(bsz)[:, None, None, None]
    gidx = jnp.arange(NSA_GROUPS)[None, :, None, None]
    gk = kslc[bidx, gidx, sel].reshape(bsz, NSA_GROUPS, tq, n_sel * SLC_BLOCK, NSA_HEAD_DIM)
    gv = vslc[bidx, gidx, sel].reshape(bsz, NSA_GROUPS, tq, n_sel * SLC_BLOCK, NSA_HEAD_DIM)
    kpos = (sel[..., None] * SLC_BLOCK + jnp.arange(SLC_BLOCK)).reshape(bsz, NSA_GROUPS, tq, -1)
    dist_s = q_pos[None, None, :, None] - kpos
    mask_s = (dist_s >= 0)[:, :, None]
    tb = rel_table.astype(f32).reshape(REL_BUCKETS, NSA_GROUPS, NSA_HPG)
    bias_s = jnp.moveaxis(tb[rel_bucket(dist_s), gidx], -1, 2)
    s_s = jnp.einsum('btghd,bgtkd->bghtk', qg, gk.astype(f32)) + bias_s
    p_s = jax.nn.softmax(jnp.where(mask_s, s_s, NEG_INF), axis=-1)
    o_s = jnp.einsum('bghtk,bgtkd->btghd', p_s, gv.astype(f32))
    dist_w = q_pos[:, None] - win_pos[None, :]
    mask_w = (dist_w >= 0) & (dist_w <= WINDOW) & (win_pos[None, :] >= 0)
    s_w = jnp.einsum('btghd,bkgd->bghtk', qg, kwin.astype(f32)) + group_bias(rel_table, dist_w)
    p_w = jax.nn.softmax(jnp.where(mask_w, s_w, NEG_INF), axis=-1)
    o_w = jnp.einsum('bghtk,bkgd->btghd', p_w, vwin.astype(f32))
    g = jax.nn.sigmoid(gates.astype(f32)).reshape(bsz, tq, NSA_GROUPS, NSA_HPG, 3)
    o = g[..., 0:1] * o_c + g[..., 1:2] * o_s + g[..., 2:3] * o_w
    return o.reshape(bsz, tq, NSA_Q)


def nsa_mix(q, kc, vc, ks, vs, kw, vw, gates, past, win_buf, wb, cmp_alpha, cmp_w, g_kcmp, rel_table):
    bsz, L = q.shape[:2]
    win_new = jnp.stack([kw, vw], axis=2)
    if past is None:
        pos0 = 0
        kc_all, vc_all, ks_all, vs_all = kc, vc, ks, vs
        win_all = jnp.concatenate([jnp.zeros((bsz, WINDOW) + win_new.shape[2:], win_new.dtype), win_new], axis=1)
    else:
        pos0 = past.shape[1]
        kc_all = jnp.concatenate([past[:, :, 0].astype(kc.dtype), kc], axis=1)
        vc_all = jnp.concatenate([past[:, :, 1].astype(vc.dtype), vc], axis=1)
        ks_all = jnp.concatenate([past[:, :, 2].astype(ks.dtype), ks], axis=1)
        vs_all = jnp.concatenate([past[:, :, 3].astype(vs.dtype), vs], axis=1)
        win_all = jnp.concatenate([win_buf.astype(win_new.dtype), win_new], axis=1)
    new_win = win_all[:, -wb:]
    total = pos0 + L
    kcmp = rmsnorm(compress(kc_all, cmp_alpha[0], cmp_w[0]), g_kcmp)
    vcmp = compress(vc_all, cmp_alpha[1], cmp_w[1])
    cmp_end = jnp.arange(kcmp.shape[1]) * CMP_STRIDE + (CMP_BLOCK - 1)
    n_s = -(-total // SLC_BLOCK)
    pad = n_s * SLC_BLOCK - total

    def to_blocks(t):
        t = jnp.pad(t, ((0, 0), (0, pad), (0, 0), (0, 0)))
        return t.reshape(bsz, n_s, SLC_BLOCK, NSA_GROUPS, NSA_HEAD_DIM).transpose(0, 3, 1, 2, 4)

    kslc, vslc = to_blocks(ks_all), to_blocks(vs_all)
    if past is None:
        def one_block(i):
            s = i * Q_BLOCK
            wkv = lax.dynamic_slice_in_dim(win_all, s, Q_BLOCK + WINDOW, axis=1)
            return nsa_attend(lax.dynamic_slice_in_dim(q, s, Q_BLOCK, axis=1), s + jnp.arange(Q_BLOCK),
                              kcmp, vcmp, cmp_end, kslc, vslc, wkv[:, :, 0], wkv[:, :, 1],
                              s - WINDOW + jnp.arange(Q_BLOCK + WINDOW),
                              lax.dynamic_slice_in_dim(gates, s, Q_BLOCK, axis=1), rel_table)
        o = lax.map(one_block, jnp.arange(L // Q_BLOCK))
        o = jnp.moveaxis(o, 0, 1).reshape(bsz, L, NSA_Q)
    else:
        win_pos = pos0 - wb + jnp.arange(win_all.shape[1])
        o = nsa_attend(q, pos0 + jnp.arange(L), kcmp, vcmp, cmp_end, kslc, vslc,
                       win_all[:, :, 0], win_all[:, :, 1], win_pos, gates, rel_table)
    return o, new_win


def even_mixer(x, past, win_buf, wb, conv_buf, h0, g_mix, w_in, w_out, conv_w, conv_b, dt_bias, a_log,
               d_skip, ssm_g, cmp_alpha, cmp_w, qk_gain, rel_table):
    bsz, L = x.shape[:2]
    h = rmsnorm(x, g_mix) @ w_in
    z, xbc, dt_raw, q, kc, vc, ks, vs, kw, vw, gates = split_cols(h, EVEN_SIZES)
    y_a, new_conv, new_h = mamba_mixer(z, xbc, dt_raw, conv_buf, h0, conv_w, conv_b, dt_bias, a_log, d_skip, ssm_g)

    def heads(t, n):
        return t.reshape(bsz, L, n, NSA_HEAD_DIM)

    q = rmsnorm(heads(q, NSA_HEADS), qk_gain[0]) * (NSA_HEAD_DIM ** -0.5)
    kc, vc, vs, vw = heads(kc, NSA_GROUPS), heads(vc, NSA_GROUPS), heads(vs, NSA_GROUPS), heads(vw, NSA_GROUPS)
    ks = rmsnorm(heads(ks, NSA_GROUPS), qk_gain[2])
    kw = rmsnorm(heads(kw, NSA_GROUPS), qk_gain[3])
    o_b, new_win = nsa_mix(q, kc, vc, ks, vs, kw, vw, gates.reshape(bsz, L, NSA_HEADS, 3), past, win_buf, wb,
                           cmp_alpha, cmp_w, qk_gain[1], rel_table)
    y = jnp.concatenate([y_a.astype(x.dtype), o_b.astype(x.dtype)], axis=-1) @ w_out
    new_kv = jnp.stack([kc, vc, ks, vs], axis=2)
    return y, new_kv, new_win, new_conv, new_h


def pool_mix(u, buf, pos0, pool_w, pool_scale):
    f32 = jnp.float32
    bsz, L = u.shape[:2]
    ext = jnp.concatenate([buf.astype(u.dtype), u], axis=1)
    cs = jnp.cumsum(ext.astype(f32), axis=1)
    cs = jnp.concatenate([jnp.zeros((bsz, 1, POOL_DIM), f32), cs], axis=1)
    pos = pos0 + jnp.arange(L)
    hi = cs[:, POOL_BUF + 1:POOL_BUF + 1 + L]
    outs = []
    for g, w in enumerate(POOL_WINDOWS):
        c0, c1 = g * POOL_GROUP_DIM, (g + 1) * POOL_GROUP_DIM
        lo = cs[:, POOL_BUF + 1 - w:POOL_BUF + 1 - w + L, c0:c1]
        cnt = jnp.minimum(pos + 1, w).astype(f32)[None, :, None]
        diff = (hi[..., c0:c1] - lo) / cnt - u[..., c0:c1].astype(f32)
        outs.append(jnp.einsum('blc,cd->bld', diff, pool_w[g].astype(f32)))
    y = jnp.concatenate(outs, axis=-1) * pool_scale.astype(f32)
    return y, ext[:, -POOL_BUF:]


def sb_attend(q, q_pos, k_segs, v_segs, k_pos):
    f32 = jnp.float32
    bsz, tq = q.shape[:2]
    qf = q.astype(f32) * (SB_HEAD_DIM ** -0.5)
    z = jnp.concatenate([jnp.einsum('bthd,bshd->bhts', qf, k.astype(f32)) for k in k_segs], axis=-1)
    mask = k_pos[None, :] < q_pos[:, None]
    log_stay = jnp.where(mask, jax.nn.log_sigmoid(-z), 0.0)
    suffix = lax.cumsum(log_stay, axis=3, reverse=True) - log_stay
    att = jnp.where(mask, jnp.exp(jax.nn.log_sigmoid(z) + suffix), 0.0)
    out = jnp.zeros((bsz, tq, SB_HEADS, SB_HEAD_DIM), f32)
    off = 0
    for v in v_segs:
        n = v.shape[1]
        out = out + jnp.einsum('bhts,bshd->bthd', att[..., off:off + n], v.astype(f32))
        off += n
    return out.reshape(bsz, tq, SB_DIM)


def odd_mixer(x, past, pool_buf, g_mix, w_in, w_out, pool_w, pool_scale):
    bsz, L = x.shape[:2]
    h = rmsnorm(x, g_mix) @ w_in
    u, q, k, v = split_cols(h, ODD_SIZES)
    pos0 = 0 if past is None else past.shape[1]
    y_c, new_pool = pool_mix(u, pool_buf, pos0, pool_w, pool_scale)
    q = q.reshape(bsz, L, SB_HEADS, SB_HEAD_DIM)
    k = k.reshape(bsz, L, SB_HEADS, SB_HEAD_DIM)
    v = v.reshape(bsz, L, SB_HEADS, SB_HEAD_DIM)
    if past is None:
        pos = jnp.arange(L)

        def one_block(i):
            s = i * Q_BLOCK
            return sb_attend(lax.dynamic_slice_in_dim(q, s, Q_BLOCK, axis=1), s + jnp.arange(Q_BLOCK), (k,), (v,), pos)

        o = lax.map(one_block, jnp.arange(L // Q_BLOCK))
        o = jnp.moveaxis(o, 0, 1).reshape(bsz, L, SB_DIM)
    else:
        o = sb_attend(q, pos0 + jnp.arange(L), (past[:, :, 0], k), (past[:, :, 1], v), jnp.arange(pos0 + L))
    y = jnp.concatenate([y_c.astype(x.dtype), o.astype(x.dtype)], axis=-1) @ w_out
    return y, jnp.stack([k, v], axis=2), new_pool


def setup_inputs(seed: int = 0) -> dict:
    key = jax.random.key(seed)
    ks = jax.random.split(key, 32)
    f32 = jnp.float32

    def nrm(k, shape, scale=1.0):
        return jax.random.normal(k, shape, f32) * scale

    def gain(k, shape):
        return 1.0 + 0.05 * jax.random.normal(k, shape, f32)

    n_pages = PAST_LEN // PAGE_SIZE
    n_used = DEC_BATCH * n_pages
    n_phys = n_used + max(1, n_used // 4)
    page_table = jax.random.permutation(ks[0], n_phys)[:n_used].reshape(DEC_BATCH, n_pages).astype(jnp.int32)
    wb = min(WINDOW, PAST_LEN)
    dt0 = jnp.exp(jax.random.uniform(ks[20], (N_EVEN, SSM_HEADS), f32, math.log(1e-3), math.log(1e-1)))
    return {
        'x_prompt': nrm(ks[1], (BATCH, SEQ, D_MODEL)),
        'x_sample': nrm(ks[2], (DEC_BATCH, DEC_SEQ, D_MODEL)),
        'cache_nsa_kv': nrm(ks[3], (N_EVEN, n_phys, PAGE_SIZE, NSA_KV_PARTS, NSA_GROUPS, NSA_HEAD_DIM)),
        'cache_sb_kv': nrm(ks[4], (N_ODD, n_phys, PAGE_SIZE, 2, SB_HEADS, SB_HEAD_DIM)),
        'state_nsa_win': nrm(ks[5], (N_EVEN, DEC_BATCH, wb, 2, NSA_GROUPS, NSA_HEAD_DIM)),
        'state_ssm': nrm(ks[6], (N_EVEN, DEC_BATCH, SSM_HEADS, SSM_HEAD_DIM, SSM_STATE), 0.5),
        'state_conv': nrm(ks[7], (N_EVEN, DEC_BATCH, SSM_CONV - 1, SSM_CONV_DIM)),
        'state_pool': nrm(ks[8], (N_ODD, DEC_BATCH, POOL_BUF, POOL_DIM)),
        'page_table': page_table,
        'ffn_norm': gain(ks[9], (DEPTH, 2, D_MODEL)),
        'ffn_w_gate': nrm(ks[10], (DEPTH, 2, D_MODEL, D_FF), D_MODEL ** -0.5),
        'ffn_w_up': nrm(ks[11], (DEPTH, 2, D_MODEL, D_FF), D_MODEL ** -0.5),
        'ffn_w_down': nrm(ks[12], (DEPTH, 2, D_FF, D_MODEL), D_FF ** -0.5),
        'mix_norm': gain(ks[13], (DEPTH, D_MODEL)),
        'w_in_even': nrm(ks[14], (N_EVEN, D_MODEL, IN_EVEN), D_MODEL ** -0.5),
        'w_out_even': nrm(ks[15], (N_EVEN, MIX_EVEN, D_MODEL), MIX_EVEN ** -0.5),
        'ssm_conv_w': nrm(ks[16], (N_EVEN, SSM_CONV, SSM_CONV_DIM), SSM_CONV ** -0.5),
        'ssm_conv_b': nrm(ks[17], (N_EVEN, SSM_CONV_DIM), 0.02),
        'ssm_dt_bias': dt0 + jnp.log(-jnp.expm1(-dt0)),
        'ssm_a_log': jnp.log(jax.random.uniform(ks[18], (N_EVEN, SSM_HEADS), f32, 1.0, 16.0)),
        'ssm_d': gain(ks[19], (N_EVEN, SSM_HEADS)),
        'ssm_norm': gain(ks[21], (N_EVEN, SSM_D_INNER)),
        'nsa_cmp_alpha': (CMP_BLOCK ** -0.5) * gain(ks[22], (N_EVEN, 2, CMP_BLOCK, NSA_HEAD_DIM)),
        'nsa_cmp_w': nrm(ks[23], (N_EVEN, 2, NSA_HEAD_DIM, NSA_HEAD_DIM), NSA_HEAD_DIM ** -0.5),
        'nsa_qk_gain': gain(ks[24], (N_EVEN, 4, NSA_HEAD_DIM)),
        'rel_bias': nrm(ks[25], (REL_BUCKETS, NSA_HEADS), 0.5),
        'w_in_odd': nrm(ks[26], (N_ODD, D_MODEL, IN_ODD), D_MODEL ** -0.5),
        'w_out_odd': nrm(ks[27], (N_ODD, MIX_ODD, D_MODEL), MIX_ODD ** -0.5),
        'pool_w': nrm(ks[28], (N_ODD, POOL_GROUPS, POOL_GROUP_DIM, POOL_GROUP_DIM), POOL_GROUP_DIM ** -0.5),
        'pool_scale': gain(ks[29], (N_ODD, POOL_DIM)),
    }


def reference(x_prompt, x_sample, cache_nsa_kv, cache_sb_kv, state_nsa_win, state_ssm, state_conv, state_pool,
              page_table, ffn_norm, ffn_w_gate, ffn_w_up, ffn_w_down, mix_norm, w_in_even, w_out_even,
              ssm_conv_w, ssm_conv_b, ssm_dt_bias, ssm_a_log, ssm_d, ssm_norm, nsa_cmp_alpha, nsa_cmp_w,
              nsa_qk_gain, rel_bias, w_in_odd, w_out_odd, pool_w, pool_scale):
    bp = x_prompt.shape[0]
    bs = x_sample.shape[0]
    past_len = page_table.shape[1] * PAGE_SIZE
    wb = state_nsa_win.shape[2]
    xp, xs = x_prompt, x_sample
    kv_p, kv_s, sb_p, sb_s, win_p, win_s = [], [], [], [], [], []
    ssm_p, ssm_s, conv_p, conv_s, pool_p, pool_s = [], [], [], [], [], []
    for l in range(DEPTH):
        ffn_a = (ffn_norm[l, 0], ffn_w_gate[l, 0], ffn_w_up[l, 0], ffn_w_down[l, 0])
        xp = macaron_half(xp, *ffn_a)
        xs = macaron_half(xs, *ffn_a)
        if l % 2 == 0:
            e = l // 2
            wts = (mix_norm[l], w_in_even[e], w_out_even[e], ssm_conv_w[e], ssm_conv_b[e], ssm_dt_bias[e],
                   ssm_a_log[e], ssm_d[e], ssm_norm[e], nsa_cmp_alpha[e], nsa_cmp_w[e], nsa_qk_gain[e], rel_bias)
            yp, a_kv, a_win, a_conv, a_h = even_mixer(
                xp, None, None, wb, jnp.zeros((bp, SSM_CONV - 1, SSM_CONV_DIM), xp.dtype),
                jnp.zeros((bp, SSM_HEADS, SSM_HEAD_DIM, SSM_STATE), jnp.float32), *wts)
            past = cache_nsa_kv[e, page_table].reshape(bs, past_len, NSA_KV_PARTS, NSA_GROUPS, NSA_HEAD_DIM)
            ys, b_kv, b_win, b_conv, b_h = even_mixer(xs, past, state_nsa_win[e], wb, state_conv[e], state_ssm[e], *wts)
            kv_p.append(a_kv); kv_s.append(b_kv)
            win_p.append(a_win); win_s.append(b_win)
            conv_p.append(a_conv); conv_s.append(b_conv)
            ssm_p.append(a_h); ssm_s.append(b_h)
        else:
            o = l // 2
            wts = (mix_norm[l], w_in_odd[o], w_out_odd[o], pool_w[o], pool_scale[o])
            yp, a_kv, a_pool = odd_mixer(xp, None, jnp.zeros((bp, POOL_BUF, POOL_DIM), xp.dtype), *wts)
            past = cache_sb_kv[o, page_table].reshape(bs, past_len, 2, SB_HEADS, SB_HEAD_DIM)
            ys, b_kv, b_pool = odd_mixer(xs, past, state_pool[o], *wts)
            sb_p.append(a_kv); sb_s.append(b_kv)
            pool_p.append(a_pool); pool_s.append(b_pool)
        xp = xp + yp
        xs = xs + ys
        ffn_b = (ffn_norm[l, 1], ffn_w_gate[l, 1], ffn_w_up[l, 1], ffn_w_down[l, 1])
        xp = macaron_half(xp, *ffn_b)
        xs = macaron_half(xs, *ffn_b)
    return (xp, xs, jnp.stack(kv_p), jnp.stack(kv_s), jnp.stack(sb_p), jnp.stack(sb_s),
            jnp.stack(win_p), jnp.stack(win_s), jnp.stack(ssm_p), jnp.stack(ssm_s),
            jnp.stack(conv_p), jnp.stack(conv_s), jnp.stack(pool_p), jnp.stack(pool_s))
```

```python
import functools
import math

import jax
import jax.numpy as jnp
import numpy as np
from jax import lax
from jax.experimental import pallas as pl
from jax.experimental.pallas import tpu as pltpu

F32 = jnp.float32
BF16 = jnp.bfloat16

D_MODEL = 2048
DEPTH = 4
PAGE_SIZE = 128
EPS = 1e-6
NEG_INF = -1e30
Q_BLOCK = 128
FFN_RESIDUAL = 0.5
D_FF = 5632

SSM_HEADS = 32
SSM_HEAD_DIM = 64
SSM_D_INNER = SSM_HEADS * SSM_HEAD_DIM
SSM_GROUPS = 4
SSM_STATE = 128
SSM_CONV = 4
SSM_CONV_DIM = SSM_D_INNER + 2 * SSM_GROUPS * SSM_STATE
SSM_CHUNK = 128

NSA_HEADS = 16
NSA_GROUPS = 4
NSA_HPG = NSA_HEADS // NSA_GROUPS
NSA_HEAD_DIM = 128
NSA_Q = NSA_HEADS * NSA_HEAD_DIM
NSA_KV = NSA_GROUPS * NSA_HEAD_DIM
NSA_KV_PARTS = 4
CMP_BLOCK = 32
CMP_STRIDE = 16
SLC_BLOCK = 64
SLC_TOPN = 16
WINDOW = 512
FORCE_SCORE = 1e4
REL_BUCKETS = 32
REL_MAX_DIST = 128

POOL_WINDOWS = (2, 4, 8, 16)
POOL_GROUPS = len(POOL_WINDOWS)
POOL_GROUP_DIM = 256
POOL_DIM = POOL_GROUPS * POOL_GROUP_DIM
POOL_BUF = max(POOL_WINDOWS) - 1

SB_HEADS = 16
SB_HEAD_DIM = 128
SB_DIM = SB_HEADS * SB_HEAD_DIM

LANES = 128
VMEM_LIMIT_BYTES = 56 * 1024 * 1024

EVEN_MAIN = SSM_D_INNER + SSM_CONV_DIM + NSA_Q + 6 * NSA_KV
EVEN_TAIL = LANES
OFF_Z = 0
OFF_XBC = SSM_D_INNER
OFF_Q = OFF_XBC + SSM_CONV_DIM
OFF_KV = OFF_Q + NSA_Q
OFF_WIN = OFF_KV + 4 * NSA_KV


def _row_tile(m, want):
    t = min(m, want)
    assert m % t == 0
    return t


def _ffn_body(x_ref, g_ref, wg_ref, wu_ref, wd_ref, o_ref, h_ref, acc_ref):
    j = pl.program_id(1)

    @pl.when(j == 0)
    def _():
        x = x_ref[...]
        y = x * lax.rsqrt(jnp.mean(x * x, axis=-1, keepdims=True) + EPS)
        h_ref[...] = (y * g_ref[...]).astype(BF16)
        acc_ref[...] = jnp.zeros_like(acc_ref)

    h = h_ref[...]
    a = jnp.dot(h, wg_ref[...], preferred_element_type=F32)
    b = jnp.dot(h, wu_ref[...], preferred_element_type=F32)
    t = (a * jax.nn.sigmoid(a)) * b
    acc_ref[...] += jnp.dot(t.astype(BF16), wd_ref[...], preferred_element_type=F32)

    @pl.when(j == pl.num_programs(1) - 1)
    def _():
        o_ref[...] = x_ref[...] + FFN_RESIDUAL * acc_ref[...]


def ffn_half(x, g, wg, wu, wd, *, tm=512, tf=512):
    m, d = x.shape
    f = wg.shape[1]
    tm = _row_tile(m, tm)
    assert f % tf == 0
    return pl.pallas_call(
        _ffn_body,
        grid=(m // tm, f // tf),
        in_specs=[
            pl.BlockSpec((tm, d), lambda i, j: (i, 0)),
            pl.BlockSpec((1, d), lambda i, j: (0, 0)),
            pl.BlockSpec((d, tf), lambda i, j: (0, j)),
            pl.BlockSpec((d, tf), lambda i, j: (0, j)),
            pl.BlockSpec((tf, d), lambda i, j: (j, 0)),
        ],
        out_specs=pl.BlockSpec((tm, d), lambda i, j: (i, 0)),
        out_shape=jax.ShapeDtypeStruct((m, d), F32),
        scratch_shapes=[pltpu.VMEM((tm, d), BF16), pltpu.VMEM((tm, d), F32)],
        compiler_params=pltpu.CompilerParams(
            dimension_semantics=("parallel", "arbitrary"), vmem_limit_bytes=VMEM_LIMIT_BYTES),
        name="ffn_half",
    )(x, g.reshape(1, d), wg, wu, wd)


def _proj_body(flag_ref, x_ref, g_ref, w_ref, pg_ref, ps_ref, *rest, has_tail):
    if has_tail:
        wt_ref, o_ref, ot_ref, h_ref = rest
    else:
        o_ref, h_ref = rest
    j = pl.program_id(1)

    @pl.when(j == 0)
    def _():
        x = x_ref[...]
        y = x * lax.rsqrt(jnp.mean(x * x, axis=-1, keepdims=True) + EPS)
        h = (y * g_ref[...]).astype(BF16)
        h_ref[...] = h
        if has_tail:
            ot_ref[...] = jnp.dot(h, wt_ref[...], preferred_element_type=F32)

    r = jnp.dot(h_ref[...], w_ref[...], preferred_element_type=F32)

    @pl.when(flag_ref[j] == 0)
    def _():
        o_ref[...] = r

    @pl.when(flag_ref[j] != 0)
    def _():
        tm, tn = r.shape
        for c in range(tn // LANES):
            rc = r[:, c * LANES:(c + 1) * LANES]
            yc = rc * lax.rsqrt(jnp.mean(rc * rc, axis=-1, keepdims=True) + EPS)
            yc = (yc * pg_ref[:, c * LANES:(c + 1) * LANES]) * ps_ref[:, c * LANES:(c + 1) * LANES]
            o_ref[:, c * LANES:(c + 1) * LANES] = yc


def in_proj(x, g, w, flags, post_gain, post_scale, w_tail=None, *, tm=512, tn=512):
    m, d = x.shape
    n = w.shape[1]
    tm = _row_tile(m, tm)
    assert n % tn == 0
    has_tail = w_tail is not None
    in_specs = [
        pl.BlockSpec((tm, d), lambda i, j, fl: (i, 0)),
        pl.BlockSpec((1, d), lambda i, j, fl: (0, 0)),
        pl.BlockSpec((d, tn), lambda i, j, fl: (0, j)),
        pl.BlockSpec((1, tn), lambda i, j, fl: (0, j)),
        pl.BlockSpec((1, tn), lambda i, j, fl: (0, j)),
    ]
    out_specs = [pl.BlockSpec((tm, tn), lambda i, j, fl: (i, j))]
    out_shape = [jax.ShapeDtypeStruct((m, n), F32)]
    args = [x, g.reshape(1, d), w, post_gain, post_scale]
    if has_tail:
        nt = w_tail.shape[1]
        in_specs.append(pl.BlockSpec((d, nt), lambda i, j, fl: (0, 0)))
        out_specs.append(pl.BlockSpec((tm, nt), lambda i, j, fl: (i, 0)))
        out_shape.append(jax.ShapeDtypeStruct((m, nt), F32))
        args.append(w_tail)
    res = pl.pallas_call(
        functools.partial(_proj_body, has_tail=has_tail),
        grid_spec=pltpu.PrefetchScalarGridSpec(
            num_scalar_prefetch=1,
            grid=(m // tm, n // tn),
            in_specs=in_specs,
            out_specs=out_specs,
            scratch_shapes=[pltpu.VMEM((tm, d), BF16)],
        ),
        out_shape=out_shape,
        compiler_params=pltpu.CompilerParams(
            dimension_semantics=("parallel", "arbitrary"), vmem_limit_bytes=VMEM_LIMIT_BYTES),
        name="in_proj",
    )(flags, *args)
    return res if has_tail else res[0]


def _out_proj_body(x_ref, a1_ref, a2_ref, w1_ref, w2_ref, o_ref):
    r = jnp.dot(a1_ref[...].astype(BF16), w1_ref[...], preferred_element_type=F32)
    r = r + jnp.dot(a2_ref[...].astype(BF16), w2_ref[...], preferred_element_type=F32)
    o_ref[...] = x_ref[...] + r


def out_proj(x, a1, a2, w1, w2, *, tm=512, tn=512):
    m, d = x.shape
    k1, k2 = a1.shape[1], a2.shape[1]
    tm = _row_tile(m, tm)
    return pl.pallas_call(
        _out_proj_body,
        grid=(m // tm, d // tn),
        in_specs=[
            pl.BlockSpec((tm, tn), lambda i, j: (i, j)),
            pl.BlockSpec((tm, k1), lambda i, j: (i, 0)),
            pl.BlockSpec((tm, k2), lambda i, j: (i, 0)),
            pl.BlockSpec((k1, tn), lambda i, j: (0, j)),
            pl.BlockSpec((k2, tn), lambda i, j: (0, j)),
        ],
        out_specs=pl.BlockSpec((tm, tn), lambda i, j: (i, j)),
        out_shape=jax.ShapeDtypeStruct((m, d), F32),
        compiler_params=pltpu.CompilerParams(
            dimension_semantics=("parallel", "arbitrary"), vmem_limit_bytes=VMEM_LIMIT_BYTES),
        name="out_proj",
    )(x, a1, a2, w1, w2)


def split_cols(h, sizes):
    offs = np.cumsum(sizes)[:-1].tolist()
    return jnp.split(h, offs, axis=-1)


def rmsnorm(x, g):
    xf = x.astype(F32)
    y = xf * lax.rsqrt(jnp.mean(xf * xf, axis=-1, keepdims=True) + EPS)
    return (y * g.astype(F32)).astype(x.dtype)


def rel_bucket(dist):
    n = jnp.maximum(dist, 0)
    exact = REL_BUCKETS // 2
    nf = jnp.maximum(n, 1).astype(F32)
    log_b = exact + (jnp.log(nf / exact) / math.log(REL_MAX_DIST / exact) * (REL_BUCKETS - exact)).astype(jnp.int32)
    return jnp.where(n < exact, n, jnp.minimum(log_b, REL_BUCKETS - 1))


def group_bias(rel_table, dist):
    b = rel_table.astype(F32)[rel_bucket(dist)]
    return jnp.moveaxis(b, -1, 0).reshape((NSA_GROUPS, NSA_HPG) + dist.shape)


def ssd_scan(xh, dt, a, bm, cm, h0):
    bsz, L = xh.shape[:2]
    q = SSM_CHUNK if L % SSM_CHUNK == 0 else L
    nc = L // q
    hpg = SSM_HEADS // SSM_GROUPS
    x = xh.astype(F32).reshape(bsz, nc, q, SSM_GROUPS, hpg, SSM_HEAD_DIM)
    dtc = dt.reshape(bsz, nc, q, SSM_GROUPS, hpg)
    bc = bm.astype(F32).reshape(bsz, nc, q, SSM_GROUPS, SSM_STATE)
    cc = cm.astype(F32).reshape(bsz, nc, q, SSM_GROUPS, SSM_STATE)
    acum = jnp.cumsum(dtc * a.reshape(SSM_GROUPS, hpg), axis=2)
    xdt = x * dtc[..., None]
    causal = jnp.tril(jnp.ones((q, q), dtype=bool))[None, None, :, :, None, None]
    seg = acum[:, :, :, None] - acum[:, :, None, :]
    decay = jnp.exp(jnp.where(causal, seg, -jnp.inf))
    cb = jnp.einsum('bctgn,bcsgn->bctsg', cc, bc)
    y_diag = jnp.einsum('bctsgh,bcsghp->bctghp', cb[..., None] * decay, xdt)
    to_end = jnp.exp(acum[:, :, -1:] - acum)
    states = jnp.einsum('bcsgn,bcsghp->bcghpn', bc, xdt * to_end[..., None])
    chunk_decay = jnp.exp(acum[:, :, -1])

    def step(h, inp):
        st, dc = inp
        return h * dc[..., None, None] + st, h

    h_init = h0.astype(F32).reshape(bsz, SSM_GROUPS, hpg, SSM_HEAD_DIM, SSM_STATE)
    h_fin, h_start = lax.scan(step, h_init, (jnp.moveaxis(states, 1, 0), jnp.moveaxis(chunk_decay, 1, 0)))
    h_start = jnp.moveaxis(h_start, 0, 1)
    y_off = jnp.einsum('bctgn,bcghpn->bctghp', cc, h_start) * jnp.exp(acum)[..., None]
    y = (y_diag + y_off).reshape(bsz, L, SSM_HEADS, SSM_HEAD_DIM)
    return y, h_fin.reshape(bsz, SSM_HEADS, SSM_HEAD_DIM, SSM_STATE)


def mamba_mixer(z, xbc, dt_raw, conv_buf, h0, conv_w, conv_b, dt_bias, a_log, d_skip, norm_g):
    bsz, L = xbc.shape[:2]
    ext = jnp.concatenate([conv_buf.astype(xbc.dtype), xbc], axis=1)
    acc = conv_b
    for k in range(SSM_CONV):
        acc = acc + ext[:, k:k + L] * conv_w[k]
    u = jax.nn.silu(acc)
    xs, bm, cm = split_cols(u, (SSM_D_INNER, SSM_GROUPS * SSM_STATE, SSM_GROUPS * SSM_STATE))
    xh = xs.reshape(bsz, L, SSM_HEADS, SSM_HEAD_DIM)
    dt = jax.nn.softplus(dt_raw.astype(F32) + dt_bias.astype(F32))
    a = -jnp.exp(a_log.astype(F32))
    y, h_new = ssd_scan(xh, dt, a, bm.reshape(bsz, L, SSM_GROUPS, SSM_STATE),
                        cm.reshape(bsz, L, SSM_GROUPS, SSM_STATE), h0)
    y = y + d_skip.astype(F32)[:, None] * xh.astype(F32)
    y = y.reshape(bsz, L, SSM_D_INNER) * jax.nn.silu(z.astype(F32))
    y = rmsnorm(y.reshape(bsz, L, SSM_GROUPS, -1), norm_g.reshape(SSM_GROUPS, -1)).reshape(bsz, L, SSM_D_INNER)
    return y, ext[:, L:], h_new


def compress(k, alpha, w):
    bsz, L = k.shape[:2]
    n_sub = L // CMP_STRIDE
    sub = k[:, :n_sub * CMP_STRIDE].astype(F32).reshape(bsz, n_sub, CMP_STRIDE, NSA_GROUPS, NSA_HEAD_DIM)
    alpha = alpha.astype(F32)
    first = jnp.einsum('bnrgd,rd->bngd', sub, alpha[:CMP_STRIDE])
    second = jnp.einsum('bnrgd,rd->bngd', sub, alpha[CMP_STRIDE:])
    pooled = first[:, :-1] + second[:, 1:]
    return jnp.einsum('bngd,de->bnge', pooled, w.astype(F32))


def nsa_attend(q, q_pos, kcmp, vcmp, cmp_end, kslc, vslc, kwin, vwin, win_pos, gates, rel_table):
    bsz, tq = q.shape[:2]
    qg = q.astype(F32).reshape(bsz, tq, NSA_GROUPS, NSA_HPG, NSA_HEAD_DIM)
    dist_c = q_pos[:, None] - cmp_end[None, :]
    mask_c = dist_c >= 0
    s_c = jnp.einsum('btghd,bngd->bghtn', qg, kcmp.astype(F32)) + group_bias(rel_table, dist_c)
    p_c = jax.nn.softmax(jnp.where(mask_c, s_c, NEG_INF), axis=-1) * mask_c
    o_c = jnp.einsum('bghtn,bngd->btghd', p_c, vcmp.astype(F32))
    n_c = kcmp.shape[1]
    n_s = kslc.shape[2]
    ratio = SLC_BLOCK // CMP_STRIDE
    lo = CMP_BLOCK // CMP_STRIDE - 1
    n_off = ratio + lo
    imp = jnp.pad(p_c.sum(axis=2), ((0, 0), (0, 0), (0, 0), (lo, ratio * n_s + n_off - lo - n_c)))
    imp_s = imp[..., 0:ratio * n_s:ratio]
    for o in range(1, n_off):
        imp_s = imp_s + imp[..., o:o + ratio * n_s:ratio]
    cur = q_pos // SLC_BLOCK
    blk = jnp.arange(n_s)
    valid = blk[None, :] <= cur[:, None]
    forced = (blk[None, :] == 0) | (blk[None, :] == cur[:, None]) | (blk[None, :] == cur[:, None] - 1)
    score = jnp.where(forced, FORCE_SCORE, jnp.where(valid, imp_s, -1.0))
    _, sel = lax.top_k(score, min(SLC_TOPN, n_s))
    n_sel = sel.shape[-1]
    bidx = jnp.arange(bsz)[:, None, None, None]
    gidx = jnp.arange(NSA_GROUPS)[None, :, None, None]
    gk = kslc[bidx, gidx, sel].reshape(bsz, NSA_GROUPS, tq, n_sel * SLC_BLOCK, NSA_HEAD_DIM)
    gv = vslc[bidx, gidx, sel].reshape(bsz, NSA_GROUPS, tq, n_sel * SLC_BLOCK, NSA_HEAD_DIM)
    kpos = (sel[..., None] * SLC_BLOCK + jnp.arange(SLC_BLOCK)).reshape(bsz, NSA_GROUPS, tq, -1)
    dist_s = q_pos[None, None, :, None] - kpos
    mask_s = (dist_s >= 0)[:, :, None]
    tb = rel_table.astype(F32).reshape(REL_BUCKETS, NSA_GROUPS, NSA_HPG)
    bias_s = jnp.moveaxis(tb[rel_bucket(dist_s), gidx], -1, 2)
    s_s = jnp.einsum('btghd,bgtkd->bghtk', qg, gk.astype(F32)) + bias_s
    p_s = jax.nn.softmax(jnp.where(mask_s, s_s, NEG_INF), axis=-1)
    o_s = jnp.einsum('bghtk,bgtkd->btghd', p_s, gv.astype(F32))
    dist_w = q_pos[:, None] - win_pos[None, :]
    mask_w = (dist_w >= 0) & (dist_w <= WINDOW) & (win_pos[None, :] >= 0)
    s_w = jnp.einsum('btghd,bkgd->bghtk', qg, kwin.astype(F32)) + group_bias(rel_table, dist_w)
    p_w = jax.nn.softmax(jnp.where(mask_w, s_w, NEG_INF), axis=-1)
    o_w = jnp.einsum('bghtk,bkgd->btghd', p_w, vwin.astype(F32))
    g = jax.nn.sigmoid(gates.astype(F32)).reshape(bsz, tq, NSA_GROUPS, NSA_HPG, 3)
    o = g[..., 0:1] * o_c + g[..., 1:2] * o_s + g[..., 2:3] * o_w
    return o.reshape(bsz, tq, NSA_Q)


def nsa_mix(q, kc, vc, ks, vs, kw, vw, gates, past, win_buf, wb, cmp_alpha, cmp_w, g_kcmp, rel_table):
    bsz, L = q.shape[:2]
    win_new = jnp.stack([kw, vw], axis=2)
    if past is None:
        pos0 = 0
        kc_all, vc_all, ks_all, vs_all = kc, vc, ks, vs
        win_all = jnp.concatenate([jnp.zeros((bsz, WINDOW) + win_new.shape[2:], win_new.dtype), win_new], axis=1)
    else:
        pos0 = past.shape[1]
        kc_all = jnp.concatenate([past[:, :, 0].astype(kc.dtype), kc], axis=1)
        vc_all = jnp.concatenate([past[:, :, 1].astype(vc.dtype), vc], axis=1)
        ks_all = jnp.concatenate([past[:, :, 2].astype(ks.dtype), ks], axis=1)
        vs_all = jnp.concatenate([past[:, :, 3].astype(vs.dtype), vs], axis=1)
        win_all = jnp.concatenate([win_buf.astype(win_new.dtype), win_new], axis=1)
    new_win = win_all[:, -wb:]
    total = pos0 + L
    kcmp = rmsnorm(compress(kc_all, cmp_alpha[0], cmp_w[0]), g_kcmp)
    vcmp = compress(vc_all, cmp_alpha[1], cmp_w[1])
    cmp_end = jnp.arange(kcmp.shape[1]) * CMP_STRIDE + (CMP_BLOCK - 1)
    n_s = -(-total // SLC_BLOCK)
    pad = n_s * SLC_BLOCK - total

    def to_blocks(t):
        t = jnp.pad(t, ((0, 0), (0, pad), (0, 0), (0, 0)))
        return t.reshape(bsz, n_s, SLC_BLOCK, NSA_GROUPS, NSA_HEAD_DIM).transpose(0, 3, 1, 2, 4)

    kslc, vslc = to_blocks(ks_all), to_blocks(vs_all)
    if past is None:
        def one_block(i):
            s = i * Q_BLOCK
            wkv = lax.dynamic_slice_in_dim(win_all, s, Q_BLOCK + WINDOW, axis=1)
            return nsa_attend(lax.dynamic_slice_in_dim(q, s, Q_BLOCK, axis=1), s + jnp.arange(Q_BLOCK),
                              kcmp, vcmp, cmp_end, kslc, vslc, wkv[:, :, 0], wkv[:, :, 1],
                              s - WINDOW + jnp.arange(Q_BLOCK + WINDOW),
                              lax.dynamic_slice_in_dim(gates, s, Q_BLOCK, axis=1), rel_table)
        o = lax.map(one_block, jnp.arange(L // Q_BLOCK))
        o = jnp.moveaxis(o, 0, 1).reshape(bsz, L, NSA_Q)
    else:
        win_pos = pos0 - wb + jnp.arange(win_all.shape[1])
        o = nsa_attend(q, pos0 + jnp.arange(L), kcmp, vcmp, cmp_end, kslc, vslc,
                       win_all[:, :, 0], win_all[:, :, 1], win_pos, gates, rel_table)
    return o, new_win


def pool_mix(u, buf, pos0, pool_w, pool_scale):
    bsz, L = u.shape[:2]
    ext = jnp.concatenate([buf.astype(u.dtype), u], axis=1)
    cs = jnp.cumsum(ext.astype(F32), axis=1)
    cs = jnp.concatenate([jnp.zeros((bsz, 1, POOL_DIM), F32), cs], axis=1)
    pos = pos0 + jnp.arange(L)
    hi = cs[:, POOL_BUF + 1:POOL_BUF + 1 + L]
    outs = []
    for g, w in enumerate(POOL_WINDOWS):
        c0, c1 = g * POOL_GROUP_DIM, (g + 1) * POOL_GROUP_DIM
        lo = cs[:, POOL_BUF + 1 - w:POOL_BUF + 1 - w + L, c0:c1]
        cnt = jnp.minimum(pos + 1, w).astype(F32)[None, :, None]
        diff = (hi[..., c0:c1] - lo) / cnt - u[..., c0:c1].astype(F32)
        outs.append(jnp.einsum('blc,cd->bld', diff, pool_w[g].astype(F32)))
    y = jnp.concatenate(outs, axis=-1) * pool_scale.astype(F32)
    return y, ext[:, -POOL_BUF:]


def sb_attend(q, q_pos, k_segs, v_segs, k_pos):
    bsz, tq = q.shape[:2]
    qf = q.astype(F32) * (SB_HEAD_DIM ** -0.5)
    z = jnp.concatenate([jnp.einsum('bthd,bshd->bhts', qf, k.astype(F32)) for k in k_segs], axis=-1)
    mask = k_pos[None, :] < q_pos[:, None]
    log_stay = jnp.where(mask, jax.nn.log_sigmoid(-z), 0.0)
    suffix = lax.cumsum(log_stay, axis=3, reverse=True) - log_stay
    att = jnp.where(mask, jnp.exp(jax.nn.log_sigmoid(z) + suffix), 0.0)
    out = jnp.zeros((bsz, tq, SB_HEADS, SB_HEAD_DIM), F32)
    off = 0
    for v in v_segs:
        n = v.shape[1]
        out = out + jnp.einsum('bhts,bshd->bthd', att[..., off:off + n], v.astype(F32))
        off += n
    return out.reshape(bsz, tq, SB_DIM)


def even_mixer(x, past, win_buf, wb, conv_buf, h0, pw):
    bsz, L, _ = x.shape
    m = bsz * L
    main, tail = in_proj(x.reshape(m, D_MODEL), pw['g_mix'], pw['w_main'], pw['flags'], pw['post_gain'],
                         pw['post_scale'], pw['w_tail'])
    main = main.reshape(bsz, L, EVEN_MAIN)
    z = main[..., OFF_Z:OFF_XBC]
    xbc = main[..., OFF_XBC:OFF_Q]
    q = main[..., OFF_Q:OFF_KV].reshape(bsz, L, NSA_HEADS, NSA_HEAD_DIM)
    kv = main[..., OFF_KV:OFF_WIN].reshape(bsz, L, 4, NSA_GROUPS, NSA_HEAD_DIM)
    win = main[..., OFF_WIN:].reshape(bsz, L, 2, NSA_GROUPS, NSA_HEAD_DIM)
    tail = tail.reshape(bsz, L, EVEN_TAIL)
    dt_raw = tail[..., :SSM_HEADS]
    gates = tail[..., SSM_HEADS:SSM_HEADS + 3 * NSA_HEADS]
    y_a, new_conv, new_h = mamba_mixer(z, xbc, dt_raw, conv_buf, h0, pw['conv_w'], pw['conv_b'], pw['dt_bias'],
                                       pw['a_log'], pw['d_skip'], pw['ssm_g'])
    o_b, new_win = nsa_mix(q, kv[:, :, 0], kv[:, :, 1], kv[:, :, 2], kv[:, :, 3], win[:, :, 0], win[:, :, 1],
                           gates.reshape(bsz, L, NSA_HEADS, 3), past, win_buf, wb,
                           pw['cmp_alpha'], pw['cmp_w'], pw['g_kcmp'], pw['rel_table'])
    y = out_proj(x.reshape(m, D_MODEL), y_a.reshape(m, SSM_D_INNER), o_b.reshape(m, NSA_Q),
                 pw['w_out_a'], pw['w_out_b'])
    return y.reshape(bsz, L, D_MODEL), kv, new_win, new_conv, new_h


def odd_mixer(x, past, pool_buf, pw):
    bsz, L, _ = x.shape
    m = bsz * L
    h = in_proj(x.reshape(m, D_MODEL), pw['g_mix'], pw['w_in'], pw['flags'], pw['post_gain'], pw['post_scale'])
    h = h.reshape(bsz, L, -1)
    u, q, k, v = split_cols(h, (POOL_DIM, SB_DIM, SB_DIM, SB_DIM))
    pos0 = 0 if past is None else past.shape[1]
    y_c, new_pool = pool_mix(u, pool_buf, pos0, pw['pool_w'], pw['pool_scale'])
    q = q.reshape(bsz, L, SB_HEADS, SB_HEAD_DIM)
    k = k.reshape(bsz, L, SB_HEADS, SB_HEAD_DIM)
    v = v.reshape(bsz, L, SB_HEADS, SB_HEAD_DIM)
    if past is None:
        pos = jnp.arange(L)

        def one_block(i):
            s = i * Q_BLOCK
            return sb_attend(lax.dynamic_slice_in_dim(q, s, Q_BLOCK, axis=1), s + jnp.arange(Q_BLOCK), (k,), (v,), pos)

        o = lax.map(one_block, jnp.arange(L // Q_BLOCK))
        o = jnp.moveaxis(o, 0, 1).reshape(bsz, L, SB_DIM)
    else:
        o = sb_attend(q, pos0 + jnp.arange(L), (past[:, :, 0], k), (past[:, :, 1], v), jnp.arange(pos0 + L))
    y = out_proj(x.reshape(m, D_MODEL), y_c.reshape(m, POOL_DIM), o.reshape(m, SB_DIM), pw['w_out_a'], pw['w_out_b'])
    return y.reshape(bsz, L, D_MODEL), jnp.stack([k, v], axis=2), new_pool


def _even_weights(l, e, mix_norm, w_in_even, w_out_even, ssm_conv_w, ssm_conv_b, ssm_dt_bias, ssm_a_log, ssm_d,
                  ssm_norm, nsa_cmp_alpha, nsa_cmp_w, nsa_qk_gain, rel_bias, tn):
    w = w_in_even[e]
    sizes = (SSM_D_INNER, SSM_CONV_DIM, SSM_HEADS, NSA_Q) + (NSA_KV,) * 6 + (3 * NSA_HEADS,)
    wz, wxbc, wdt, wq, wkc, wvc, wks, wvs, wkw, wvw, wg = split_cols(w, sizes)
    w_main = jnp.concatenate([wz, wxbc, wq, wkc, wvc, wks, wvs, wkw, wvw], axis=1).astype(BF16)
    pad = EVEN_TAIL - SSM_HEADS - 3 * NSA_HEADS
    w_tail = jnp.concatenate([wdt, wg, jnp.zeros((D_MODEL, pad), F32)], axis=1).astype(BF16)
    gain = nsa_qk_gain[e]
    ones = jnp.ones((NSA_KV,), F32)
    post_gain = jnp.concatenate([
        jnp.ones((OFF_Q,), F32), jnp.tile(gain[0], NSA_HEADS), ones, ones, jnp.tile(gain[2], NSA_GROUPS), ones,
        jnp.tile(gain[3], NSA_GROUPS), ones]).reshape(1, EVEN_MAIN)
    post_scale = jnp.concatenate([
        jnp.ones((OFF_Q,), F32), jnp.full((NSA_Q,), NSA_HEAD_DIM ** -0.5, F32),
        jnp.ones((6 * NSA_KV,), F32)]).reshape(1, EVEN_MAIN)
    flags = np.zeros((EVEN_MAIN // tn,), np.int32)
    for lo_, hi_ in ((OFF_Q, OFF_KV), (OFF_KV + 2 * NSA_KV, OFF_KV + 3 * NSA_KV), (OFF_WIN, OFF_WIN + NSA_KV)):
        assert lo_ % tn == 0 and hi_ % tn == 0
        flags[lo_ // tn:hi_ // tn] = 1
    wo = w_out_even[e].astype(BF16)
    return dict(g_mix=mix_norm[l], w_main=w_main, w_tail=w_tail, flags=jnp.asarray(flags), post_gain=post_gain,
                post_scale=post_scale, conv_w=ssm_conv_w[e], conv_b=ssm_conv_b[e], dt_bias=ssm_dt_bias[e],
                a_log=ssm_a_log[e], d_skip=ssm_d[e], ssm_g=ssm_norm[e], cmp_alpha=nsa_cmp_alpha[e],
                cmp_w=nsa_cmp_w[e], g_kcmp=gain[1], rel_table=rel_bias,
                w_out_a=wo[:SSM_D_INNER], w_out_b=wo[SSM_D_INNER:])


def _odd_weights(l, o, mix_norm, w_in_odd, w_out_odd, pool_w, pool_scale, tn):
    n = w_in_odd.shape[2]
    wo = w_out_odd[o].astype(BF16)
    return dict(g_mix=mix_norm[l], w_in=w_in_odd[o].astype(BF16), flags=jnp.zeros((n // tn,), jnp.int32),
                post_gain=jnp.ones((1, n), F32), post_scale=jnp.ones((1, n), F32),
                pool_w=pool_w[o], pool_scale=pool_scale[o], w_out_a=wo[:POOL_DIM], w_out_b=wo[POOL_DIM:])


def kernel(x_prompt, x_sample, cache_nsa_kv, cache_sb_kv, state_nsa_win, state_ssm, state_conv, state_pool, page_table, ffn_norm, ffn_w_gate, ffn_w_up, ffn_w_down, mix_norm, w_in_even, w_out_even, ssm_conv_w, ssm_conv_b, ssm_dt_bias, ssm_a_log, ssm_d, ssm_norm, nsa_cmp_alpha, nsa_cmp_w, nsa_qk_gain, rel_bias, w_in_odd, w_out_odd, pool_w, pool_scale):
    bp, lp, _ = x_prompt.shape
    bs, ls, _ = x_sample.shape
    past_len = page_table.shape[1] * PAGE_SIZE
    wb = state_nsa_win.shape[2]
    tn = 512
    xp = x_prompt.reshape(bp * lp, D_MODEL)
    xs = x_sample.reshape(bs * ls, D_MODEL)
    wg_all = ffn_w_gate.astype(BF16)
    wu_all = ffn_w_up.astype(BF16)
    wd_all = ffn_w_down.astype(BF16)
    kv_p, kv_s, sb_p, sb_s, win_p, win_s = [], [], [], [], [], []
    ssm_p, ssm_s, conv_p, conv_s, pool_p, pool_s = [], [], [], [], [], []
    for l in range(DEPTH):
        fa = (ffn_norm[l, 0], wg_all[l, 0], wu_all[l, 0], wd_all[l, 0])
        xp = ffn_half(xp, *fa)
        xs = ffn_half(xs, *fa)
        xp3 = xp.reshape(bp, lp, D_MODEL)
        xs3 = xs.reshape(bs, ls, D_MODEL)
        if l % 2 == 0:
            e = l // 2
            pw = _even_weights(l, e, mix_norm, w_in_even, w_out_even, ssm_conv_w, ssm_conv_b, ssm_dt_bias,
                               ssm_a_log, ssm_d, ssm_norm, nsa_cmp_alpha, nsa_cmp_w, nsa_qk_gain, rel_bias, tn)
            xp3, a_kv, a_win, a_conv, a_h = even_mixer(
                xp3, None, None, wb, jnp.zeros((bp, SSM_CONV - 1, SSM_CONV_DIM), F32),
                jnp.zeros((bp, SSM_HEADS, SSM_HEAD_DIM, SSM_STATE), F32), pw)
            past = cache_nsa_kv[e, page_table].reshape(bs, past_len, NSA_KV_PARTS, NSA_GROUPS, NSA_HEAD_DIM)
            xs3, b_kv, b_win, b_conv, b_h = even_mixer(xs3, past, state_nsa_win[e], wb, state_conv[e], state_ssm[e], pw)
            kv_p.append(a_kv); kv_s.append(b_kv)
            win_p.append(a_win); win_s.append(b_win)
            conv_p.append(a_conv); conv_s.append(b_conv)
            ssm_p.append(a_h); ssm_s.append(b_h)
        else:
            o = l // 2
            pw = _odd_weights(l, o, mix_norm, w_in_odd, w_out_odd, pool_w, pool_scale, tn)
            xp3, a_kv, a_pool = odd_mixer(xp3, None, jnp.zeros((bp, POOL_BUF, POOL_DIM), F32), pw)
            past = cache_sb_kv[o, page_table].reshape(bs, past_len, 2, SB_HEADS, SB_HEAD_DIM)
            xs3, b_kv, b_pool = odd_mixer(xs3, past, state_pool[o], pw)
            sb_p.append(a_kv); sb_s.append(b_kv)
            pool_p.append(a_pool); pool_s.append(b_pool)
        xp = xp3.reshape(bp * lp, D_MODEL)
        xs = xs3.reshape(bs * ls, D_MODEL)
        fb = (ffn_norm[l, 1], wg_all[l, 1], wu_all[l, 1], wd_all[l, 1])
        xp = ffn_half(xp, *fb)
        xs = ffn_half(xs, *fb)
    return (xp.reshape(bp, lp, D_MODEL), xs.reshape(bs, ls, D_MODEL), jnp.stack(kv_p), jnp.stack(kv_s),
            jnp.stack(sb_p), jnp.stack(sb_s), jnp.stack(win_p), jnp.stack(win_s), jnp.stack(ssm_p), jnp.stack(ssm_s),
            jnp.stack(conv_p), jnp.stack(conv_s), jnp.stack(pool_p), jnp.stack(pool_s))
```

```python
import functools
import math

import jax
import jax.numpy as jnp
import numpy as np
from jax import lax
from jax.experimental import pallas as pl
from jax.experimental.pallas import tpu as pltpu

F32 = jnp.float32
BF16 = jnp.bfloat16

D_MODEL = 2048
DEPTH = 4
PAGE_SIZE = 128
EPS = 1e-6
NEG_INF = -1e30
Q_BLOCK = 128
FFN_RESIDUAL = 0.5
D_FF = 5632

SSM_HEADS = 32
SSM_HEAD_DIM = 64
SSM_D_INNER = SSM_HEADS * SSM_HEAD_DIM
SSM_GROUPS = 4
SSM_STATE = 128
SSM_CONV = 4
SSM_CONV_DIM = SSM_D_INNER + 2 * SSM_GROUPS * SSM_STATE
SSM_CHUNK = 128

NSA_HEADS = 16
NSA_GROUPS = 4
NSA_HPG = NSA_HEADS // NSA_GROUPS
NSA_HEAD_DIM = 128
NSA_Q = NSA_HEADS * NSA_HEAD_DIM
NSA_KV = NSA_GROUPS * NSA_HEAD_DIM
NSA_KV_PARTS = 4
CMP_BLOCK = 32
CMP_STRIDE = 16
SLC_BLOCK = 64
SLC_TOPN = 16
WINDOW = 512
FORCE_SCORE = 1e4
REL_BUCKETS = 32
REL_MAX_DIST = 128

POOL_WINDOWS = (2, 4, 8, 16)
POOL_GROUPS = len(POOL_WINDOWS)
POOL_GROUP_DIM = 256
POOL_DIM = POOL_GROUPS * POOL_GROUP_DIM
POOL_BUF = max(POOL_WINDOWS) - 1

SB_HEADS = 16
SB_HEAD_DIM = 128
SB_DIM = SB_HEADS * SB_HEAD_DIM

LANES = 128
VMEM_LIMIT_BYTES = 56 * 1024 * 1024

EVEN_MAIN = SSM_D_INNER + SSM_CONV_DIM + NSA_Q + 6 * NSA_KV
EVEN_TAIL = LANES
OFF_Z = 0
OFF_XBC = SSM_D_INNER
OFF_Q = OFF_XBC + SSM_CONV_DIM
OFF_KV = OFF_Q + NSA_Q
OFF_WIN = OFF_KV + 4 * NSA_KV


def _row_tile(m, want):
    t = min(m, want)
    assert m % t == 0
    return t


def _ffn_body(x_ref, g_ref, wg_ref, wu_ref, wd_ref, o_ref, h_ref, acc_ref):
    j = pl.program_id(1)

    @pl.when(j == 0)
    def _():
        x = x_ref[...]
        y = x * lax.rsqrt(jnp.mean(x * x, axis=-1, keepdims=True) + EPS)
        h_ref[...] = (y * g_ref[...]).astype(BF16)
        acc_ref[...] = jnp.zeros_like(acc_ref)

    h = h_ref[...]
    a = jnp.dot(h, wg_ref[...], preferred_element_type=F32)
    b = jnp.dot(h, wu_ref[...], preferred_element_type=F32)
    t = (a * jax.nn.sigmoid(a)) * b
    acc_ref[...] += jnp.dot(t.astype(BF16), wd_ref[...], preferred_element_type=F32)

    @pl.when(j == pl.num_programs(1) - 1)
    def _():
        o_ref[...] = x_ref[...] + FFN_RESIDUAL * acc_ref[...]


def ffn_half(x, g, wg, wu, wd, *, tm=512, tf=512):
    m, d = x.shape
    f = wg.shape[1]
    tm = _row_tile(m, tm)
    assert f % tf == 0
    return pl.pallas_call(
        _ffn_body,
        grid=(m // tm, f // tf),
        in_specs=[
            pl.BlockSpec((tm, d), lambda i, j: (i, 0)),
            pl.BlockSpec((1, d), lambda i, j: (0, 0)),
            pl.BlockSpec((d, tf), lambda i, j: (0, j)),
            pl.BlockSpec((d, tf), lambda i, j: (0, j)),
            pl.BlockSpec((tf, d), lambda i, j: (j, 0)),
        ],
        out_specs=pl.BlockSpec((tm, d), lambda i, j: (i, 0)),
        out_shape=jax.ShapeDtypeStruct((m, d), F32),
        scratch_shapes=[pltpu.VMEM((tm, d), BF16), pltpu.VMEM((tm, d), F32)],
        compiler_params=pltpu.CompilerParams(
            dimension_semantics=("parallel", "arbitrary"), vmem_limit_bytes=VMEM_LIMIT_BYTES),
        name="ffn_half",
    )(x, g.reshape(1, d), wg, wu, wd)


def _proj_body(flag_ref, x_ref, g_ref, w_ref, pg_ref, ps_ref, *rest, has_tail):
    if has_tail:
        wt_ref, o_ref, ot_ref, h_ref = rest
    else:
        o_ref, h_ref = rest
    j = pl.program_id(1)

    @pl.when(j == 0)
    def _():
        x = x_ref[...]
        y = x * lax.rsqrt(jnp.mean(x * x, axis=-1, keepdims=True) + EPS)
        h = (y * g_ref[...]).astype(BF16)
        h_ref[...] = h
        if has_tail:
            ot_ref[...] = jnp.dot(h, wt_ref[...], preferred_element_type=F32)

    r = jnp.dot(h_ref[...], w_ref[...], preferred_element_type=F32)

    @pl.when(flag_ref[j] == 0)
    def _():
        o_ref[...] = r

    @pl.when(flag_ref[j] != 0)
    def _():
        tm, tn = r.shape
        for c in range(tn // LANES):
            rc = r[:, c * LANES:(c + 1) * LANES]
            yc = rc * lax.rsqrt(jnp.mean(rc * rc, axis=-1, keepdims=True) + EPS)
            yc = (yc * pg_ref[:, c * LANES:(c + 1) * LANES]) * ps_ref[:, c * LANES:(c + 1) * LANES]
            o_ref[:, c * LANES:(c + 1) * LANES] = yc


def in_proj(x, g, w, flags, post_gain, post_scale, w_tail=None, *, tm=512, tn=512):
    m, d = x.shape
    n = w.shape[1]
    tm = _row_tile(m, tm)
    assert n % tn == 0
    has_tail = w_tail is not None
    in_specs = [
        pl.BlockSpec((tm, d), lambda i, j, fl: (i, 0)),
        pl.BlockSpec((1, d), lambda i, j, fl: (0, 0)),
        pl.BlockSpec((d, tn), lambda i, j, fl: (0, j)),
        pl.BlockSpec((1, tn), lambda i, j, fl: (0, j)),
        pl.BlockSpec((1, tn), lambda i, j, fl: (0, j)),
    ]
    out_specs = [pl.BlockSpec((tm, tn), lambda i, j, fl: (i, j))]
    out_shape = [jax.ShapeDtypeStruct((m, n), F32)]
    args = [x, g.reshape(1, d), w, post_gain, post_scale]
    if has_tail:
        nt = w_tail.shape[1]
        in_specs.append(pl.BlockSpec((d, nt), lambda i, j, fl: (0, 0)))
        out_specs.append(pl.BlockSpec((tm, nt), lambda i, j, fl: (i, 0)))
        out_shape.append(jax.ShapeDtypeStruct((m, nt), F32))
        args.append(w_tail)
    res = pl.pallas_call(
        functools.partial(_proj_body, has_tail=has_tail),
        grid_spec=pltpu.PrefetchScalarGridSpec(
            num_scalar_prefetch=1,
            grid=(m // tm, n // tn),
            in_specs=in_specs,
            out_specs=out_specs,
            scratch_shapes=[pltpu.VMEM((tm, d), BF16)],
        ),
        out_shape=out_shape,
        compiler_params=pltpu.CompilerParams(
            dimension_semantics=("parallel", "arbitrary"), vmem_limit_bytes=VMEM_LIMIT_BYTES),
        name="in_proj",
    )(flags, *args)
    return res if has_tail else res[0]


def _out_proj_body(x_ref, a1_ref, a2_ref, w1_ref, w2_ref, o_ref):
    r = jnp.dot(a1_ref[...].astype(BF16), w1_ref[...], preferred_element_type=F32)
    r = r + jnp.dot(a2_ref[...].astype(BF16), w2_ref[...], preferred_element_type=F32)
    o_ref[...] = x_ref[...] + r


def out_proj(x, a1, a2, w1, w2, *, tm=512, tn=512):
    m, d = x.shape
    k1, k2 = a1.shape[1], a2.shape[1]
    tm = _row_tile(m, tm)
    return pl.pallas_call(
        _out_proj_body,
        grid=(m // tm, d // tn),
        in_specs=[
            pl.BlockSpec((tm, tn), lambda i, j: (i, j)),
            pl.BlockSpec((tm, k1), lambda i, j: (i, 0)),
            pl.BlockSpec((tm, k2), lambda i, j: (i, 0)),
            pl.BlockSpec((k1, tn), lambda i, j: (0, j)),
            pl.BlockSpec((k2, tn), lambda i, j: (0, j)),
        ],
        out_specs=pl.BlockSpec((tm, tn), lambda i, j: (i, j)),
        out_shape=jax.ShapeDtypeStruct((m, d), F32),
        compiler_params=pltpu.CompilerParams(
            dimension_semantics=("parallel", "arbitrary"), vmem_limit_bytes=VMEM_LIMIT_BYTES),
        name="out_proj",
    )(x, a1, a2, w1, w2)


def _dot_nt(a, b):
    return lax.dot_general(a, b, (((1,), (1,)), ((), ())), preferred_element_type=F32)


def _dot(a, b):
    return jnp.dot(a, b, preferred_element_type=F32)


def _softplus(z):
    return jnp.maximum(z, 0.0) + jnp.log1p(jnp.exp(-jnp.abs(z)))


def _sb_prompt_body(q_ref, k_ref, v_ref, tri_ref, o_ref):
    i = pl.program_id(2)
    blk = Q_BLOCK
    q = (q_ref[...] * (SB_HEAD_DIM ** -0.5)).astype(BF16)
    tri = tri_ref[...]
    tt = lax.broadcasted_iota(jnp.int32, (blk, blk), 0)
    ss = lax.broadcasted_iota(jnp.int32, (blk, blk), 1)
    strict = ss < tt

    def chunk(kb, run, acc, diag):
        r0 = pl.multiple_of(kb * blk, blk)
        k = k_ref[pl.ds(r0, blk), :].astype(BF16)
        z = _dot_nt(q, k)
        sp = _softplus(z)
        ls = jnp.where(strict, -sp, 0.0) if diag else -sp
        hi = ls.astype(BF16)
        lo = (ls - hi.astype(F32)).astype(BF16)
        ct = _dot(hi, tri) + _dot(lo, tri)
        suffix = (ct[:, :blk] - ls) + run
        att = jnp.exp((z - sp) + suffix)
        if diag:
            att = jnp.where(strict, att, 0.0)
        acc = acc + _dot(att.astype(BF16), v_ref[pl.ds(r0, blk), :].astype(BF16))
        return run + ct[:, blk:], acc

    zero = jnp.zeros((blk, blk), F32)
    run, acc = chunk(i, zero, jnp.zeros((blk, SB_HEAD_DIM), F32), True)

    def body(r, carry):
        return chunk(i - 1 - r, carry[0], carry[1], False)

    run, acc = lax.fori_loop(0, i, body, (run, acc))
    o_ref[...] = acc


def sb_prompt(h, bsz, seq):
    qb = seq // Q_BLOCK
    q_off = POOL_DIM // SB_HEAD_DIM
    k_off = q_off + SB_HEADS
    v_off = k_off + SB_HEADS
    jj = np.arange(Q_BLOCK)
    tri = np.concatenate([(jj[:, None] >= jj[None, :]).astype(np.float32),
                          np.ones((Q_BLOCK, Q_BLOCK), np.float32)], axis=1)
    return pl.pallas_call(
        _sb_prompt_body,
        grid=(bsz, SB_HEADS, qb),
        in_specs=[
            pl.BlockSpec((Q_BLOCK, SB_HEAD_DIM), lambda b, hh, i: (b * qb + i, q_off + hh)),
            pl.BlockSpec((seq, SB_HEAD_DIM), lambda b, hh, i: (b, k_off + hh)),
            pl.BlockSpec((seq, SB_HEAD_DIM), lambda b, hh, i: (b, v_off + hh)),
            pl.BlockSpec((Q_BLOCK, 2 * Q_BLOCK), lambda b, hh, i: (0, 0)),
        ],
        out_specs=pl.BlockSpec((Q_BLOCK, SB_HEAD_DIM), lambda b, hh, i: (b * qb + i, hh)),
        out_shape=jax.ShapeDtypeStruct((bsz * seq, SB_DIM), F32),
        compiler_params=pltpu.CompilerParams(
            dimension_semantics=("parallel", "parallel", "arbitrary"), vmem_limit_bytes=VMEM_LIMIT_BYTES),
        name="sb_prompt",
    )(h, h, h, jnp.asarray(tri, BF16))


def _rel_bucket_np(dist):
    n = np.maximum(dist, 0)
    exact = REL_BUCKETS // 2
    nf = np.maximum(n, 1).astype(np.float32)
    log_b = exact + (np.log(nf / np.float32(exact)) / np.float32(math.log(REL_MAX_DIST / exact))
                     * np.float32(REL_BUCKETS - exact)).astype(np.int32)
    return np.where(n < exact, n, np.minimum(log_b, REL_BUCKETS - 1)).astype(np.int32)


def nsa_bias_tables(rel_bias, seq):
    qb = seq // Q_BLOCK
    n_pad = seq // CMP_STRIDE
    s = np.arange(Q_BLOCK)
    near_idx = _rel_bucket_np(np.arange(2)[:, None, None] * Q_BLOCK + s[None, None, :] - s[None, :, None])
    near = rel_bias[jnp.asarray(near_idx)]
    near = near.reshape(2, Q_BLOCK, Q_BLOCK, NSA_GROUPS, NSA_HPG).transpose(3, 0, 1, 4, 2)
    near = near.reshape(NSA_GROUPS, 2, Q_BLOCK, NSA_HPG * Q_BLOCK)
    far = jnp.repeat(rel_bias[REL_BUCKETS - 1].reshape(NSA_GROUPS, 1, NSA_HPG), Q_BLOCK, axis=-1)
    far = far.reshape(NSA_GROUPS, 1, NSA_HPG * Q_BLOCK)
    t_all = np.arange(seq).reshape(qb, 1, Q_BLOCK)
    n_all = np.arange(n_pad).reshape(1, n_pad, 1)
    dist_c = t_all - (n_all * CMP_STRIDE + CMP_BLOCK - 1)
    ok = (dist_c >= 0) & (n_all < n_pad - 1)
    cb = rel_bias[jnp.asarray(_rel_bucket_np(dist_c))]
    cb = jnp.where(jnp.asarray(ok)[..., None], cb, NEG_INF)
    cb = cb.reshape(qb, n_pad, Q_BLOCK, NSA_GROUPS, NSA_HPG).transpose(3, 0, 1, 4, 2)
    cb = cb.reshape(NSA_GROUPS, qb, n_pad, NSA_HPG * Q_BLOCK)
    return near, far, cb


def _nsa_prompt_body(q_ref, tail_ref, ks_ref, vs_ref, kw_ref, vw_ref, kcmp_ref, vcmpt_ref, cb_ref, nb_ref, fb_ref,
                     at_ref, o_ref, vst_ref, vwt_ref, sel_ref, gt_ref, *, seq):
    g = pl.program_id(1)
    i = pl.program_id(2)
    blk = Q_BLOCK
    hw = NSA_HPG * blk
    n_slc = seq // SLC_BLOCK

    @pl.when(i == 0)
    def _():
        for c in range(seq // blk):
            vst_ref[:, c * blk:(c + 1) * blk] = vs_ref[c * blk:(c + 1) * blk, :].T.astype(BF16)
            vwt_ref[:, c * blk:(c + 1) * blk] = vw_ref[c * blk:(c + 1) * blk, :].T.astype(BF16)

    q4 = jnp.concatenate([q_ref[:, h * blk:(h + 1) * blk] for h in range(NSA_HPG)], axis=0).astype(BF16)

    cb = cb_ref[0, 0]
    sc = _dot_nt(kcmp_ref[0, 0].astype(BF16), q4) + cb
    mc = jnp.max(sc, axis=0, keepdims=True)
    pc = jnp.exp(sc - mc)
    pc = pc / jnp.sum(pc, axis=0, keepdims=True)
    pc = pc * jnp.where(cb > 0.5 * NEG_INF, 1.0, 0.0)
    o_c = _dot(vcmpt_ref[0, 0].astype(BF16), pc.astype(BF16))
    psum = pc[:, 0:blk]
    for h in range(1, NSA_HPG):
        psum = psum + pc[:, h * blk:(h + 1) * blk]
    p1 = psum.astype(BF16)
    r1 = psum - p1.astype(F32)
    p2 = r1.astype(BF16)
    p3 = (r1 - p2.astype(F32)).astype(BF16)
    at = at_ref[...]
    imp = (_dot(at, p1) + _dot(at, p2)) + _dot(at, p3)

    jdx = lax.broadcasted_iota(jnp.int32, (n_slc, blk), 0)
    tpos = i * blk + lax.broadcasted_iota(jnp.int32, (n_slc, blk), 1)
    cur = tpos // SLC_BLOCK
    forced = jnp.where(jdx == 0, 1.0, 0.0) + jnp.where(jdx == cur, 1.0, 0.0) + jnp.where(jdx == cur - 1, 1.0, 0.0)
    score = jnp.where(forced > 0.0, FORCE_SCORE, jnp.where(jdx <= cur, imp, -1.0))
    rank = jnp.zeros((n_slc, blk), F32)
    for r in range(n_slc):
        row = jnp.broadcast_to(score[r:r + 1, :], (n_slc, blk))
        gt = jnp.where(row > score, 1.0, 0.0)
        ge = jnp.where(row >= score, 1.0, 0.0)
        rank = rank + jnp.where(jdx > r, ge, gt)
    sel_ref[...] = jnp.where(rank < float(min(SLC_TOPN, n_slc)), 1.0, 0.0)

    ss = lax.broadcasted_iota(jnp.int32, (blk, blk), 0)
    tt = lax.broadcasted_iota(jnp.int32, (blk, blk), 1)
    causal = jnp.where(ss <= tt, 1.0, 0.0)
    anti = jnp.where(ss >= tt, 1.0, 0.0)
    bias_far = fb_ref[0]

    def tile4(mk):
        return jnp.concatenate([mk] * NSA_HPG, axis=1)

    def scores(k_ref, kb, bias):
        r0 = pl.multiple_of(kb * blk, blk)
        return _dot_nt(k_ref[pl.ds(r0, blk), :].astype(BF16), q4) + bias

    def pv(vt_ref, kb, p):
        r0 = pl.multiple_of(kb * blk, blk)
        return _dot(vt_ref[:, pl.ds(r0, blk)], p.astype(BF16))

    def first(s, mk):
        s = jnp.where(tile4(mk) > 0.0, s, NEG_INF)
        m = jnp.max(s, axis=0, keepdims=True)
        p = jnp.exp(s - m)
        return m, jnp.sum(p, axis=0, keepdims=True), p

    def update(carry, s, mk, vt_ref, kb):
        m, l, acc = carry
        s = jnp.where(tile4(mk) > 0.0, s, NEG_INF)
        m_new = jnp.maximum(m, jnp.max(s, axis=0, keepdims=True))
        alpha = jnp.exp(m - m_new)
        p = jnp.exp(s - m_new)
        return m_new, alpha * l + jnp.sum(p, axis=0, keepdims=True), alpha * acc + pv(vt_ref, kb, p)

    def sel_mask(kb):
        r0 = sel_ref[pl.ds(2 * kb, 1), :]
        r1 = sel_ref[pl.ds(2 * kb + 1, 1), :]
        half = blk // 2
        return jnp.concatenate([jnp.broadcast_to(r0, (half, blk)), jnp.broadcast_to(r1, (half, blk))], axis=0)

    m, l, p = first(scores(ks_ref, i, nb_ref[0, 0]), sel_mask(i) * causal)
    carry = (m, l, pv(vst_ref, i, p))
    kb1 = jnp.maximum(i - 1, 0)
    ok1 = jnp.where(i >= 1, 1.0, 0.0)
    carry = update(carry, scores(ks_ref, kb1, nb_ref[0, 1]), sel_mask(kb1) * ok1, vst_ref, kb1)

    def far_body(kb, c):
        return update(c, scores(ks_ref, kb, bias_far), sel_mask(kb), vst_ref, kb)

    m, l, acc = lax.fori_loop(0, jnp.maximum(i - 1, 0), far_body, carry)
    o_s = acc / l

    m, l, p = first(scores(kw_ref, i, nb_ref[0, 0]), causal)
    carry = (m, l, pv(vwt_ref, i, p))
    ones = jnp.ones((blk, blk), F32)
    for d in range(1, WINDOW // blk + 1):
        kb = jnp.maximum(i - d, 0)
        ok = jnp.where(i >= d, 1.0, 0.0)
        bias = nb_ref[0, 1] if d == 1 else bias_far
        mk = (anti if d == WINDOW // blk else ones) * ok
        carry = update(carry, scores(kw_ref, kb, bias), mk, vwt_ref, kb)
    m, l, acc = carry
    o_w = acc / l

    gt_ref[...] = tail_ref[...].T

    def gate(c):
        rows = [gt_ref[pl.ds(SSM_HEADS + 3 * (NSA_HPG * g + h) + c, 1), :] for h in range(NSA_HPG)]
        return jax.nn.sigmoid(jnp.concatenate(rows, axis=1))

    o_t = (gate(0) * o_c + gate(1) * o_s) + gate(2) * o_w
    for h in range(NSA_HPG):
        o_ref[:, h * blk:(h + 1) * blk] = o_t[:, h * blk:(h + 1) * blk].T


def nsa_prompt(main, tail, kcmp, vcmpt, tables, bsz, seq):
    near, far, cb = tables
    qb = seq // Q_BLOCK
    n_pad = seq // CMP_STRIDE
    n_slc = seq // SLC_BLOCK
    hw = NSA_HPG * Q_BLOCK
    ratio = SLC_BLOCK // CMP_STRIDE
    lo = CMP_BLOCK // CMP_STRIDE - 1
    jj = np.arange(n_slc)[:, None]
    nn = np.arange(n_pad)[None, :]
    a_t = ((nn >= ratio * jj - lo) & (nn <= ratio * jj + ratio - 1) & (nn < n_pad - 1)).astype(np.float32)
    dh = NSA_HEAD_DIM
    col = lambda off: off // dh
    return pl.pallas_call(
        functools.partial(_nsa_prompt_body, seq=seq),
        grid=(bsz, NSA_GROUPS, qb),
        in_specs=[
            pl.BlockSpec((Q_BLOCK, hw), lambda b, g, i: (b * qb + i, OFF_Q // hw + g)),
            pl.BlockSpec((Q_BLOCK, EVEN_TAIL), lambda b, g, i: (b * qb + i, 0)),
            pl.BlockSpec((seq, dh), lambda b, g, i: (b, col(OFF_KV + 2 * NSA_KV) + g)),
            pl.BlockSpec((seq, dh), lambda b, g, i: (b, col(OFF_KV + 3 * NSA_KV) + g)),
            pl.BlockSpec((seq, dh), lambda b, g, i: (b, col(OFF_WIN) + g)),
            pl.BlockSpec((seq, dh), lambda b, g, i: (b, col(OFF_WIN + NSA_KV) + g)),
            pl.BlockSpec((1, 1, n_pad, dh), lambda b, g, i: (b, g, 0, 0)),
            pl.BlockSpec((1, 1, dh, n_pad), lambda b, g, i: (b, g, 0, 0)),
            pl.BlockSpec((1, 1, n_pad, hw), lambda b, g, i: (g, i, 0, 0)),
            pl.BlockSpec((1, 2, Q_BLOCK, hw), lambda b, g, i: (g, 0, 0, 0)),
            pl.BlockSpec((1, 1, hw), lambda b, g, i: (g, 0, 0)),
            pl.BlockSpec((n_slc, n_pad), lambda b, g, i: (0, 0)),
        ],
        out_specs=pl.BlockSpec((Q_BLOCK, hw), lambda b, g, i: (b * qb + i, g)),
        out_shape=jax.ShapeDtypeStruct((bsz * seq, NSA_Q), F32),
        scratch_shapes=[pltpu.VMEM((dh, seq), BF16), pltpu.VMEM((dh, seq), BF16),
                        pltpu.VMEM((n_slc, Q_BLOCK), F32), pltpu.VMEM((EVEN_TAIL, Q_BLOCK), F32)],
        compiler_params=pltpu.CompilerParams(
            dimension_semantics=("parallel", "parallel", "arbitrary"), vmem_limit_bytes=VMEM_LIMIT_BYTES),
        name="nsa_prompt",
    )(main, tail, main, main, main, main, kcmp, vcmpt, cb, near, far, jnp.asarray(a_t, BF16))


def split_cols(h, sizes):
    offs = np.cumsum(sizes)[:-1].tolist()
    return jnp.split(h, offs, axis=-1)


def rmsnorm(x, g):
    xf = x.astype(F32)
    y = xf * lax.rsqrt(jnp.mean(xf * xf, axis=-1, keepdims=True) + EPS)
    return (y * g.astype(F32)).astype(x.dtype)


def rel_bucket(dist):
    n = jnp.maximum(dist, 0)
    exact = REL_BUCKETS // 2
    nf = jnp.maximum(n, 1).astype(F32)
    log_b = exact + (jnp.log(nf / exact) / math.log(REL_MAX_DIST / exact) * (REL_BUCKETS - exact)).astype(jnp.int32)
    return jnp.where(n < exact, n, jnp.minimum(log_b, REL_BUCKETS - 1))


def group_bias(rel_table, dist):
    b = rel_table.astype(F32)[rel_bucket(dist)]
    return jnp.moveaxis(b, -1, 0).reshape((NSA_GROUPS, NSA_HPG) + dist.shape)


def ssd_scan(xh, dt, a, bm, cm, h0):
    bsz, L = xh.shape[:2]
    q = SSM_CHUNK if L % SSM_CHUNK == 0 else L
    nc = L // q
    hpg = SSM_HEADS // SSM_GROUPS
    x = xh.astype(F32).reshape(bsz, nc, q, SSM_GROUPS, hpg, SSM_HEAD_DIM)
    dtc = dt.reshape(bsz, nc, q, SSM_GROUPS, hpg)
    bc = bm.astype(F32).reshape(bsz, nc, q, SSM_GROUPS, SSM_STATE)
    cc = cm.astype(F32).reshape(bsz, nc, q, SSM_GROUPS, SSM_STATE)
    acum = jnp.cumsum(dtc * a.reshape(SSM_GROUPS, hpg), axis=2)
    xdt = x * dtc[..., None]
    causal = jnp.tril(jnp.ones((q, q), dtype=bool))[None, None, :, :, None, None]
    seg = acum[:, :, :, None] - acum[:, :, None, :]
    decay = jnp.exp(jnp.where(causal, seg, -jnp.inf))
    cb = jnp.einsum('bctgn,bcsgn->bctsg', cc, bc)
    y_diag = jnp.einsum('bctsgh,bcsghp->bctghp', cb[..., None] * decay, xdt)
    to_end = jnp.exp(acum[:, :, -1:] - acum)
    states = jnp.einsum('bcsgn,bcsghp->bcghpn', bc, xdt * to_end[..., None])
    chunk_decay = jnp.exp(acum[:, :, -1])

    def step(h, inp):
        st, dc = inp
        return h * dc[..., None, None] + st, h

    h_init = h0.astype(F32).reshape(bsz, SSM_GROUPS, hpg, SSM_HEAD_DIM, SSM_STATE)
    h_fin, h_start = lax.scan(step, h_init, (jnp.moveaxis(states, 1, 0), jnp.moveaxis(chunk_decay, 1, 0)))
    h_start = jnp.moveaxis(h_start, 0, 1)
    y_off = jnp.einsum('bctgn,bcghpn->bctghp', cc, h_start) * jnp.exp(acum)[..., None]
    y = (y_diag + y_off).reshape(bsz, L, SSM_HEADS, SSM_HEAD_DIM)
    return y, h_fin.reshape(bsz, SSM_HEADS, SSM_HEAD_DIM, SSM_STATE)


def mamba_mixer(z, xbc, dt_raw, conv_buf, h0, conv_w, conv_b, dt_bias, a_log, d_skip, norm_g):
    bsz, L = xbc.shape[:2]
    ext = jnp.concatenate([conv_buf.astype(xbc.dtype), xbc], axis=1)
    acc = conv_b
    for k in range(SSM_CONV):
        acc = acc + ext[:, k:k + L] * conv_w[k]
    u = jax.nn.silu(acc)
    xs, bm, cm = split_cols(u, (SSM_D_INNER, SSM_GROUPS * SSM_STATE, SSM_GROUPS * SSM_STATE))
    xh = xs.reshape(bsz, L, SSM_HEADS, SSM_HEAD_DIM)
    dt = jax.nn.softplus(dt_raw.astype(F32) + dt_bias.astype(F32))
    a = -jnp.exp(a_log.astype(F32))
    y, h_new = ssd_scan(xh, dt, a, bm.reshape(bsz, L, SSM_GROUPS, SSM_STATE),
                        cm.reshape(bsz, L, SSM_GROUPS, SSM_STATE), h0)
    y = y + d_skip.astype(F32)[:, None] * xh.astype(F32)
    y = y.reshape(bsz, L, SSM_D_INNER) * jax.nn.silu(z.astype(F32))
    y = rmsnorm(y.reshape(bsz, L, SSM_GROUPS, -1), norm_g.reshape(SSM_GROUPS, -1)).reshape(bsz, L, SSM_D_INNER)
    return y, ext[:, L:], h_new


def compress(k, alpha, w):
    bsz, L = k.shape[:2]
    n_sub = L // CMP_STRIDE
    sub = k[:, :n_sub * CMP_STRIDE].astype(F32).reshape(bsz, n_sub, CMP_STRIDE, NSA_GROUPS, NSA_HEAD_DIM)
    alpha = alpha.astype(F32)
    first = jnp.einsum('bnrgd,rd->bngd', sub, alpha[:CMP_STRIDE])
    second = jnp.einsum('bnrgd,rd->bngd', sub, alpha[CMP_STRIDE:])
    pooled = first[:, :-1] + second[:, 1:]
    return jnp.einsum('bngd,de->bnge', pooled, w.astype(F32))


def nsa_attend(q, q_pos, kcmp, vcmp, cmp_end, kslc, vslc, kwin, vwin, win_pos, gates, rel_table):
    bsz, tq = q.shape[:2]
    qg = q.astype(F32).reshape(bsz, tq, NSA_GROUPS, NSA_HPG, NSA_HEAD_DIM)
    dist_c = q_pos[:, None] - cmp_end[None, :]
    mask_c = dist_c >= 0
    s_c = jnp.einsum('btghd,bngd->bghtn', qg, kcmp.astype(F32)) + group_bias(rel_table, dist_c)
    p_c = jax.nn.softmax(jnp.where(mask_c, s_c, NEG_INF), axis=-1) * mask_c
    o_c = jnp.einsum('bghtn,bngd->btghd', p_c, vcmp.astype(F32))
    n_c = kcmp.shape[1]
    n_s = kslc.shape[2]
    ratio = SLC_BLOCK // CMP_STRIDE
    lo = CMP_BLOCK // CMP_STRIDE - 1
    n_off = ratio + lo
    imp = jnp.pad(p_c.sum(axis=2), ((0, 0), (0, 0), (0, 0), (lo, ratio * n_s + n_off - lo - n_c)))
    imp_s = imp[..., 0:ratio * n_s:ratio]
    for o in range(1, n_off):
        imp_s = imp_s + imp[..., o:o + ratio * n_s:ratio]
    cur = q_pos // SLC_BLOCK
    blk = jnp.arange(n_s)
    valid = blk[None, :] <= cur[:, None]
    forced = (blk[None, :] == 0) | (blk[None, :] == cur[:, None]) | (blk[None, :] == cur[:, None] - 1)
    score = jnp.where(forced, FORCE_SCORE, jnp.where(valid, imp_s, -1.0))
    _, sel = lax.top_k(score, min(SLC_TOPN, n_s))
    n_sel = sel.shape[-1]
    bidx = jnp.arange(bsz)[:, None, None, None]
    gidx = jnp.arange(NSA_GROUPS)[None, :, None, None]
    gk = kslc[bidx, gidx, sel].reshape(bsz, NSA_GROUPS, tq, n_sel * SLC_BLOCK, NSA_HEAD_DIM)
    gv = vslc[bidx, gidx, sel].reshape(bsz, NSA_GROUPS, tq, n_sel * SLC_BLOCK, NSA_HEAD_DIM)
    kpos = (sel[..., None] * SLC_BLOCK + jnp.arange(SLC_BLOCK)).reshape(bsz, NSA_GROUPS, tq, -1)
    dist_s = q_pos[None, None, :, None] - kpos
    mask_s = (dist_s >= 0)[:, :, None]
    tb = rel_table.astype(F32).reshape(REL_BUCKETS, NSA_GROUPS, NSA_HPG)
    bias_s = jnp.moveaxis(tb[rel_bucket(dist_s), gidx], -1, 2)
    s_s = jnp.einsum('btghd,bgtkd->bghtk', qg, gk.astype(F32)) + bias_s
    p_s = jax.nn.softmax(jnp.where(mask_s, s_s, NEG_INF), axis=-1)
    o_s = jnp.einsum('bghtk,bgtkd->btghd', p_s, gv.astype(F32))
    dist_w = q_pos[:, None] - win_pos[None, :]
    mask_w = (dist_w >= 0) & (dist_w <= WINDOW) & (win_pos[None, :] >= 0)
    s_w = jnp.einsum('btghd,bkgd->bghtk', qg, kwin.astype(F32)) + group_bias(rel_table, dist_w)
    p_w = jax.nn.softmax(jnp.where(mask_w, s_w, NEG_INF), axis=-1)
    o_w = jnp.einsum('bghtk,bkgd->btghd', p_w, vwin.astype(F32))
    g = jax.nn.sigmoid(gates.astype(F32)).reshape(bsz, tq, NSA_GROUPS, NSA_HPG, 3)
    o = g[..., 0:1] * o_c + g[..., 1:2] * o_s + g[..., 2:3] * o_w
    return o.reshape(bsz, tq, NSA_Q)


def nsa_mix(q, kc, vc, ks, vs, kw, vw, gates, past, win_buf, wb, cmp_alpha, cmp_w, g_kcmp, rel_table):
    bsz, L = q.shape[:2]
    win_new = jnp.stack([kw, vw], axis=2)
    if past is None:
        pos0 = 0
        kc_all, vc_all, ks_all, vs_all = kc, vc, ks, vs
        win_all = jnp.concatenate([jnp.zeros((bsz, WINDOW) + win_new.shape[2:], win_new.dtype), win_new], axis=1)
    else:
        pos0 = past.shape[1]
        kc_all = jnp.concatenate([past[:, :, 0].astype(kc.dtype), kc], axis=1)
        vc_all = jnp.concatenate([past[:, :, 1].astype(vc.dtype), vc], axis=1)
        ks_all = jnp.concatenate([past[:, :, 2].astype(ks.dtype), ks], axis=1)
        vs_all = jnp.concatenate([past[:, :, 3].astype(vs.dtype), vs], axis=1)
        win_all = jnp.concatenate([win_buf.astype(win_new.dtype), win_new], axis=1)
    new_win = win_all[:, -wb:]
    total = pos0 + L
    kcmp = rmsnorm(compress(kc_all, cmp_alpha[0], cmp_w[0]), g_kcmp)
    vcmp = compress(vc_all, cmp_alpha[1], cmp_w[1])
    cmp_end = jnp.arange(kcmp.shape[1]) * CMP_STRIDE + (CMP_BLOCK - 1)
    n_s = -(-total // SLC_BLOCK)
    pad = n_s * SLC_BLOCK - total

    def to_blocks(t):
        t = jnp.pad(t, ((0, 0), (0, pad), (0, 0), (0, 0)))
        return t.reshape(bsz, n_s, SLC_BLOCK, NSA_GROUPS, NSA_HEAD_DIM).transpose(0, 3, 1, 2, 4)

    kslc, vslc = to_blocks(ks_all), to_blocks(vs_all)
    if past is None:
        def one_block(i):
            s = i * Q_BLOCK
            wkv = lax.dynamic_slice_in_dim(win_all, s, Q_BLOCK + WINDOW, axis=1)
            return nsa_attend(lax.dynamic_slice_in_dim(q, s, Q_BLOCK, axis=1), s + jnp.arange(Q_BLOCK),
                              kcmp, vcmp, cmp_end, kslc, vslc, wkv[:, :, 0], wkv[:, :, 1],
                              s - WINDOW + jnp.arange(Q_BLOCK + WINDOW),
                              lax.dynamic_slice_in_dim(gates, s, Q_BLOCK, axis=1), rel_table)
        o = lax.map(one_block, jnp.arange(L // Q_BLOCK))
        o = jnp.moveaxis(o, 0, 1).reshape(bsz, L, NSA_Q)
    else:
        win_pos = pos0 - wb + jnp.arange(win_all.shape[1])
        o = nsa_attend(q, pos0 + jnp.arange(L), kcmp, vcmp, cmp_end, kslc, vslc,
                       win_all[:, :, 0], win_all[:, :, 1], win_pos, gates, rel_table)
    return o, new_win


def pool_mix(u, buf, pos0, pool_w, pool_scale):
    bsz, L = u.shape[:2]
    ext = jnp.concatenate([buf.astype(u.dtype), u], axis=1)
    cs = jnp.cumsum(ext.astype(F32), axis=1)
    cs = jnp.concatenate([jnp.zeros((bsz, 1, POOL_DIM), F32), cs], axis=1)
    pos = pos0 + jnp.arange(L)
    hi = cs[:, POOL_BUF + 1:POOL_BUF + 1 + L]
    outs = []
    for g, w in enumerate(POOL_WINDOWS):
        c0, c1 = g * POOL_GROUP_DIM, (g + 1) * POOL_GROUP_DIM
        lo = cs[:, POOL_BUF + 1 - w:POOL_BUF + 1 - w + L, c0:c1]
        cnt = jnp.minimum(pos + 1, w).astype(F32)[None, :, None]
        diff = (hi[..., c0:c1] - lo) / cnt - u[..., c0:c1].astype(F32)
        outs.append(jnp.einsum('blc,cd->bld', diff, pool_w[g].astype(F32)))
    y = jnp.concatenate(outs, axis=-1) * pool_scale.astype(F32)
    return y, ext[:, -POOL_BUF:]


def sb_attend(q, q_pos, k_segs, v_segs, k_pos):
    bsz, tq = q.shape[:2]
    qf = q.astype(F32) * (SB_HEAD_DIM ** -0.5)
    z = jnp.concatenate([jnp.einsum('bthd,bshd->bhts', qf, k.astype(F32)) for k in k_segs], axis=-1)
    mask = k_pos[None, :] < q_pos[:, None]
    log_stay = jnp.where(mask, jax.nn.log_sigmoid(-z), 0.0)
    suffix = lax.cumsum(log_stay, axis=3, reverse=True) - log_stay
    att = jnp.where(mask, jnp.exp(jax.nn.log_sigmoid(z) + suffix), 0.0)
    out = jnp.zeros((bsz, tq, SB_HEADS, SB_HEAD_DIM), F32)
    off = 0
    for v in v_segs:
        n = v.shape[1]
        out = out + jnp.einsum('bhts,bshd->bthd', att[..., off:off + n], v.astype(F32))
        off += n
    return out.reshape(bsz, tq, SB_DIM)


def even_mixer(x, past, win_buf, wb, conv_buf, h0, pw):
    bsz, L, _ = x.shape
    m = bsz * L
    main, tail = in_proj(x.reshape(m, D_MODEL), pw['g_mix'], pw['w_main'], pw['flags'], pw['post_gain'],
                         pw['post_scale'], pw['w_tail'])
    main2, tail2 = main, tail
    main = main.reshape(bsz, L, EVEN_MAIN)
    z = main[..., OFF_Z:OFF_XBC]
    xbc = main[..., OFF_XBC:OFF_Q]
    q = main[..., OFF_Q:OFF_KV].reshape(bsz, L, NSA_HEADS, NSA_HEAD_DIM)
    kv = main[..., OFF_KV:OFF_WIN].reshape(bsz, L, 4, NSA_GROUPS, NSA_HEAD_DIM)
    win = main[..., OFF_WIN:].reshape(bsz, L, 2, NSA_GROUPS, NSA_HEAD_DIM)
    tail = tail.reshape(bsz, L, EVEN_TAIL)
    dt_raw = tail[..., :SSM_HEADS]
    gates = tail[..., SSM_HEADS:SSM_HEADS + 3 * NSA_HEADS]
    y_a, new_conv, new_h = mamba_mixer(z, xbc, dt_raw, conv_buf, h0, pw['conv_w'], pw['conv_b'], pw['dt_bias'],
                                       pw['a_log'], pw['d_skip'], pw['ssm_g'])
    if past is None:
        kcmp = rmsnorm(compress(kv[:, :, 0], pw['cmp_alpha'][0], pw['cmp_w'][0]), pw['g_kcmp'])
        vcmp = compress(kv[:, :, 1], pw['cmp_alpha'][1], pw['cmp_w'][1])
        kcmp = jnp.pad(kcmp, ((0, 0), (0, 1), (0, 0), (0, 0))).transpose(0, 2, 1, 3)
        vcmpt = jnp.pad(vcmp, ((0, 0), (0, 1), (0, 0), (0, 0))).transpose(0, 2, 3, 1)
        o_b = nsa_prompt(main2, tail2, kcmp, vcmpt, pw['nsa_tables'], bsz, L)
        new_win = win[:, L - wb:]
    else:
        o_b, new_win = nsa_mix(q, kv[:, :, 0], kv[:, :, 1], kv[:, :, 2], kv[:, :, 3], win[:, :, 0], win[:, :, 1],
                               gates.reshape(bsz, L, NSA_HEADS, 3), past, win_buf, wb,
                               pw['cmp_alpha'], pw['cmp_w'], pw['g_kcmp'], pw['rel_table'])
    y = out_proj(x.reshape(m, D_MODEL), y_a.reshape(m, SSM_D_INNER), o_b.reshape(m, NSA_Q),
                 pw['w_out_a'], pw['w_out_b'])
    return y.reshape(bsz, L, D_MODEL), kv, new_win, new_conv, new_h


def odd_mixer(x, past, pool_buf, pw):
    bsz, L, _ = x.shape
    m = bsz * L
    h = in_proj(x.reshape(m, D_MODEL), pw['g_mix'], pw['w_in'], pw['flags'], pw['post_gain'], pw['post_scale'])
    h2 = h
    h = h.reshape(bsz, L, -1)
    u, q, k, v = split_cols(h, (POOL_DIM, SB_DIM, SB_DIM, SB_DIM))
    pos0 = 0 if past is None else past.shape[1]
    y_c, new_pool = pool_mix(u, pool_buf, pos0, pw['pool_w'], pw['pool_scale'])
    q = q.reshape(bsz, L, SB_HEADS, SB_HEAD_DIM)
    k = k.reshape(bsz, L, SB_HEADS, SB_HEAD_DIM)
    v = v.reshape(bsz, L, SB_HEADS, SB_HEAD_DIM)
    if past is None:
        o = sb_prompt(h2, bsz, L)
    else:
        o = sb_attend(q, pos0 + jnp.arange(L), (past[:, :, 0], k), (past[:, :, 1], v), jnp.arange(pos0 + L))
    y = out_proj(x.reshape(m, D_MODEL), y_c.reshape(m, POOL_DIM), o.reshape(m, SB_DIM), pw['w_out_a'], pw['w_out_b'])
    return y.reshape(bsz, L, D_MODEL), jnp.stack([k, v], axis=2), new_pool


def _even_weights(l, e, mix_norm, w_in_even, w_out_even, ssm_conv_w, ssm_conv_b, ssm_dt_bias, ssm_a_log, ssm_d,
                  ssm_norm, nsa_cmp_alpha, nsa_cmp_w, nsa_qk_gain, rel_bias, nsa_tables, tn):
    w = w_in_even[e]
    sizes = (SSM_D_INNER, SSM_CONV_DIM, SSM_HEADS, NSA_Q) + (NSA_KV,) * 6 + (3 * NSA_HEADS,)
    wz, wxbc, wdt, wq, wkc, wvc, wks, wvs, wkw, wvw, wg = split_cols(w, sizes)
    w_main = jnp.concatenate([wz, wxbc, wq, wkc, wvc, wks, wvs, wkw, wvw], axis=1).astype(BF16)
    pad = EVEN_TAIL - SSM_HEADS - 3 * NSA_HEADS
    w_tail = jnp.concatenate([wdt, wg, jnp.zeros((D_MODEL, pad), F32)], axis=1).astype(BF16)
    gain = nsa_qk_gain[e]
    ones = jnp.ones((NSA_KV,), F32)
    post_gain = jnp.concatenate([
        jnp.ones((OFF_Q,), F32), jnp.tile(gain[0], NSA_HEADS), ones, ones, jnp.tile(gain[2], NSA_GROUPS), ones,
        jnp.tile(gain[3], NSA_GROUPS), ones]).reshape(1, EVEN_MAIN)
    post_scale = jnp.concatenate([
        jnp.ones((OFF_Q,), F32), jnp.full((NSA_Q,), NSA_HEAD_DIM ** -0.5, F32),
        jnp.ones((6 * NSA_KV,), F32)]).reshape(1, EVEN_MAIN)
    flags = np.zeros((EVEN_MAIN // tn,), np.int32)
    for lo_, hi_ in ((OFF_Q, OFF_KV), (OFF_KV + 2 * NSA_KV, OFF_KV + 3 * NSA_KV), (OFF_WIN, OFF_WIN + NSA_KV)):
        assert lo_ % tn == 0 and hi_ % tn == 0
        flags[lo_ // tn:hi_ // tn] = 1
    wo = w_out_even[e].astype(BF16)
    return dict(g_mix=mix_norm[l], w_main=w_main, w_tail=w_tail, flags=jnp.asarray(flags), post_gain=post_gain,
                post_scale=post_scale, conv_w=ssm_conv_w[e], conv_b=ssm_conv_b[e], dt_bias=ssm_dt_bias[e],
                a_log=ssm_a_log[e], d_skip=ssm_d[e], ssm_g=ssm_norm[e], cmp_alpha=nsa_cmp_alpha[e],
                cmp_w=nsa_cmp_w[e], g_kcmp=gain[1], rel_table=rel_bias, nsa_tables=nsa_tables,
                w_out_a=wo[:SSM_D_INNER], w_out_b=wo[SSM_D_INNER:])


def _odd_weights(l, o, mix_norm, w_in_odd, w_out_odd, pool_w, pool_scale, tn):
    n = w_in_odd.shape[2]
    wo = w_out_odd[o].astype(BF16)
    return dict(g_mix=mix_norm[l], w_in=w_in_odd[o].astype(BF16), flags=jnp.zeros((n // tn,), jnp.int32),
                post_gain=jnp.ones((1, n), F32), post_scale=jnp.ones((1, n), F32),
                pool_w=pool_w[o], pool_scale=pool_scale[o], w_out_a=wo[:POOL_DIM], w_out_b=wo[POOL_DIM:])


def kernel(x_prompt, x_sample, cache_nsa_kv, cache_sb_kv, state_nsa_win, state_ssm, state_conv, state_pool, page_table, ffn_norm, ffn_w_gate, ffn_w_up, ffn_w_down, mix_norm, w_in_even, w_out_even, ssm_conv_w, ssm_conv_b, ssm_dt_bias, ssm_a_log, ssm_d, ssm_norm, nsa_cmp_alpha, nsa_cmp_w, nsa_qk_gain, rel_bias, w_in_odd, w_out_odd, pool_w, pool_scale):
    bp, lp, _ = x_prompt.shape
    bs, ls, _ = x_sample.shape
    past_len = page_table.shape[1] * PAGE_SIZE
    wb = state_nsa_win.shape[2]
    tn = 512
    xp = x_prompt.reshape(bp * lp, D_MODEL)
    xs = x_sample.reshape(bs * ls, D_MODEL)
    wg_all = ffn_w_gate.astype(BF16)
    wu_all = ffn_w_up.astype(BF16)
    wd_all = ffn_w_down.astype(BF16)
    nsa_tables = nsa_bias_tables(rel_bias, lp)
    kv_p, kv_s, sb_p, sb_s, win_p, win_s = [], [], [], [], [], []
    ssm_p, ssm_s, conv_p, conv_s, pool_p, pool_s = [], [], [], [], [], []
    for l in range(DEPTH):
        fa = (ffn_norm[l, 0], wg_all[l, 0], wu_all[l, 0], wd_all[l, 0])
        xp = ffn_half(xp, *fa)
        xs = ffn_half(xs, *fa)
        xp3 = xp.reshape(bp, lp, D_MODEL)
        xs3 = xs.reshape(bs, ls, D_MODEL)
        if l % 2 == 0:
            e = l // 2
            pw = _even_weights(l, e, mix_norm, w_in_even, w_out_even, ssm_conv_w, ssm_conv_b, ssm_dt_bias,
                               ssm_a_log, ssm_d, ssm_norm, nsa_cmp_alpha, nsa_cmp_w, nsa_qk_gain, rel_bias, nsa_tables, tn)
            xp3, a_kv, a_win, a_conv, a_h = even_mixer(
                xp3, None, None, wb, jnp.zeros((bp, SSM_CONV - 1, SSM_CONV_DIM), F32),
                jnp.zeros((bp, SSM_HEADS, SSM_HEAD_DIM, SSM_STATE), F32), pw)
            past = cache_nsa_kv[e, page_table].reshape(bs, past_len, NSA_KV_PARTS, NSA_GROUPS, NSA_HEAD_DIM)
            xs3, b_kv, b_win, b_conv, b_h = even_mixer(xs3, past, state_nsa_win[e], wb, state_conv[e], state_ssm[e], pw)
            kv_p.append(a_kv); kv_s.append(b_kv)
            win_p.append(a_win); win_s.append(b_win)
            conv_p.append(a_conv); conv_s.append(b_conv)
            ssm_p.append(a_h); ssm_s.append(b_h)
        else:
            o = l // 2
            pw = _odd_weights(l, o, mix_norm, w_in_odd, w_out_odd, pool_w, pool_scale, tn)
            xp3, a_kv, a_pool = odd_mixer(xp3, None, jnp.zeros((bp, POOL_BUF, POOL_DIM), F32), pw)
            past = cache_sb_kv[o, page_table].reshape(bs, past_len, 2, SB_HEADS, SB_HEAD_DIM)
            xs3, b_kv, b_pool = odd_mixer(xs3, past, state_pool[o], pw)
            sb_p.append(a_kv); sb_s.append(b_kv)
            pool_p.append(a_pool); pool_s.append(b_pool)
        xp = xp3.reshape(bp * lp, D_MODEL)
        xs = xs3.reshape(bs * ls, D_MODEL)
        fb = (ffn_norm[l, 1], wg_all[l, 1], wu_all[l, 1], wd_all[l, 1])
        xp = ffn_half(xp, *fb)
        xs = ffn_half(xs, *fb)
    return (xp.reshape(bp, lp, D_MODEL), xs.reshape(bs, ls, D_MODEL), jnp.stack(kv_p), jnp.stack(kv_s),
            jnp.stack(sb_p), jnp.stack(sb_s), jnp.stack(win_p), jnp.stack(win_s), jnp.stack(ssm_p), jnp.stack(ssm_s),
            jnp.stack(conv_p), jnp.stack(conv_s), jnp.stack(pool_p), jnp.stack(pool_s))
```

```python
import functools
import math

import jax
import jax.numpy as jnp
import numpy as np
from jax import lax
from jax.experimental import pallas as pl
from jax.experimental.pallas import tpu as pltpu

F32 = jnp.float32
BF16 = jnp.bfloat16

D_MODEL = 2048
DEPTH = 4
PAGE_SIZE = 128
EPS = 1e-6
NEG_INF = -1e30
Q_BLOCK = 128
FFN_RESIDUAL = 0.5
D_FF = 5632

SSM_HEADS = 32
SSM_HEAD_DIM = 64
SSM_D_INNER = SSM_HEADS * SSM_HEAD_DIM
SSM_GROUPS = 4
SSM_STATE = 128
SSM_CONV = 4
SSM_CONV_DIM = SSM_D_INNER + 2 * SSM_GROUPS * SSM_STATE
SSM_CHUNK = 128

NSA_HEADS = 16
NSA_GROUPS = 4
NSA_HPG = NSA_HEADS // NSA_GROUPS
NSA_HEAD_DIM = 128
NSA_Q = NSA_HEADS * NSA_HEAD_DIM
NSA_KV = NSA_GROUPS * NSA_HEAD_DIM
NSA_KV_PARTS = 4
CMP_BLOCK = 32
CMP_STRIDE = 16
SLC_BLOCK = 64
SLC_TOPN = 16
WINDOW = 512
FORCE_SCORE = 1e4
REL_BUCKETS = 32
REL_MAX_DIST = 128
CMP_BAND = 2 * (Q_BLOCK // CMP_STRIDE)

POOL_WINDOWS = (2, 4, 8, 16)
POOL_GROUPS = len(POOL_WINDOWS)
POOL_GROUP_DIM = 256
POOL_DIM = POOL_GROUPS * POOL_GROUP_DIM
POOL_BUF = max(POOL_WINDOWS) - 1

SB_HEADS = 16
SB_HEAD_DIM = 128
SB_DIM = SB_HEADS * SB_HEAD_DIM

LANES = 128
VMEM_LIMIT_BYTES = 56 * 1024 * 1024

EVEN_MAIN = SSM_D_INNER + SSM_CONV_DIM + NSA_Q + 6 * NSA_KV
EVEN_TAIL = LANES
OFF_Z = 0
OFF_XBC = SSM_D_INNER
OFF_Q = OFF_XBC + SSM_CONV_DIM
OFF_KV = OFF_Q + NSA_Q
OFF_WIN = OFF_KV + 4 * NSA_KV


def _row_tile(m, want):
    t = min(m, want)
    assert m % t == 0
    return t


def _ffn_body(x_ref, g_ref, wg_ref, wu_ref, wd_ref, o_ref, h_ref, acc_ref):
    j = pl.program_id(1)

    @pl.when(j == 0)
    def _():
        x = x_ref[...]
        y = x * lax.rsqrt(jnp.mean(x * x, axis=-1, keepdims=True) + EPS)
        h_ref[...] = (y * g_ref[...]).astype(BF16)
        acc_ref[...] = jnp.zeros_like(acc_ref)

    h = h_ref[...]
    a = jnp.dot(h, wg_ref[...], preferred_element_type=F32)
    b = jnp.dot(h, wu_ref[...], preferred_element_type=F32)
    t = (a * jax.nn.sigmoid(a)) * b
    acc_ref[...] += jnp.dot(t.astype(BF16), wd_ref[...], preferred_element_type=F32)

    @pl.when(j == pl.num_programs(1) - 1)
    def _():
        o_ref[...] = x_ref[...] + FFN_RESIDUAL * acc_ref[...]


def ffn_half(x, g, wg, wu, wd, *, tm=512, tf=512):
    m, d = x.shape
    f = wg.shape[1]
    tm = _row_tile(m, tm)
    assert f % tf == 0
    return pl.pallas_call(
        _ffn_body,
        grid=(m // tm, f // tf),
        in_specs=[
            pl.BlockSpec((tm, d), lambda i, j: (i, 0)),
            pl.BlockSpec((1, d), lambda i, j: (0, 0)),
            pl.BlockSpec((d, tf), lambda i, j: (0, j)),
            pl.BlockSpec((d, tf), lambda i, j: (0, j)),
            pl.BlockSpec((tf, d), lambda i, j: (j, 0)),
        ],
        out_specs=pl.BlockSpec((tm, d), lambda i, j: (i, 0)),
        out_shape=jax.ShapeDtypeStruct((m, d), F32),
        scratch_shapes=[pltpu.VMEM((tm, d), BF16), pltpu.VMEM((tm, d), F32)],
        compiler_params=pltpu.CompilerParams(
            dimension_semantics=("parallel", "arbitrary"), vmem_limit_bytes=VMEM_LIMIT_BYTES),
        name="ffn_half",
    )(x, g.reshape(1, d), wg, wu, wd)


def _proj_body(flag_ref, x_ref, g_ref, w_ref, pg_ref, ps_ref, *rest, has_tail):
    if has_tail:
        wt_ref, o_ref, ot_ref, h_ref = rest
    else:
        o_ref, h_ref = rest
    j = pl.program_id(1)

    @pl.when(j == 0)
    def _():
        x = x_ref[...]
        y = x * lax.rsqrt(jnp.mean(x * x, axis=-1, keepdims=True) + EPS)
        h = (y * g_ref[...]).astype(BF16)
        h_ref[...] = h
        if has_tail:
            ot_ref[...] = jnp.dot(h, wt_ref[...], preferred_element_type=F32)

    r = jnp.dot(h_ref[...], w_ref[...], preferred_element_type=F32)

    @pl.when(flag_ref[j] == 0)
    def _():
        o_ref[...] = r

    @pl.when(flag_ref[j] != 0)
    def _():
        tm, tn = r.shape
        for c in range(tn // LANES):
            rc = r[:, c * LANES:(c + 1) * LANES]
            yc = rc * lax.rsqrt(jnp.mean(rc * rc, axis=-1, keepdims=True) + EPS)
            yc = (yc * pg_ref[:, c * LANES:(c + 1) * LANES]) * ps_ref[:, c * LANES:(c + 1) * LANES]
            o_ref[:, c * LANES:(c + 1) * LANES] = yc


def in_proj(x, g, w, flags, post_gain, post_scale, w_tail=None, *, tm=512, tn=512):
    m, d = x.shape
    n = w.shape[1]
    tm = _row_tile(m, tm)
    assert n % tn == 0
    has_tail = w_tail is not None
    in_specs = [
        pl.BlockSpec((tm, d), lambda i, j, fl: (i, 0)),
        pl.BlockSpec((1, d), lambda i, j, fl: (0, 0)),
        pl.BlockSpec((d, tn), lambda i, j, fl: (0, j)),
        pl.BlockSpec((1, tn), lambda i, j, fl: (0, j)),
        pl.BlockSpec((1, tn), lambda i, j, fl: (0, j)),
    ]
    out_specs = [pl.BlockSpec((tm, tn), lambda i, j, fl: (i, j))]
    out_shape = [jax.ShapeDtypeStruct((m, n), F32)]
    args = [x, g.reshape(1, d), w, post_gain, post_scale]
    if has_tail:
        nt = w_tail.shape[1]
        in_specs.append(pl.BlockSpec((d, nt), lambda i, j, fl: (0, 0)))
        out_specs.append(pl.BlockSpec((tm, nt), lambda i, j, fl: (i, 0)))
        out_shape.append(jax.ShapeDtypeStruct((m, nt), F32))
        args.append(w_tail)
    res = pl.pallas_call(
        functools.partial(_proj_body, has_tail=has_tail),
        grid_spec=pltpu.PrefetchScalarGridSpec(
            num_scalar_prefetch=1,
            grid=(m // tm, n // tn),
            in_specs=in_specs,
            out_specs=out_specs,
            scratch_shapes=[pltpu.VMEM((tm, d), BF16)],
        ),
        out_shape=out_shape,
        compiler_params=pltpu.CompilerParams(
            dimension_semantics=("parallel", "arbitrary"), vmem_limit_bytes=VMEM_LIMIT_BYTES),
        name="in_proj",
    )(flags, *args)
    return res if has_tail else res[0]


def _out_proj_body(x_ref, a1_ref, a2_ref, w1_ref, w2_ref, o_ref):
    r = jnp.dot(a1_ref[...].astype(BF16), w1_ref[...], preferred_element_type=F32)
    r = r + jnp.dot(a2_ref[...].astype(BF16), w2_ref[...], preferred_element_type=F32)
    o_ref[...] = x_ref[...] + r


def out_proj(x, a1, a2, w1, w2, *, tm=512, tn=512):
    m, d = x.shape
    k1, k2 = a1.shape[1], a2.shape[1]
    tm = _row_tile(m, tm)
    return pl.pallas_call(
        _out_proj_body,
        grid=(m // tm, d // tn),
        in_specs=[
            pl.BlockSpec((tm, tn), lambda i, j: (i, j)),
            pl.BlockSpec((tm, k1), lambda i, j: (i, 0)),
            pl.BlockSpec((tm, k2), lambda i, j: (i, 0)),
            pl.BlockSpec((k1, tn), lambda i, j: (0, j)),
            pl.BlockSpec((k2, tn), lambda i, j: (0, j)),
        ],
        out_specs=pl.BlockSpec((tm, tn), lambda i, j: (i, j)),
        out_shape=jax.ShapeDtypeStruct((m, d), F32),
        compiler_params=pltpu.CompilerParams(
            dimension_semantics=("parallel", "arbitrary"), vmem_limit_bytes=VMEM_LIMIT_BYTES),
        name="out_proj",
    )(x, a1, a2, w1, w2)


def _dot_nt(a, b):
    return lax.dot_general(a, b, (((1,), (1,)), ((), ())), preferred_element_type=F32)


def _dot(a, b):
    return jnp.dot(a, b, preferred_element_type=F32)


def _softplus(z):
    return jnp.maximum(z, 0.0) + jnp.log(1.0 + jnp.exp(-jnp.abs(z)))


SB_KEY_CHUNK = 4 * Q_BLOCK
SB_HEADS_PER_STEP = 2


def _sb_prompt_body(q_ref, k_ref, v_ref, tri_ref, o_ref):
    i = pl.program_id(2)
    blk = Q_BLOCK
    ck = SB_KEY_CHUNK
    nsub = ck // blk
    dh = SB_HEAD_DIM
    heads = range(SB_HEADS_PER_STEP)
    qs = [(q_ref[:, h * dh:(h + 1) * dh] * (dh ** -0.5)).astype(BF16) for h in heads]
    tri = tri_ref[...]

    def chunk_head(h, c, run, acc, diag):
        r0 = pl.multiple_of(c * ck, ck)
        z = _dot_nt(qs[h], k_ref[pl.ds(r0, ck), h * dh:(h + 1) * dh].astype(BF16))
        sp = _softplus(z)
        if diag:
            t_pos = i * blk + lax.broadcasted_iota(jnp.int32, (blk, ck), 0)
            s_pos = c * ck + lax.broadcasted_iota(jnp.int32, (blk, ck), 1)
            strict = s_pos < t_pos
            go = jnp.where(strict, sp, 0.0)
        else:
            go = sp
        hi = go.astype(BF16)
        lo = (go - hi.astype(F32)).astype(BF16)
        suffix = [None] * nsub
        for j in reversed(range(nsub)):
            sl = slice(j * blk, (j + 1) * blk)
            ct = _dot(hi[:, sl], tri) + _dot(lo[:, sl], tri)
            suffix[j] = (ct[:, :blk] - go[:, sl]) + run
            run = run + ct[:, blk:]
        att = jnp.exp((z - sp) - jnp.concatenate(suffix, axis=1))
        if diag:
            att = jnp.where(strict, att, 0.0)
        acc = acc + _dot(att.astype(BF16), v_ref[pl.ds(r0, ck), h * dh:(h + 1) * dh].astype(BF16))
        return run, acc

    def chunk(c, carry, diag):
        return tuple(chunk_head(h, c, carry[h][0], carry[h][1], diag) for h in heads)

    cd = i // nsub
    init = tuple((jnp.zeros((blk, blk), F32), jnp.zeros((blk, dh), F32)) for _ in heads)
    carry = chunk(cd, init, True)
    carry = lax.fori_loop(0, cd, lambda r, cr: chunk(cd - 1 - r, cr, False), carry)
    for h in heads:
        o_ref[:, h * dh:(h + 1) * dh] = carry[h][1]


def sb_prompt(h, bsz, seq):
    qb = seq // Q_BLOCK
    hw = SB_HEADS_PER_STEP * SB_HEAD_DIM
    q_off = POOL_DIM // hw
    k_off = q_off + SB_DIM // hw
    v_off = k_off + SB_DIM // hw
    jj = np.arange(Q_BLOCK)
    tri = np.concatenate([(jj[:, None] >= jj[None, :]).astype(np.float32),
                          np.ones((Q_BLOCK, Q_BLOCK), np.float32)], axis=1)
    return pl.pallas_call(
        _sb_prompt_body,
        grid=(bsz, SB_DIM // hw, qb),
        in_specs=[
            pl.BlockSpec((Q_BLOCK, hw), lambda b, hh, i: (b * qb + i, q_off + hh)),
            pl.BlockSpec((seq, hw), lambda b, hh, i: (b, k_off + hh)),
            pl.BlockSpec((seq, hw), lambda b, hh, i: (b, v_off + hh)),
            pl.BlockSpec((Q_BLOCK, 2 * Q_BLOCK), lambda b, hh, i: (0, 0)),
        ],
        out_specs=pl.BlockSpec((Q_BLOCK, hw), lambda b, hh, i: (b * qb + i, hh)),
        out_shape=jax.ShapeDtypeStruct((bsz * seq, SB_DIM), F32),
        compiler_params=pltpu.CompilerParams(
            dimension_semantics=("parallel", "parallel", "arbitrary"), vmem_limit_bytes=VMEM_LIMIT_BYTES),
        name="sb_prompt",
    )(h, h, h, jnp.asarray(tri, BF16))


CONV_HALO = 8


def _split3(x):
    x1 = x.astype(BF16)
    r = x - x1.astype(F32)
    x2 = r.astype(BF16)
    x3 = (r - x2.astype(F32)).astype(BF16)
    return x1, x2, x3


def _ssd_body(z_ref, x_ref, bc_ref, xh_ref, bch_ref, cv_ref, tail_ref, h0_ref, cw_ref, cb_ref, dtb_r_ref, alog_r_ref,
              dtb_c_ref, alog_c_ref, dsk_ref, ng_ref, tril_ref, y_ref, ht_ref, st_ref, *, n_valid):
    c = pl.program_id(1)
    q = SSM_CHUNK
    p_dim = SSM_HEAD_DIM
    hpg = SSM_HEADS // SSM_GROUPS
    gn = SSM_GROUPS * SSM_STATE

    @pl.when(c == 0)
    def _():
        st_ref[...] = h0_ref[0]

    first = c == 0
    cv = cv_ref[0]
    xe = jnp.concatenate([jnp.where(first, cv[:, :SSM_D_INNER], xh_ref[...]), x_ref[...]], axis=0)
    bce = jnp.concatenate([jnp.where(first, cv[:, SSM_D_INNER:], bch_ref[...]), bc_ref[...]], axis=0)

    def conv(e, lo, hi):
        acc = cb_ref[:, lo:hi]
        for k in range(SSM_CONV):
            r0 = CONV_HALO - (SSM_CONV - 1) + k
            acc = acc + e[r0:r0 + q, :] * cw_ref[k:k + 1, lo:hi]
        return acc * jax.nn.sigmoid(acc)

    xs = conv(xe, 0, SSM_D_INNER)
    bcm = conv(bce, SSM_D_INNER, SSM_CONV_DIM)

    tail = tail_ref[...]
    t_row = lax.broadcasted_iota(jnp.int32, (q, LANES), 0)
    t_lane = lax.broadcasted_iota(jnp.int32, (LANES, q), 1)
    dt_r = jnp.where(t_row < n_valid, _softplus1p(tail + dtb_r_ref[...]), 0.0)
    dt_c = jnp.where(t_lane < n_valid, _softplus1p(tail.T + dtb_c_ref[...]), 0.0)
    da_r = dt_r * (-jnp.exp(alog_r_ref[...]))
    da_c = dt_c * (-jnp.exp(alog_c_ref[...]))
    tril = tril_ref[...]
    a1, a2, a3 = _split3(da_r)
    acum = (_dot(tril, a1) + _dot(tril, a2)) + _dot(tril, a3)
    c1, c2, c3 = _split3(da_c)
    acum_t = (_dot_nt(c1, tril) + _dot_nt(c2, tril)) + _dot_nt(c3, tril)
    last = acum[q - 1:q, :]
    causal = lax.broadcasted_iota(jnp.int32, (q, q), 1) <= lax.broadcasted_iota(jnp.int32, (q, q), 0)

    for g in range(SSM_GROUPS):
        bg = bcm[:, g * SSM_STATE:(g + 1) * SSM_STATE]
        cg = bcm[:, gn + g * SSM_STATE:gn + (g + 1) * SSM_STATE].astype(BF16)
        cbg = _dot_nt(cg, bg.astype(BF16))
        bg_t = bg.T.astype(BF16)
        for hh in range(hpg):
            h = g * hpg + hh
            col = acum[:, h:h + 1]
            decay = jnp.exp(jnp.where(causal, col - acum_t[h:h + 1, :], NEG_INF))
            xh = xs[:, h * p_dim:(h + 1) * p_dim]
            xdt = xh * dt_r[:, h:h + 1]
            y_diag = _dot((cbg * decay).astype(BF16), xdt.astype(BF16))
            s_t = st_ref[h]
            y_off = _dot(cg, s_t.astype(BF16)) * jnp.exp(col)
            end = last[:, h:h + 1]
            st_ref[h] = s_t * jnp.exp(end) + _dot(bg_t, (xdt * jnp.exp(end - col)).astype(BF16))
            y_ref[:, h * p_dim:(h + 1) * p_dim] = (y_diag + y_off) + dsk_ref[:, h * p_dim:(h + 1) * p_dim] * xh

    z = z_ref[...]
    y = y_ref[...] * (z * jax.nn.sigmoid(z))
    gw = SSM_D_INNER // SSM_GROUPS
    for g in range(SSM_GROUPS):
        yg = y[:, g * gw:(g + 1) * gw]
        yg = yg * lax.rsqrt(jnp.mean(yg * yg, axis=-1, keepdims=True) + EPS)
        y_ref[:, g * gw:(g + 1) * gw] = yg * ng_ref[:, g * gw:(g + 1) * gw]

    @pl.when(c == pl.num_programs(1) - 1)
    def _():
        ht_ref[0] = st_ref[...]


def _softplus1p(x):
    return jnp.maximum(x, 0.0) + jnp.log1p(jnp.exp(-jnp.abs(x)))


def ssd_mix(main, tail, conv_halo, h0_t, pw, bsz, seq, n_valid):
    q = SSM_CHUNK
    nc = seq // q
    per = q // CONV_HALO
    bcw = SSM_CONV_DIM - SSM_D_INNER
    pad_h = LANES - SSM_HEADS
    dtb = jnp.pad(pw['dt_bias'], (0, pad_h))
    alog = jnp.pad(pw['a_log'], (0, pad_h))
    tt = np.arange(q)
    tril = (tt[None, :] <= tt[:, None]).astype(np.float32)
    halo = lambda b, c: (jnp.maximum((b * nc + c) * per - 1, 0), 0)
    st_shape = (SSM_HEADS, SSM_STATE, SSM_HEAD_DIM)
    vec = lambda n: pl.BlockSpec((1, n), lambda b, c: (0, 0))
    colv = pl.BlockSpec((LANES, 1), lambda b, c: (0, 0))
    y, ht = pl.pallas_call(
        functools.partial(_ssd_body, n_valid=n_valid),
        grid=(bsz, nc),
        in_specs=[
            pl.BlockSpec((q, SSM_D_INNER), lambda b, c: (b * nc + c, 0)),
            pl.BlockSpec((q, SSM_D_INNER), lambda b, c: (b * nc + c, 1)),
            pl.BlockSpec((q, bcw), lambda b, c: (b * nc + c, 2 * SSM_D_INNER // bcw)),
            pl.BlockSpec((CONV_HALO, SSM_D_INNER), lambda b, c: (halo(b, c)[0], 1)),
            pl.BlockSpec((CONV_HALO, bcw), lambda b, c: (halo(b, c)[0], 2 * SSM_D_INNER // bcw)),
            pl.BlockSpec((1, CONV_HALO, SSM_CONV_DIM), lambda b, c: (b, 0, 0)),
            pl.BlockSpec((q, LANES), lambda b, c: (b * nc + c, 0)),
            pl.BlockSpec((1,) + st_shape, lambda b, c: (b, 0, 0, 0)),
            pl.BlockSpec((SSM_CONV, SSM_CONV_DIM), lambda b, c: (0, 0)),
            vec(SSM_CONV_DIM), vec(LANES), vec(LANES), colv, colv, vec(SSM_D_INNER), vec(SSM_D_INNER),
            pl.BlockSpec((q, q), lambda b, c: (0, 0)),
        ],
        out_specs=[pl.BlockSpec((q, SSM_D_INNER), lambda b, c: (b * nc + c, 0)),
                   pl.BlockSpec((1,) + st_shape, lambda b, c: (b, 0, 0, 0))],
        out_shape=[jax.ShapeDtypeStruct((bsz * seq, SSM_D_INNER), F32),
                   jax.ShapeDtypeStruct((bsz,) + st_shape, F32)],
        scratch_shapes=[pltpu.VMEM(st_shape, F32)],
        compiler_params=pltpu.CompilerParams(
            dimension_semantics=("parallel", "arbitrary"), vmem_limit_bytes=VMEM_LIMIT_BYTES),
        name="ssd_mix",
    )(main, main, main, main, main, conv_halo, tail, h0_t, pw['conv_w'], pw['conv_b'].reshape(1, -1),
      dtb.reshape(1, LANES), alog.reshape(1, LANES), dtb.reshape(LANES, 1), alog.reshape(LANES, 1),
      jnp.repeat(pw['d_skip'], SSM_HEAD_DIM).reshape(1, -1), pw['ssm_g'].reshape(1, -1), jnp.asarray(tril, BF16))
    return y, ht


POOL_HALO = 16


def _pool_body(u_ref, halo_ref, buf_ref, w_ref, sc_ref, o_ref, *, pos0):
    i = pl.program_id(1)
    tl = u_ref.shape[0]
    u = u_ref[...]
    prev = jnp.where(i == 0, buf_ref[0], halo_ref[...])
    x = jnp.concatenate([prev, u], axis=0)
    pos = pos0 + i * tl + lax.broadcasted_iota(jnp.int32, (tl, 1), 0)
    for g, w in enumerate(POOL_WINDOWS):
        cols = slice(g * POOL_GROUP_DIM, (g + 1) * POOL_GROUP_DIM)
        s = x[:, cols]
        span = 1
        while span < w:
            s = s[span:, :] + s[:-span, :]
            span *= 2
        win = s[POOL_HALO - (w - 1):, :]
        cnt = jnp.minimum(pos + 1, w).astype(F32)
        diff = win / cnt - u[:, cols]
        y = _dot(diff.astype(BF16), w_ref[g])
        o_ref[:, cols] = y * sc_ref[:, cols]


def pool_mix_pallas(h, buf16, pool_w, pool_scale, bsz, seq, pos0, *, tl=512):
    tl = min(tl, seq)
    nt = seq // tl
    per = max(tl // POOL_HALO, 1)
    if tl < POOL_HALO:
        assert nt == 1
        halo_src = buf16.reshape(bsz * POOL_HALO, POOL_DIM)
        halo_map = lambda b, i: (b, 0)
    else:
        halo_src = h
        halo_map = lambda b, i: (jnp.maximum((b * nt + i) * per - 1, 0), 0)
    return pl.pallas_call(
        functools.partial(_pool_body, pos0=pos0),
        grid=(bsz, nt),
        in_specs=[
            pl.BlockSpec((tl, POOL_DIM), lambda b, i: (b * nt + i, 0)),
            pl.BlockSpec((POOL_HALO, POOL_DIM), halo_map),
            pl.BlockSpec((1, POOL_HALO, POOL_DIM), lambda b, i: (b, 0, 0)),
            pl.BlockSpec((POOL_GROUPS, POOL_GROUP_DIM, POOL_GROUP_DIM), lambda b, i: (0, 0, 0)),
            pl.BlockSpec((1, POOL_DIM), lambda b, i: (0, 0)),
        ],
        out_specs=pl.BlockSpec((tl, POOL_DIM), lambda b, i: (b * nt + i, 0)),
        out_shape=jax.ShapeDtypeStruct((bsz * seq, POOL_DIM), F32),
        compiler_params=pltpu.CompilerParams(
            dimension_semantics=("parallel", "arbitrary"), vmem_limit_bytes=VMEM_LIMIT_BYTES),
        name="pool_mix",
    )(h, halo_src, buf16, pool_w.astype(BF16), pool_scale.reshape(1, POOL_DIM))


def _sb_decode_body(pt_ref, qbd_ref, new_ref, page_ref, tri_ref, o_ref, run_ref, acc_ref, *, n_tok):
    p = pl.program_id(1)
    rows = SB_HEADS * n_tok
    tri = tri_ref[...]
    qbd = qbd_ref[0]

    def page_update(kv_ref, masked):
        z = _dot_nt(qbd, kv_ref[0, :, :SB_DIM].astype(BF16))
        sp = _softplus(z)
        if masked:
            tok = lax.broadcasted_iota(jnp.int32, (rows, PAGE_SIZE), 0) % n_tok
            key = lax.broadcasted_iota(jnp.int32, (rows, PAGE_SIZE), 1)
            strict = key < tok
            go = jnp.where(strict, sp, 0.0)
        else:
            go = sp
        hi = go.astype(BF16)
        lo = (go - hi.astype(F32)).astype(BF16)
        ct = _dot(hi, tri) + _dot(lo, tri)
        run = run_ref[...]
        att = jnp.exp((z - sp) - ((ct[:, :PAGE_SIZE] - go) + run))
        if masked:
            att = jnp.where(strict, att, 0.0)
        run_ref[...] = run + ct[:, PAGE_SIZE:]
        acc_ref[...] += _dot(att.astype(BF16), kv_ref[0, :, SB_DIM:].astype(BF16))

    @pl.when(p == 0)
    def _():
        run_ref[...] = jnp.zeros_like(run_ref)
        acc_ref[...] = jnp.zeros_like(acc_ref)
        page_update(new_ref, True)

    page_update(page_ref, False)

    @pl.when(p == pl.num_programs(1) - 1)
    def _():
        for h in range(SB_HEADS):
            o_ref[0, :, h * SB_HEAD_DIM:(h + 1) * SB_HEAD_DIM] = (
                acc_ref[h * n_tok:(h + 1) * n_tok, h * SB_HEAD_DIM:(h + 1) * SB_HEAD_DIM])


def sb_decode(h, cache, page_table, bsz, n_tok):
    n_pages = page_table.shape[1]
    rows = SB_HEADS * n_tok
    assert rows == PAGE_SIZE and n_tok <= PAGE_SIZE
    h3 = h.reshape(bsz, n_tok, -1)
    q = h3[..., POOL_DIM:POOL_DIM + SB_DIM].reshape(bsz, n_tok, SB_HEADS, SB_HEAD_DIM) * (SB_HEAD_DIM ** -0.5)
    eye = jnp.eye(SB_HEADS, dtype=F32)
    qbd = jnp.einsum('bthd,hg->bhtgd', q, eye).reshape(bsz, rows, SB_DIM).astype(BF16)
    new_kv = jnp.pad(h3[..., POOL_DIM + SB_DIM:], ((0, 0), (0, PAGE_SIZE - n_tok), (0, 0)))
    jj = np.arange(PAGE_SIZE)
    tri = np.concatenate([(jj[:, None] >= jj[None, :]).astype(np.float32),
                          np.ones((PAGE_SIZE, PAGE_SIZE), np.float32)], axis=1)
    return pl.pallas_call(
        functools.partial(_sb_decode_body, n_tok=n_tok),
        grid_spec=pltpu.PrefetchScalarGridSpec(
            num_scalar_prefetch=1,
            grid=(bsz, n_pages),
            in_specs=[
                pl.BlockSpec((1, rows, SB_DIM), lambda b, p, pt: (b, 0, 0)),
                pl.BlockSpec((1, PAGE_SIZE, 2 * SB_DIM), lambda b, p, pt: (b, 0, 0)),
                pl.BlockSpec((1, PAGE_SIZE, 2 * SB_DIM), lambda b, p, pt: (pt[b, n_pages - 1 - p], 0, 0)),
                pl.BlockSpec((PAGE_SIZE, 2 * PAGE_SIZE), lambda b, p, pt: (0, 0)),
            ],
            out_specs=pl.BlockSpec((1, n_tok, SB_DIM), lambda b, p, pt: (b, 0, 0)),
            scratch_shapes=[pltpu.VMEM((rows, PAGE_SIZE), F32), pltpu.VMEM((rows, SB_DIM), F32)],
        ),
        out_shape=jax.ShapeDtypeStruct((bsz, n_tok, SB_DIM), F32),
        compiler_params=pltpu.CompilerParams(
            dimension_semantics=("parallel", "arbitrary"), vmem_limit_bytes=VMEM_LIMIT_BYTES),
        name="sb_decode",
    )(page_table, qbd, new_kv, cache, jnp.asarray(tri, BF16))


def _rel_bucket_np(dist):
    n = np.maximum(dist, 0)
    exact = REL_BUCKETS // 2
    nf = np.maximum(n, 1).astype(np.float32)
    log_b = exact + (np.log(nf / np.float32(exact)) / np.float32(math.log(REL_MAX_DIST / exact))
                     * np.float32(REL_BUCKETS - exact)).astype(np.int32)
    return np.where(n < exact, n, np.minimum(log_b, REL_BUCKETS - 1)).astype(np.int32)


def nsa_bias_tables(rel_bias, seq):
    s = np.arange(Q_BLOCK)
    far_from = (CMP_BAND // 2 + 1) * CMP_STRIDE - (CMP_BLOCK - 1)
    assert (_rel_bucket_np(np.arange(far_from, 2 * seq)) == REL_BUCKETS - 1).all()
    near_idx = _rel_bucket_np(np.arange(2)[:, None, None] * Q_BLOCK + s[None, None, :] - s[None, :, None])
    near = rel_bias[jnp.asarray(near_idx)]
    near = near.reshape(2, Q_BLOCK, Q_BLOCK, NSA_GROUPS, NSA_HPG).transpose(3, 0, 1, 4, 2)
    near = near.reshape(NSA_GROUPS, 2, Q_BLOCK, NSA_HPG * Q_BLOCK)
    far = jnp.repeat(rel_bias[REL_BUCKETS - 1].reshape(NSA_GROUPS, 1, NSA_HPG), Q_BLOCK, axis=-1)
    far = far.reshape(NSA_GROUPS, 1, NSA_HPG * Q_BLOCK)
    m_rel = np.arange(-CMP_BAND // 2, CMP_BAND // 2).reshape(CMP_BAND, 1)
    dist_c = s.reshape(1, Q_BLOCK) - (m_rel * CMP_STRIDE + CMP_BLOCK - 1)
    cb = rel_bias[jnp.asarray(_rel_bucket_np(dist_c))]
    cb = jnp.where(jnp.asarray(dist_c >= 0)[..., None], cb, NEG_INF)
    cb = cb.reshape(CMP_BAND, Q_BLOCK, NSA_GROUPS, NSA_HPG).transpose(2, 0, 3, 1)
    cb = cb.reshape(NSA_GROUPS, CMP_BAND, NSA_HPG * Q_BLOCK)
    return near, far, cb


def _nsa_prompt_body(q_ref, tail_ref, ks_ref, vs_ref, kw_ref, vw_ref, kcmp_ref, vcmpt_ref, cb_ref, nb_ref, fb_ref,
                     at_ref, o_ref, vst_ref, vwt_ref, sel_ref, gt_ref, cbs_ref, *, seq):
    g = pl.program_id(1)
    i = pl.program_id(2)
    blk = Q_BLOCK
    hw = NSA_HPG * blk
    n_slc = seq // SLC_BLOCK

    @pl.when(i == 0)
    def _():
        for c in range(seq // blk):
            vst_ref[:, c * blk:(c + 1) * blk] = vs_ref[c * blk:(c + 1) * blk, :].T.astype(BF16)
            vwt_ref[:, c * blk:(c + 1) * blk] = vw_ref[c * blk:(c + 1) * blk, :].T.astype(BF16)

    q4 = jnp.concatenate([q_ref[:, h * blk:(h + 1) * blk] for h in range(NSA_HPG)], axis=0).astype(BF16)

    n_pad = seq // CMP_STRIDE
    half = CMP_BAND // 2
    rows = lax.broadcasted_iota(jnp.int32, (n_pad + half, hw), 0)
    cbs_ref[...] = jnp.where(rows < half * i, jnp.broadcast_to(fb_ref[0], (n_pad + half, hw)), NEG_INF)
    cbs_ref[pl.ds(pl.multiple_of(half * i, half), CMP_BAND), :] = cb_ref[0]
    cb = cbs_ref[half:, :]
    sc = _dot_nt(kcmp_ref[0, 0].astype(BF16), q4) + cb
    mc = jnp.max(sc, axis=0, keepdims=True)
    pc = jnp.exp(sc - mc)
    pc = pc / jnp.sum(pc, axis=0, keepdims=True)
    pc = pc * jnp.where(cb > 0.5 * NEG_INF, 1.0, 0.0)
    o_c = _dot(vcmpt_ref[0, 0].astype(BF16), pc.astype(BF16))
    psum = pc[:, 0:blk]
    for h in range(1, NSA_HPG):
        psum = psum + pc[:, h * blk:(h + 1) * blk]
    p1 = psum.astype(BF16)
    r1 = psum - p1.astype(F32)
    p2 = r1.astype(BF16)
    p3 = (r1 - p2.astype(F32)).astype(BF16)
    at = at_ref[...]
    imp = (_dot(at, p1) + _dot(at, p2)) + _dot(at, p3)

    jdx = lax.broadcasted_iota(jnp.int32, (n_slc, blk), 0)
    tpos = i * blk + lax.broadcasted_iota(jnp.int32, (n_slc, blk), 1)
    cur = tpos // SLC_BLOCK
    forced = jnp.where(jdx == 0, 1.0, 0.0) + jnp.where(jdx == cur, 1.0, 0.0) + jnp.where(jdx == cur - 1, 1.0, 0.0)
    score = jnp.where(forced > 0.0, FORCE_SCORE, jnp.where(jdx <= cur, imp, -1.0))
    rank = jnp.zeros((n_slc, blk), F32)
    for r in range(n_slc):
        row = jnp.broadcast_to(score[r:r + 1, :], (n_slc, blk))
        gt = jnp.where(row > score, 1.0, 0.0)
        ge = jnp.where(row >= score, 1.0, 0.0)
        rank = rank + jnp.where(jdx > r, ge, gt)
    sel_ref[...] = jnp.where(rank < float(min(SLC_TOPN, n_slc)), 1.0, 0.0)

    ss = lax.broadcasted_iota(jnp.int32, (blk, blk), 0)
    tt = lax.broadcasted_iota(jnp.int32, (blk, blk), 1)
    causal = jnp.where(ss <= tt, 1.0, 0.0)
    anti = jnp.where(ss >= tt, 1.0, 0.0)
    bias_far = fb_ref[0]

    def tile4(mk):
        return jnp.concatenate([mk] * NSA_HPG, axis=1)

    def scores(k_ref, kb, bias):
        r0 = pl.multiple_of(kb * blk, blk)
        return _dot_nt(k_ref[pl.ds(r0, blk), :].astype(BF16), q4) + bias

    def pv(vt_ref, kb, p):
        r0 = pl.multiple_of(kb * blk, blk)
        return _dot(vt_ref[:, pl.ds(r0, blk)], p.astype(BF16))

    def first(s, mk):
        s = jnp.where(tile4(mk) > 0.0, s, NEG_INF)
        m = jnp.max(s, axis=0, keepdims=True)
        p = jnp.exp(s - m)
        return m, jnp.sum(p, axis=0, keepdims=True), p

    def update(carry, s, mk, vt_ref, kb):
        m, l, acc = carry
        s = jnp.where(tile4(mk) > 0.0, s, NEG_INF)
        m_new = jnp.maximum(m, jnp.max(s, axis=0, keepdims=True))
        alpha = jnp.exp(m - m_new)
        p = jnp.exp(s - m_new)
        return m_new, alpha * l + jnp.sum(p, axis=0, keepdims=True), alpha * acc + pv(vt_ref, kb, p)

    def sel_mask(kb):
        r0 = sel_ref[pl.ds(2 * kb, 1), :]
        r1 = sel_ref[pl.ds(2 * kb + 1, 1), :]
        half = blk // 2
        return jnp.concatenate([jnp.broadcast_to(r0, (half, blk)), jnp.broadcast_to(r1, (half, blk))], axis=0)

    m, l, p = first(scores(ks_ref, i, nb_ref[0, 0]), sel_mask(i) * causal)
    carry = (m, l, pv(vst_ref, i, p))
    kb1 = jnp.maximum(i - 1, 0)
    ok1 = jnp.where(i >= 1, 1.0, 0.0)
    carry = update(carry, scores(ks_ref, kb1, nb_ref[0, 1]), sel_mask(kb1) * ok1, vst_ref, kb1)

    def far_body(kb, c):
        return update(c, scores(ks_ref, kb, bias_far), sel_mask(kb), vst_ref, kb)

    m, l, acc = lax.fori_loop(0, jnp.maximum(i - 1, 0), far_body, carry)
    o_s = acc / l

    m, l, p = first(scores(kw_ref, i, nb_ref[0, 0]), causal)
    carry = (m, l, pv(vwt_ref, i, p))
    ones = jnp.ones((blk, blk), F32)
    for d in range(1, WINDOW // blk + 1):
        kb = jnp.maximum(i - d, 0)
        ok = jnp.where(i >= d, 1.0, 0.0)
        bias = nb_ref[0, 1] if d == 1 else bias_far
        mk = (anti if d == WINDOW // blk else ones) * ok
        carry = update(carry, scores(kw_ref, kb, bias), mk, vwt_ref, kb)
    m, l, acc = carry
    o_w = acc / l

    gt_ref[...] = tail_ref[...].T

    def gate(c):
        rows = [gt_ref[pl.ds(SSM_HEADS + 3 * (NSA_HPG * g + h) + c, 1), :] for h in range(NSA_HPG)]
        return jax.nn.sigmoid(jnp.concatenate(rows, axis=1))

    o_t = (gate(0) * o_c + gate(1) * o_s) + gate(2) * o_w
    for h in range(NSA_HPG):
        o_ref[:, h * blk:(h + 1) * blk] = o_t[:, h * blk:(h + 1) * blk].T


def nsa_prompt(main, tail, kcmp, vcmpt, tables, bsz, seq):
    near, far, cb = tables
    qb = seq // Q_BLOCK
    n_pad = seq // CMP_STRIDE
    n_slc = seq // SLC_BLOCK
    hw = NSA_HPG * Q_BLOCK
    ratio = SLC_BLOCK // CMP_STRIDE
    lo = CMP_BLOCK // CMP_STRIDE - 1
    jj = np.arange(n_slc)[:, None]
    nn = np.arange(n_pad)[None, :]
    a_t = ((nn >= ratio * jj - lo) & (nn <= ratio * jj + ratio - 1) & (nn < n_pad - 1)).astype(np.float32)
    dh = NSA_HEAD_DIM
    col = lambda off: off // dh
    return pl.pallas_call(
        functools.partial(_nsa_prompt_body, seq=seq),
        grid=(bsz, NSA_GROUPS, qb),
        in_specs=[
            pl.BlockSpec((Q_BLOCK, hw), lambda b, g, i: (b * qb + i, OFF_Q // hw + g)),
            pl.BlockSpec((Q_BLOCK, EVEN_TAIL), lambda b, g, i: (b * qb + i, 0)),
            pl.BlockSpec((seq, dh), lambda b, g, i: (b, col(OFF_KV + 2 * NSA_KV) + g)),
            pl.BlockSpec((seq, dh), lambda b, g, i: (b, col(OFF_KV + 3 * NSA_KV) + g)),
            pl.BlockSpec((seq, dh), lambda b, g, i: (b, col(OFF_WIN) + g)),
            pl.BlockSpec((seq, dh), lambda b, g, i: (b, col(OFF_WIN + NSA_KV) + g)),
            pl.BlockSpec((1, 1, n_pad, dh), lambda b, g, i: (b, g, 0, 0)),
            pl.BlockSpec((1, 1, dh, n_pad), lambda b, g, i: (b, g, 0, 0)),
            pl.BlockSpec((1, CMP_BAND, hw), lambda b, g, i: (g, 0, 0)),
            pl.BlockSpec((1, 2, Q_BLOCK, hw), lambda b, g, i: (g, 0, 0, 0)),
            pl.BlockSpec((1, 1, hw), lambda b, g, i: (g, 0, 0)),
            pl.BlockSpec((n_slc, n_pad), lambda b, g, i: (0, 0)),
        ],
        out_specs=pl.BlockSpec((Q_BLOCK, hw), lambda b, g, i: (b * qb + i, g)),
        out_shape=jax.ShapeDtypeStruct((bsz * seq, NSA_Q), F32),
        scratch_shapes=[pltpu.VMEM((dh, seq), BF16), pltpu.VMEM((dh, seq), BF16),
                        pltpu.VMEM((n_slc, Q_BLOCK), F32), pltpu.VMEM((EVEN_TAIL, Q_BLOCK), F32),
                        pltpu.VMEM((n_pad + CMP_BAND // 2, hw), F32)],
        compiler_params=pltpu.CompilerParams(
            dimension_semantics=("parallel", "parallel", "arbitrary"), vmem_limit_bytes=VMEM_LIMIT_BYTES),
        name="nsa_prompt",
    )(main, tail, main, main, main, main, kcmp, vcmpt, cb, near, far, jnp.asarray(a_t, BF16))


CMP_COLS = 2 * NSA_KV
SUBS_PER_PAGE = PAGE_SIZE // CMP_STRIDE


def _cmp_pool_body(idx_ref, src_ref, a1_ref, a2_ref, o_ref, prev_ref):
    p = pl.program_id(1)
    x = src_ref[0]
    first = (x * a1_ref[...]).reshape(SUBS_PER_PAGE, CMP_STRIDE, CMP_COLS).sum(axis=1)
    second = (x * a2_ref[...]).reshape(SUBS_PER_PAGE, CMP_STRIDE, CMP_COLS).sum(axis=1)
    prev = jnp.where(p == 0, 0.0, prev_ref[...])
    shifted = jnp.concatenate([prev[SUBS_PER_PAGE - 1:, :], first[:SUBS_PER_PAGE - 1, :]], axis=0)
    o_ref[0] = shifted + second
    prev_ref[...] = first


def cmp_pool(src, block_idx, col_block, cmp_alpha):
    bsz, n = block_idx.shape

    def tiled(half):
        a = jnp.concatenate([jnp.tile(cmp_alpha[0][half], (1, NSA_GROUPS)),
                             jnp.tile(cmp_alpha[1][half], (1, NSA_GROUPS))], axis=1)
        return jnp.tile(a, (SUBS_PER_PAGE, 1))

    a1 = tiled(slice(0, CMP_STRIDE))
    a2 = tiled(slice(CMP_STRIDE, CMP_BLOCK))
    return pl.pallas_call(
        _cmp_pool_body,
        grid_spec=pltpu.PrefetchScalarGridSpec(
            num_scalar_prefetch=1,
            grid=(bsz, n),
            in_specs=[
                pl.BlockSpec((1, PAGE_SIZE, CMP_COLS), lambda b, p, idx: (idx[b, p], 0, col_block)),
                pl.BlockSpec((PAGE_SIZE, CMP_COLS), lambda b, p, idx: (0, 0)),
                pl.BlockSpec((PAGE_SIZE, CMP_COLS), lambda b, p, idx: (0, 0)),
            ],
            out_specs=pl.BlockSpec((1, SUBS_PER_PAGE, CMP_COLS), lambda b, p, idx: (b, p, 0)),
            scratch_shapes=[pltpu.VMEM((SUBS_PER_PAGE, CMP_COLS), F32)],
        ),
        out_shape=jax.ShapeDtypeStruct((bsz, n * SUBS_PER_PAGE, CMP_COLS), F32),
        compiler_params=pltpu.CompilerParams(
            dimension_semantics=("parallel", "arbitrary"), vmem_limit_bytes=VMEM_LIMIT_BYTES),
        name="cmp_pool",
    )(block_idx, src, a1, a2)


def _cmp_project_body(p_ref, wk_ref, wv_ref, gk_ref, k_ref, vt_ref):
    n = p_ref.shape[1]
    dh = NSA_HEAD_DIM
    zero = jnp.zeros((1, dh), F32)
    pk = jnp.concatenate([p_ref[0, 1:, 0:dh], zero], axis=0)
    pv = jnp.concatenate([p_ref[0, 1:, dh:2 * dh], zero], axis=0)
    kp = _dot(pk.astype(BF16), wk_ref[...])
    k_ref[0, 0] = (kp * lax.rsqrt(jnp.mean(kp * kp, axis=-1, keepdims=True) + EPS)) * gk_ref[...]
    vp = _dot(pv.astype(BF16), wv_ref[...])
    for c in range(n // LANES):
        vt_ref[0, 0, :, c * LANES:(c + 1) * LANES] = vp[c * LANES:(c + 1) * LANES, :].T


def cmp_project(pooled, cmp_w, g_kcmp):
    bsz, n, _ = pooled.shape
    dh = NSA_HEAD_DIM
    pg = pooled.reshape(bsz, n, 2, NSA_GROUPS, dh).transpose(0, 3, 1, 2, 4).reshape(bsz * NSA_GROUPS, n, 2 * dh)
    k, vt = pl.pallas_call(
        _cmp_project_body,
        grid=(bsz, NSA_GROUPS),
        in_specs=[
            pl.BlockSpec((1, n, 2 * dh), lambda b, g: (b * NSA_GROUPS + g, 0, 0)),
            pl.BlockSpec((dh, dh), lambda b, g: (0, 0)),
            pl.BlockSpec((dh, dh), lambda b, g: (0, 0)),
            pl.BlockSpec((1, dh), lambda b, g: (0, 0)),
        ],
        out_specs=[pl.BlockSpec((1, 1, n, dh), lambda b, g: (b, g, 0, 0)),
                   pl.BlockSpec((1, 1, dh, n), lambda b, g: (b, g, 0, 0))],
        out_shape=[jax.ShapeDtypeStruct((bsz, NSA_GROUPS, n, dh), F32),
                   jax.ShapeDtypeStruct((bsz, NSA_GROUPS, dh, n), F32)],
        compiler_params=pltpu.CompilerParams(
            dimension_semantics=("parallel", "parallel"), vmem_limit_bytes=VMEM_LIMIT_BYTES),
        name="cmp_project",
    )(pg, cmp_w[0].astype(BF16), cmp_w[1].astype(BF16), g_kcmp.reshape(1, dh))
    return k, vt


def _nsa_decode_body(pt_ref, qbd_ref, pool_ref, wk_ref, wv_ref, gk_ref, cbias_ref, at_ref, gsum_ref, gexp_ref,
                     new_ref, nbias_ref, page_ref, lbias_ref, far_ref, win_ref, wbias_ref, gate_ref, o_ref,
                     kc_ref, vct_ref, oc_ref, score_ref, sel_ref, m_ref, l_ref, acc_ref, *, n_tok, pos0):
    s = pl.program_id(1)
    n_pages = pl.num_programs(1) - 1
    dh = NSA_HEAD_DIM
    gd = NSA_GROUPS * dh
    lanes = NSA_HEADS * n_tok
    n_cmp = pool_ref.shape[1]
    n_rows = score_ref.shape[0]
    qbd = qbd_ref[0]

    def kv_t(v):
        return jnp.concatenate([v[:, g * dh:(g + 1) * dh].T for g in range(NSA_GROUPS)], axis=0).astype(BF16)

    def masked_update(sc, mask_rows, v, init):
        sc = jnp.where(mask_rows > 0.0, sc, NEG_INF)
        m_old = jnp.full((1, lanes), NEG_INF, F32) if init else m_ref[...]
        m_new = jnp.maximum(m_old, jnp.max(sc, axis=0, keepdims=True))
        p = jnp.exp(sc - m_new)
        pv = _dot(kv_t(v), p.astype(BF16))
        if init:
            l_ref[...] = jnp.sum(p, axis=0, keepdims=True)
            acc_ref[...] = pv
        else:
            alpha = jnp.exp(m_old - m_new)
            l_ref[...] = alpha * l_ref[...] + jnp.sum(p, axis=0, keepdims=True)
            acc_ref[...] = alpha * acc_ref[...] + pv
        m_ref[...] = m_new

    @pl.when(s == 0)
    def _():
        for g in range(NSA_GROUPS):
            kp = _dot(pool_ref[0, :, g * dh:(g + 1) * dh].astype(BF16), wk_ref[...])
            kp = (kp * lax.rsqrt(jnp.mean(kp * kp, axis=-1, keepdims=True) + EPS)) * gk_ref[...]
            kc_ref[:, g * dh:(g + 1) * dh] = kp.astype(BF16)
            vp = _dot(pool_ref[0, :, gd + g * dh:gd + (g + 1) * dh].astype(BF16), wv_ref[...])
            for c in range(n_cmp // LANES):
                vct_ref[g * dh:(g + 1) * dh, c * LANES:(c + 1) * LANES] = vp[c * LANES:(c + 1) * LANES, :].T.astype(BF16)
        cb = cbias_ref[...]
        sc = _dot(kc_ref[...], qbd) + cb
        pc = jnp.exp(sc - jnp.max(sc, axis=0, keepdims=True))
        pc = pc / jnp.sum(pc, axis=0, keepdims=True)
        pc = pc * jnp.where(cb > 0.5 * NEG_INF, 1.0, 0.0)
        oc_ref[...] = _dot(vct_ref[...], pc.astype(BF16))
        at = at_ref[...]
        p1, p2, p3 = _split3(pc)
        u = (_dot(at, p1) + _dot(at, p2)) + _dot(at, p3)
        gs = gsum_ref[...]
        u1, u2, u3 = _split3(u)
        imp = (_dot(u1, gs) + _dot(u2, gs)) + _dot(u3, gs)
        jdx = lax.broadcasted_iota(jnp.int32, (n_rows, lanes), 0)
        tok = lax.broadcasted_iota(jnp.int32, (n_rows, lanes), 1) % n_tok
        cur = (pos0 + tok) // SLC_BLOCK
        n_blocks = (pos0 + n_tok + SLC_BLOCK - 1) // SLC_BLOCK
        forced = (jnp.where(jdx == 0, 1.0, 0.0) + jnp.where(jdx == cur, 1.0, 0.0)
                  + jnp.where(jdx == cur - 1, 1.0, 0.0))
        score = jnp.where(forced > 0.0, FORCE_SCORE, jnp.where(jdx <= cur, imp, -1.0))
        score = jnp.where(jdx < n_blocks, score, -2.0)
        score_ref[...] = score

        def rank_body(r, rank):
            row = jnp.broadcast_to(score_ref[pl.ds(r, 1), :], (n_rows, lanes))
            gt = jnp.where(row > score, 1.0, 0.0)
            ge = jnp.where(row >= score, 1.0, 0.0)
            return rank + jnp.where(jdx > r, ge, gt)

        rank = lax.fori_loop(0, n_blocks, rank_body, jnp.zeros((n_rows, lanes), F32))
        sel = jnp.where(rank < float(SLC_TOPN), 1.0, 0.0).astype(BF16)
        sel_ref[...] = _dot(sel, gexp_ref[...])
        new = new_ref[0]
        sc = _dot(new[:, :gd].astype(BF16), qbd) + nbias_ref[...]
        rows = jnp.broadcast_to(sel_ref[pl.ds(n_blocks - 1, 1), :], (PAGE_SIZE, lanes))
        masked_update(sc, rows, new[:, gd:], True)

    @pl.when(s > 0)
    def _():
        p = s - 1
        page = page_ref[0]
        bias = jnp.where(p == n_pages - 1, lbias_ref[...], jnp.broadcast_to(far_ref[...], (PAGE_SIZE, lanes)))
        sc = _dot(page[:, :gd].astype(BF16), qbd) + bias
        half = PAGE_SIZE // 2
        rows = jnp.concatenate([jnp.broadcast_to(sel_ref[pl.ds(2 * p, 1), :], (half, lanes)),
                                jnp.broadcast_to(sel_ref[pl.ds(2 * p + 1, 1), :], (half, lanes))], axis=0)
        masked_update(sc, rows, page[:, gd:], False)

    @pl.when(s == n_pages)
    def _():
        o_s = acc_ref[...] / l_ref[...]
        win = win_ref[0]
        sw = _dot(win[:, :gd].astype(BF16), qbd) + wbias_ref[...]
        pw = jnp.exp(sw - jnp.max(sw, axis=0, keepdims=True))
        pw = pw / jnp.sum(pw, axis=0, keepdims=True)
        o_w = _dot(kv_t(win[:, gd:]), pw.astype(BF16))
        gates = jax.nn.sigmoid(gate_ref[0])
        o_t = (gates[0:1, :] * oc_ref[...] + gates[1:2, :] * o_s) + gates[2:3, :] * o_w
        for g in range(NSA_GROUPS):
            blk = o_t[g * dh:(g + 1) * dh, :].T
            for hh in range(NSA_HPG):
                h = g * NSA_HPG + hh
                o_ref[0, :, h * dh:(h + 1) * dh] = blk[h * n_tok:(h + 1) * n_tok, :]


def nsa_decode(main, tail, pooled, cache, page_table, win_state, pw, bsz, n_tok):
    n_pages = page_table.shape[1]
    pos0 = n_pages * PAGE_SIZE
    wb = win_state.shape[1]
    dh = NSA_HEAD_DIM
    gd = NSA_GROUPS * dh
    lanes = NSA_HEADS * n_tok
    assert lanes == LANES and pos0 % PAGE_SIZE == 0 and n_tok <= SLC_BLOCK
    n_cmp = pooled.shape[1]
    n_blocks = (pos0 + n_tok + SLC_BLOCK - 1) // SLC_BLOCK
    n_rows = -(-n_blocks // 8) * 8
    rel = pw['rel_table']
    m3 = main.reshape(bsz, n_tok, -1)
    q = m3[..., OFF_Q:OFF_KV].reshape(bsz, n_tok, NSA_HEADS, dh)
    grp = np.repeat(np.eye(NSA_GROUPS, dtype=np.float32), NSA_HPG, axis=0)
    qbd = jnp.einsum('bthd,hg->bgdht', q, jnp.asarray(grp)).reshape(bsz, gd, lanes).astype(BF16)
    t_l = np.tile(np.arange(n_tok), NSA_HEADS)[None, :]
    h_l = np.repeat(np.arange(NSA_HEADS), n_tok)

    def bias_table(dist, ok):
        tab = rel[jnp.asarray(_rel_bucket_np(dist)), jnp.asarray(np.broadcast_to(h_l[None, :], dist.shape))]
        return jnp.where(jnp.asarray(ok), tab, NEG_INF)

    r_c = np.arange(n_cmp)[:, None]
    dist_c = pos0 + t_l - ((r_c - 1) * CMP_STRIDE + CMP_BLOCK - 1)
    cbias = bias_table(dist_c, (r_c >= 1) & (dist_c >= 0))
    rr = np.arange(PAGE_SIZE)[:, None]
    nbias = bias_table(t_l - rr, (t_l - rr) >= 0)
    lbias = bias_table(PAGE_SIZE + t_l - rr, np.ones((PAGE_SIZE, lanes), bool))
    assert (_rel_bucket_np(np.arange(PAGE_SIZE + 1, 2 * PAGE_SIZE)) == REL_BUCKETS - 1).all()
    far = rel[REL_BUCKETS - 1][jnp.asarray(h_l)].reshape(1, lanes)
    n_win = -(-(wb + n_tok) // PAGE_SIZE) * PAGE_SIZE
    w_r = np.arange(n_win)[:, None]
    dist_w = pos0 + t_l - (pos0 - wb + w_r)
    wbias = bias_table(dist_w, (dist_w >= 0) & (dist_w <= WINDOW) & (pos0 - wb + w_r >= 0) & (w_r < wb + n_tok))
    ratio = SLC_BLOCK // CMP_STRIDE
    lo = CMP_BLOCK // CMP_STRIDE - 1
    jj = np.arange(n_rows)[:, None]
    nn = np.arange(n_cmp)[None, :] - 1
    a_t = ((nn >= ratio * jj - lo) & (nn <= ratio * jj + ratio - 1) & (nn >= 0) & (jj < n_blocks)).astype(np.float32)
    g_l = h_l // NSA_HPG
    gsum = np.zeros((lanes, lanes), np.float32)
    gsum[np.arange(lanes), g_l * n_tok + t_l[0]] = 1.0
    gexp = gsum.T.copy()
    new_kv = jnp.pad(m3[..., OFF_KV + 2 * NSA_KV:OFF_KV + 4 * NSA_KV], ((0, 0), (0, PAGE_SIZE - n_tok), (0, 0)))
    win_all = jnp.concatenate([win_state, m3[..., OFF_WIN:OFF_WIN + 2 * NSA_KV]], axis=1)
    win_all = jnp.pad(win_all, ((0, 0), (0, n_win - wb - n_tok), (0, 0)))
    gates = tail.reshape(bsz, n_tok, -1)[..., SSM_HEADS:SSM_HEADS + 3 * NSA_HEADS]
    gates = gates.reshape(bsz, n_tok, NSA_HEADS, 3).transpose(0, 3, 2, 1).reshape(bsz, 3, lanes)
    gates = jnp.pad(gates, ((0, 0), (0, 5), (0, 0)))
    const = lambda shape: pl.BlockSpec(shape, lambda b, s, pt: (0,) * len(shape))
    per_b = lambda shape: pl.BlockSpec((1,) + shape, lambda b, s, pt: (b,) + (0,) * len(shape))
    return pl.pallas_call(
        functools.partial(_nsa_decode_body, n_tok=n_tok, pos0=pos0),
        grid_spec=pltpu.PrefetchScalarGridSpec(
            num_scalar_prefetch=1,
            grid=(bsz, n_pages + 1),
            in_specs=[
                per_b((gd, lanes)), per_b((n_cmp, CMP_COLS)), const((dh, dh)), const((dh, dh)), const((1, dh)),
                const((n_cmp, lanes)), const((n_rows, n_cmp)), const((lanes, lanes)), const((lanes, lanes)),
                per_b((PAGE_SIZE, 2 * NSA_KV)), const((PAGE_SIZE, lanes)),
                pl.BlockSpec((1, PAGE_SIZE, 2 * NSA_KV), lambda b, s, pt: (pt[b, jnp.maximum(s - 1, 0)], 0, 1)),
                const((PAGE_SIZE, lanes)), const((1, lanes)),
                per_b((n_win, 2 * NSA_KV)), const((n_win, lanes)), per_b((8, lanes)),
            ],
            out_specs=pl.BlockSpec((1, n_tok, NSA_Q), lambda b, s, pt: (b, 0, 0)),
            scratch_shapes=[
                pltpu.VMEM((n_cmp, gd), BF16), pltpu.VMEM((gd, n_cmp), BF16), pltpu.VMEM((gd, lanes), F32),
                pltpu.VMEM((n_rows, lanes), F32), pltpu.VMEM((n_rows, lanes), F32),
                pltpu.VMEM((1, lanes), F32), pltpu.VMEM((1, lanes), F32), pltpu.VMEM((gd, lanes), F32),
            ],
        ),
        out_shape=jax.ShapeDtypeStruct((bsz, n_tok, NSA_Q), F32),
        compiler_params=pltpu.CompilerParams(
            dimension_semantics=("parallel", "arbitrary"), vmem_limit_bytes=VMEM_LIMIT_BYTES),
        name="nsa_decode",
    )(page_table, qbd, pooled, pw['cmp_w'][0].astype(BF16), pw['cmp_w'][1].astype(BF16),
      pw['g_kcmp'].reshape(1, dh), cbias, jnp.asarray(a_t, BF16), jnp.asarray(gsum, BF16), jnp.asarray(gexp, BF16),
      new_kv, nbias, cache, lbias, far, win_all, wbias, gates)


def split_cols(h, sizes):
    offs = np.cumsum(sizes)[:-1].tolist()
    return jnp.split(h, offs, axis=-1)


def even_prompt(x, bsz, seq, wb, pw):
    main, tail = in_proj(x, pw['g_mix'], pw['w_main'], pw['flags'], pw['post_gain'], pw['post_scale'], pw['w_tail'])
    y_a, h_t = ssd_mix(main, tail, jnp.zeros((bsz, CONV_HALO, SSM_CONV_DIM), F32),
                       jnp.zeros((bsz, SSM_HEADS, SSM_STATE, SSM_HEAD_DIM), F32), pw, bsz, seq, seq)
    nb = seq // PAGE_SIZE
    idx = (jnp.arange(bsz, dtype=jnp.int32)[:, None] * nb + jnp.arange(nb, dtype=jnp.int32)[None, :])
    pooled = cmp_pool(main.reshape(bsz * nb, PAGE_SIZE, EVEN_MAIN), idx, OFF_KV // CMP_COLS, pw['cmp_alpha'])
    kcmp, vcmpt = cmp_project(pooled, pw['cmp_w'], pw['g_kcmp'])
    o_b = nsa_prompt(main, tail, kcmp, vcmpt, pw['nsa_tables'], bsz, seq)
    y = out_proj(x, y_a, o_b, pw['w_out_a'], pw['w_out_b'])
    m3 = main.reshape(bsz, seq, EVEN_MAIN)
    kv = m3[..., OFF_KV:OFF_WIN].reshape(bsz, seq, NSA_KV_PARTS, NSA_GROUPS, NSA_HEAD_DIM)
    new_win = m3[:, seq - wb:, OFF_WIN:].reshape(bsz, wb, 2, NSA_GROUPS, NSA_HEAD_DIM)
    new_conv = m3[:, seq - (SSM_CONV - 1):, OFF_XBC:OFF_Q]
    return y, kv, new_win, new_conv, h_t.transpose(0, 1, 3, 2)


def even_decode(x, bsz, n_tok, cache, page_idx, win_state, conv_state, ssm_state, pw):
    wb = win_state.shape[1]
    main, tail = in_proj(x, pw['g_mix'], pw['w_main'], pw['flags'], pw['post_gain'], pw['post_scale'], pw['w_tail'])
    m3 = main.reshape(bsz, n_tok, EVEN_MAIN)
    pad_rows = ((0, 0), (0, SSM_CHUNK - n_tok), (0, 0))
    main_p = jnp.pad(m3[..., :OFF_Q], pad_rows).reshape(bsz * SSM_CHUNK, OFF_Q)
    tail_p = jnp.pad(tail.reshape(bsz, n_tok, EVEN_TAIL), pad_rows).reshape(bsz * SSM_CHUNK, EVEN_TAIL)
    halo = jnp.pad(conv_state, ((0, 0), (CONV_HALO - (SSM_CONV - 1), 0), (0, 0)))
    y_a, h_t = ssd_mix(main_p, tail_p, halo, ssm_state.transpose(0, 1, 3, 2), pw, bsz, SSM_CHUNK, n_tok)
    y_a = y_a.reshape(bsz, SSM_CHUNK, SSM_D_INNER)[:, :n_tok].reshape(bsz * n_tok, SSM_D_INNER)
    pooled = cmp_pool(cache, page_idx, 0, pw['cmp_alpha'])
    win2 = win_state.reshape(bsz, wb, 2 * NSA_KV)
    o_b = nsa_decode(main, tail, pooled, cache, page_idx, win2, pw, bsz, n_tok)
    y = out_proj(x, y_a, o_b.reshape(bsz * n_tok, NSA_Q), pw['w_out_a'], pw['w_out_b'])
    kv = m3[..., OFF_KV:OFF_WIN].reshape(bsz, n_tok, NSA_KV_PARTS, NSA_GROUPS, NSA_HEAD_DIM)
    new_win = jnp.concatenate([win2, m3[..., OFF_WIN:]], axis=1)[:, -wb:]
    new_win = new_win.reshape(bsz, wb, 2, NSA_GROUPS, NSA_HEAD_DIM)
    new_conv = jnp.concatenate([conv_state, m3[..., OFF_XBC:OFF_Q]], axis=1)[:, -(SSM_CONV - 1):]
    return y, kv, new_win, new_conv, h_t.transpose(0, 1, 3, 2)


def odd_prompt(x, bsz, seq, pw):
    h = in_proj(x, pw['g_mix'], pw['w_in'], pw['flags'], pw['post_gain'], pw['post_scale'])
    y_c = pool_mix_pallas(h, jnp.zeros((bsz, POOL_HALO, POOL_DIM), F32), pw['pool_w'], pw['pool_scale'], bsz, seq, 0)
    o = sb_prompt(h, bsz, seq)
    y = out_proj(x, y_c, o, pw['w_out_a'], pw['w_out_b'])
    h3 = h.reshape(bsz, seq, -1)
    kv = h3[..., POOL_DIM + SB_DIM:].reshape(bsz, seq, 2, SB_HEADS, SB_HEAD_DIM)
    return y, kv, h3[:, seq - POOL_BUF:, :POOL_DIM]


def odd_decode(x, bsz, n_tok, cache, page_idx, pool_state, pw):
    pos0 = page_idx.shape[1] * PAGE_SIZE
    h = in_proj(x, pw['g_mix'], pw['w_in'], pw['flags'], pw['post_gain'], pw['post_scale'])
    buf16 = jnp.pad(pool_state, ((0, 0), (POOL_HALO - POOL_BUF, 0), (0, 0)))
    y_c = pool_mix_pallas(h, buf16, pw['pool_w'], pw['pool_scale'], bsz, n_tok, pos0)
    o = sb_decode(h, cache, page_idx, bsz, n_tok)
    y = out_proj(x, y_c, o.reshape(bsz * n_tok, SB_DIM), pw['w_out_a'], pw['w_out_b'])
    h3 = h.reshape(bsz, n_tok, -1)
    kv = h3[..., POOL_DIM + SB_DIM:].reshape(bsz, n_tok, 2, SB_HEADS, SB_HEAD_DIM)
    new_pool = jnp.concatenate([pool_state, h3[..., :POOL_DIM]], axis=1)[:, -POOL_BUF:]
    return y, kv, new_pool


def _even_weights(l, e, mix_norm, w_in_even, w_out_even, ssm_conv_w, ssm_conv_b, ssm_dt_bias, ssm_a_log, ssm_d,
                  ssm_norm, nsa_cmp_alpha, nsa_cmp_w, nsa_qk_gain, rel_bias, nsa_tables, tn):
    w = w_in_even[e]
    sizes = (SSM_D_INNER, SSM_CONV_DIM, SSM_HEADS, NSA_Q) + (NSA_KV,) * 6 + (3 * NSA_HEADS,)
    wz, wxbc, wdt, wq, wkc, wvc, wks, wvs, wkw, wvw, wg = split_cols(w, sizes)
    w_main = jnp.concatenate([wz, wxbc, wq, wkc, wvc, wks, wvs, wkw, wvw], axis=1).astype(BF16)
    pad = EVEN_TAIL - SSM_HEADS - 3 * NSA_HEADS
    w_tail = jnp.concatenate([wdt, wg, jnp.zeros((D_MODEL, pad), F32)], axis=1).astype(BF16)
    gain = nsa_qk_gain[e]
    ones = jnp.ones((NSA_KV,), F32)
    post_gain = jnp.concatenate([
        jnp.ones((OFF_Q,), F32), jnp.tile(gain[0], NSA_HEADS), ones, ones, jnp.tile(gain[2], NSA_GROUPS), ones,
        jnp.tile(gain[3], NSA_GROUPS), ones]).reshape(1, EVEN_MAIN)
    post_scale = jnp.concatenate([
        jnp.ones((OFF_Q,), F32), jnp.full((NSA_Q,), NSA_HEAD_DIM ** -0.5, F32),
        jnp.ones((6 * NSA_KV,), F32)]).reshape(1, EVEN_MAIN)
    flags = np.zeros((EVEN_MAIN // tn,), np.int32)
    for lo_, hi_ in ((OFF_Q, OFF_KV), (OFF_KV + 2 * NSA_KV, OFF_KV + 3 * NSA_KV), (OFF_WIN, OFF_WIN + NSA_KV)):
        assert lo_ % tn == 0 and hi_ % tn == 0
        flags[lo_ // tn:hi_ // tn] = 1
    wo = w_out_even[e].astype(BF16)
    return dict(g_mix=mix_norm[l], w_main=w_main, w_tail=w_tail, flags=jnp.asarray(flags), post_gain=post_gain,
                post_scale=post_scale, conv_w=ssm_conv_w[e], conv_b=ssm_conv_b[e], dt_bias=ssm_dt_bias[e],
                a_log=ssm_a_log[e], d_skip=ssm_d[e], ssm_g=ssm_norm[e], cmp_alpha=nsa_cmp_alpha[e],
                cmp_w=nsa_cmp_w[e], g_kcmp=gain[1], rel_table=rel_bias, nsa_tables=nsa_tables,
                w_out_a=wo[:SSM_D_INNER], w_out_b=wo[SSM_D_INNER:])


def _odd_weights(l, o, mix_norm, w_in_odd, w_out_odd, pool_w, pool_scale, tn):
    n = w_in_odd.shape[2]
    wo = w_out_odd[o].astype(BF16)
    return dict(g_mix=mix_norm[l], w_in=w_in_odd[o].astype(BF16), flags=jnp.zeros((n // tn,), jnp.int32),
                post_gain=jnp.ones((1, n), F32), post_scale=jnp.ones((1, n), F32),
                pool_w=pool_w[o], pool_scale=pool_scale[o], w_out_a=wo[:POOL_DIM], w_out_b=wo[POOL_DIM:])


def kernel(x_prompt, x_sample, cache_nsa_kv, cache_sb_kv, state_nsa_win, state_ssm, state_conv, state_pool, page_table, ffn_norm, ffn_w_gate, ffn_w_up, ffn_w_down, mix_norm, w_in_even, w_out_even, ssm_conv_w, ssm_conv_b, ssm_dt_bias, ssm_a_log, ssm_d, ssm_norm, nsa_cmp_alpha, nsa_cmp_w, nsa_qk_gain, rel_bias, w_in_odd, w_out_odd, pool_w, pool_scale):
    bp, lp, _ = x_prompt.shape
    bs, ls, _ = x_sample.shape
    wb = state_nsa_win.shape[2]
    n_phys = cache_nsa_kv.shape[1]
    tn = 512
    xp = x_prompt.reshape(bp * lp, D_MODEL)
    xs = x_sample.reshape(bs * ls, D_MODEL)
    wg_all = ffn_w_gate.astype(BF16)
    wu_all = ffn_w_up.astype(BF16)
    wd_all = ffn_w_down.astype(BF16)
    nsa_tables = nsa_bias_tables(rel_bias, lp)
    nsa_pages = cache_nsa_kv.reshape(-1, PAGE_SIZE, NSA_KV_PARTS * NSA_KV)
    sb_pages = cache_sb_kv.reshape(-1, PAGE_SIZE, 2 * SB_DIM)
    kv_p, kv_s, sb_p, sb_s, win_p, win_s = [], [], [], [], [], []
    ssm_p, ssm_s, conv_p, conv_s, pool_p, pool_s = [], [], [], [], [], []
    for l in range(DEPTH):
        fa = (ffn_norm[l, 0], wg_all[l, 0], wu_all[l, 0], wd_all[l, 0])
        xp = ffn_half(xp, *fa)
        xs = ffn_half(xs, *fa)
        if l % 2 == 0:
            e = l // 2
            pw = _even_weights(l, e, mix_norm, w_in_even, w_out_even, ssm_conv_w, ssm_conv_b, ssm_dt_bias,
                               ssm_a_log, ssm_d, ssm_norm, nsa_cmp_alpha, nsa_cmp_w, nsa_qk_gain, rel_bias, nsa_tables, tn)
            xp, a_kv, a_win, a_conv, a_h = even_prompt(xp, bp, lp, wb, pw)
            xs, b_kv, b_win, b_conv, b_h = even_decode(xs, bs, ls, nsa_pages, page_table + e * n_phys,
                                                       state_nsa_win[e], state_conv[e], state_ssm[e], pw)
            kv_p.append(a_kv); kv_s.append(b_kv)
            win_p.append(a_win); win_s.append(b_win)
            conv_p.append(a_conv); conv_s.append(b_conv)
            ssm_p.append(a_h); ssm_s.append(b_h)
        else:
            o = l // 2
            pw = _odd_weights(l, o, mix_norm, w_in_odd, w_out_odd, pool_w, pool_scale, tn)
            xp, a_kv, a_pool = odd_prompt(xp, bp, lp, pw)
            xs, b_kv, b_pool = odd_decode(xs, bs, ls, sb_pages, page_table + o * n_phys, state_pool[o], pw)
            sb_p.append(a_kv); sb_s.append(b_kv)
            pool_p.append(a_pool); pool_s.append(b_pool)
        fb = (ffn_norm[l, 1], wg_all[l, 1], wu_all[l, 1], wd_all[l, 1])
        xp = ffn_half(xp, *fb)
        xs = ffn_half(xs, *fb)
    return (xp.reshape(bp, lp, D_MODEL), xs.reshape(bs, ls, D_MODEL), jnp.stack(kv_p), jnp.stack(kv_s),
            jnp.stack(sb_p), jnp.stack(sb_s), jnp.stack(win_p), jnp.stack(win_s), jnp.stack(ssm_p), jnp.stack(ssm_s),
            jnp.stack(conv_p), jnp.stack(conv_s), jnp.stack(pool_p), jnp.stack(pool_s))
```

```python
import functools
import math

import jax
import jax.numpy as jnp
import numpy as np
from jax import lax
from jax.experimental import pallas as pl
from jax.experimental.pallas import tpu as pltpu

F32 = jnp.float32
BF16 = jnp.bfloat16

D_MODEL = 2048
DEPTH = 4
PAGE_SIZE = 128
EPS = 1e-6
NEG_INF = -1e30
Q_BLOCK = 128
FFN_RESIDUAL = 0.5
D_FF = 5632

SSM_HEADS = 32
SSM_HEAD_DIM = 64
SSM_D_INNER = SSM_HEADS * SSM_HEAD_DIM
SSM_GROUPS = 4
SSM_STATE = 128
SSM_CONV = 4
SSM_CONV_DIM = SSM_D_INNER + 2 * SSM_GROUPS * SSM_STATE
SSM_CHUNK = 128

NSA_HEADS = 16
NSA_GROUPS = 4
NSA_HPG = NSA_HEADS // NSA_GROUPS
NSA_HEAD_DIM = 128
NSA_Q = NSA_HEADS * NSA_HEAD_DIM
NSA_KV = NSA_GROUPS * NSA_HEAD_DIM
NSA_KV_PARTS = 4
CMP_BLOCK = 32
CMP_STRIDE = 16
SLC_BLOCK = 64
SLC_TOPN = 16
WINDOW = 512
FORCE_SCORE = 1e4
REL_BUCKETS = 32
REL_MAX_DIST = 128
CMP_BAND = 2 * (Q_BLOCK // CMP_STRIDE)

POOL_WINDOWS = (2, 4, 8, 16)
POOL_GROUPS = len(POOL_WINDOWS)
POOL_GROUP_DIM = 256
POOL_DIM = POOL_GROUPS * POOL_GROUP_DIM
POOL_BUF = max(POOL_WINDOWS) - 1

SB_HEADS = 16
SB_HEAD_DIM = 128
SB_DIM = SB_HEADS * SB_HEAD_DIM

LANES = 128
VMEM_LIMIT_BYTES = 56 * 1024 * 1024

EVEN_MAIN = SSM_D_INNER + SSM_CONV_DIM + NSA_Q + 6 * NSA_KV
EVEN_TAIL = LANES
OFF_Z = 0
OFF_XBC = SSM_D_INNER
OFF_Q = OFF_XBC + SSM_CONV_DIM
OFF_KV = OFF_Q + NSA_Q
OFF_WIN = OFF_KV + 4 * NSA_KV


def _row_tile(m, want):
    t = min(m, want)
    assert m % t == 0
    return t


def _ffn_body(x_ref, g_ref, wg_ref, wu_ref, wd_ref, o_ref, h_ref, acc_ref):
    j = pl.program_id(1)

    @pl.when(j == 0)
    def _():
        x = x_ref[...]
        y = x * lax.rsqrt(jnp.mean(x * x, axis=-1, keepdims=True) + EPS)
        h_ref[...] = (y * g_ref[...]).astype(BF16)
        acc_ref[...] = jnp.zeros_like(acc_ref)

    h = h_ref[...]
    a = jnp.dot(h, wg_ref[...], preferred_element_type=F32)
    b = jnp.dot(h, wu_ref[...], preferred_element_type=F32)
    t = (a * jax.nn.sigmoid(a)) * b
    acc_ref[...] += jnp.dot(t.astype(BF16), wd_ref[...], preferred_element_type=F32)

    @pl.when(j == pl.num_programs(1) - 1)
    def _():
        o_ref[...] = x_ref[...] + FFN_RESIDUAL * acc_ref[...]


def ffn_half(x, g, wg, wu, wd, *, tm=512, tf=512):
    m, d = x.shape
    f = wg.shape[1]
    tm = _row_tile(m, tm)
    assert f % tf == 0
    return pl.pallas_call(
        _ffn_body,
        grid=(m // tm, f // tf),
        in_specs=[
            pl.BlockSpec((tm, d), lambda i, j: (i, 0)),
            pl.BlockSpec((1, d), lambda i, j: (0, 0)),
            pl.BlockSpec((d, tf), lambda i, j: (0, j)),
            pl.BlockSpec((d, tf), lambda i, j: (0, j)),
            pl.BlockSpec((tf, d), lambda i, j: (j, 0)),
        ],
        out_specs=pl.BlockSpec((tm, d), lambda i, j: (i, 0)),
        out_shape=jax.ShapeDtypeStruct((m, d), F32),
        scratch_shapes=[pltpu.VMEM((tm, d), BF16), pltpu.VMEM((tm, d), F32)],
        compiler_params=pltpu.CompilerParams(
            dimension_semantics=("parallel", "arbitrary"), vmem_limit_bytes=VMEM_LIMIT_BYTES),
        name="ffn_half",
    )(x, g.reshape(1, d), wg, wu, wd)


def _proj_body(flag_ref, x_ref, g_ref, w_ref, pg_ref, ps_ref, *rest, has_tail):
    if has_tail:
        wt_ref, o_ref, ot_ref, h_ref = rest
    else:
        o_ref, h_ref = rest
    j = pl.program_id(1)

    @pl.when(j == 0)
    def _():
        x = x_ref[...]
        y = x * lax.rsqrt(jnp.mean(x * x, axis=-1, keepdims=True) + EPS)
        h = (y * g_ref[...]).astype(BF16)
        h_ref[...] = h
        if has_tail:
            ot_ref[...] = jnp.dot(h, wt_ref[...], preferred_element_type=F32)

    r = jnp.dot(h_ref[...], w_ref[...], preferred_element_type=F32)

    @pl.when(flag_ref[j] == 0)
    def _():
        o_ref[...] = r

    @pl.when(flag_ref[j] != 0)
    def _():
        tm, tn = r.shape
        for c in range(tn // LANES):
            rc = r[:, c * LANES:(c + 1) * LANES]
            yc = rc * lax.rsqrt(jnp.mean(rc * rc, axis=-1, keepdims=True) + EPS)
            yc = (yc * pg_ref[:, c * LANES:(c + 1) * LANES]) * ps_ref[:, c * LANES:(c + 1) * LANES]
            o_ref[:, c * LANES:(c + 1) * LANES] = yc


def in_proj(x, g, w, flags, post_gain, post_scale, w_tail=None, *, tm=512, tn=512):
    m, d = x.shape
    n = w.shape[1]
    tm = _row_tile(m, tm)
    assert n % tn == 0
    has_tail = w_tail is not None
    in_specs = [
        pl.BlockSpec((tm, d), lambda i, j, fl: (i, 0)),
        pl.BlockSpec((1, d), lambda i, j, fl: (0, 0)),
        pl.BlockSpec((d, tn), lambda i, j, fl: (0, j)),
        pl.BlockSpec((1, tn), lambda i, j, fl: (0, j)),
        pl.BlockSpec((1, tn), lambda i, j, fl: (0, j)),
    ]
    out_specs = [pl.BlockSpec((tm, tn), lambda i, j, fl: (i, j))]
    out_shape = [jax.ShapeDtypeStruct((m, n), F32)]
    args = [x, g.reshape(1, d), w, post_gain, post_scale]
    if has_tail:
        nt = w_tail.shape[1]
        in_specs.append(pl.BlockSpec((d, nt), lambda i, j, fl: (0, 0)))
        out_specs.append(pl.BlockSpec((tm, nt), lambda i, j, fl: (i, 0)))
        out_shape.append(jax.ShapeDtypeStruct((m, nt), F32))
        args.append(w_tail)
    res = pl.pallas_call(
        functools.partial(_proj_body, has_tail=has_tail),
        grid_spec=pltpu.PrefetchScalarGridSpec(
            num_scalar_prefetch=1,
            grid=(m // tm, n // tn),
            in_specs=in_specs,
            out_specs=out_specs,
            scratch_shapes=[pltpu.VMEM((tm, d), BF16)],
        ),
        out_shape=out_shape,
        compiler_params=pltpu.CompilerParams(
            dimension_semantics=("parallel", "arbitrary"), vmem_limit_bytes=VMEM_LIMIT_BYTES),
        name="in_proj",
    )(flags, *args)
    return res if has_tail else res[0]


def _out_proj_body(x_ref, a1_ref, a2_ref, w1_ref, w2_ref, o_ref):
    r = jnp.dot(a1_ref[...].astype(BF16), w1_ref[...], preferred_element_type=F32)
    r = r + jnp.dot(a2_ref[...].astype(BF16), w2_ref[...], preferred_element_type=F32)
    o_ref[...] = x_ref[...] + r


def out_proj(x, a1, a2, w1, w2, *, tm=512, tn=512):
    m, d = x.shape
    k1, k2 = a1.shape[1], a2.shape[1]
    tm = _row_tile(m, tm)
    return pl.pallas_call(
        _out_proj_body,
        grid=(m // tm, d // tn),
        in_specs=[
            pl.BlockSpec((tm, tn), lambda i, j: (i, j)),
            pl.BlockSpec((tm, k1), lambda i, j: (i, 0)),
            pl.BlockSpec((tm, k2), lambda i, j: (i, 0)),
            pl.BlockSpec((k1, tn), lambda i, j: (0, j)),
            pl.BlockSpec((k2, tn), lambda i, j: (0, j)),
        ],
        out_specs=pl.BlockSpec((tm, tn), lambda i, j: (i, j)),
        out_shape=jax.ShapeDtypeStruct((m, d), F32),
        compiler_params=pltpu.CompilerParams(
            dimension_semantics=("parallel", "arbitrary"), vmem_limit_bytes=VMEM_LIMIT_BYTES),
        name="out_proj",
    )(x, a1, a2, w1, w2)


def _dot_nt(a, b):
    return lax.dot_general(a, b, (((1,), (1,)), ((), ())), preferred_element_type=F32)


def _dot(a, b):
    return jnp.dot(a, b, preferred_element_type=F32)


def _softplus(z):
    return jnp.maximum(z, 0.0) + jnp.log(1.0 + jnp.exp(-jnp.abs(z)))


SB_KEY_CHUNK = 4 * Q_BLOCK
SB_HEADS_PER_STEP = 2


def _sb_prompt_body(q_ref, k_ref, v_ref, tri_ref, o_ref):
    i = pl.program_id(2)
    blk = Q_BLOCK
    ck = SB_KEY_CHUNK
    nsub = ck // blk
    dh = SB_HEAD_DIM
    heads = range(SB_HEADS_PER_STEP)
    qs = [(q_ref[:, h * dh:(h + 1) * dh] * (dh ** -0.5)).astype(BF16) for h in heads]
    tri = tri_ref[...]

    def chunk_head(h, c, run, acc, diag):
        r0 = pl.multiple_of(c * ck, ck)
        z = _dot_nt(qs[h], k_ref[pl.ds(r0, ck), h * dh:(h + 1) * dh].astype(BF16))
        sp = _softplus(z)
        if diag:
            t_pos = i * blk + lax.broadcasted_iota(jnp.int32, (blk, ck), 0)
            s_pos = c * ck + lax.broadcasted_iota(jnp.int32, (blk, ck), 1)
            strict = s_pos < t_pos
            go = jnp.where(strict, sp, 0.0)
        else:
            go = sp
        hi = go.astype(BF16)
        lo = (go - hi.astype(F32)).astype(BF16)
        suffix = [None] * nsub
        for j in reversed(range(nsub)):
            sl = slice(j * blk, (j + 1) * blk)
            ct = _dot(hi[:, sl], tri) + _dot(lo[:, sl], tri)
            suffix[j] = (ct[:, :blk] - go[:, sl]) + run
            run = run + ct[:, blk:]
        att = jnp.exp((z - sp) - jnp.concatenate(suffix, axis=1))
        if diag:
            att = jnp.where(strict, att, 0.0)
        acc = acc + _dot(att.astype(BF16), v_ref[pl.ds(r0, ck), h * dh:(h + 1) * dh].astype(BF16))
        return run, acc

    def chunk(c, carry, diag):
        return tuple(chunk_head(h, c, carry[h][0], carry[h][1], diag) for h in heads)

    cd = i // nsub
    init = tuple((jnp.zeros((blk, blk), F32), jnp.zeros((blk, dh), F32)) for _ in heads)
    carry = chunk(cd, init, True)
    carry = lax.fori_loop(0, cd, lambda r, cr: chunk(cd - 1 - r, cr, False), carry)
    for h in heads:
        o_ref[:, h * dh:(h + 1) * dh] = carry[h][1]


def sb_prompt(h, bsz, seq):
    qb = seq // Q_BLOCK
    hw = SB_HEADS_PER_STEP * SB_HEAD_DIM
    q_off = POOL_DIM // hw
    k_off = q_off + SB_DIM // hw
    v_off = k_off + SB_DIM // hw
    jj = np.arange(Q_BLOCK)
    tri = np.concatenate([(jj[:, None] >= jj[None, :]).astype(np.float32),
                          np.ones((Q_BLOCK, Q_BLOCK), np.float32)], axis=1)
    return pl.pallas_call(
        _sb_prompt_body,
        grid=(bsz, SB_DIM // hw, qb),
        in_specs=[
            pl.BlockSpec((Q_BLOCK, hw), lambda b, hh, i: (b * qb + i, q_off + hh)),
            pl.BlockSpec((seq, hw), lambda b, hh, i: (b, k_off + hh)),
            pl.BlockSpec((seq, hw), lambda b, hh, i: (b, v_off + hh)),
            pl.BlockSpec((Q_BLOCK, 2 * Q_BLOCK), lambda b, hh, i: (0, 0)),
        ],
        out_specs=pl.BlockSpec((Q_BLOCK, hw), lambda b, hh, i: (b * qb + i, hh)),
        out_shape=jax.ShapeDtypeStruct((bsz * seq, SB_DIM), F32),
        compiler_params=pltpu.CompilerParams(
            dimension_semantics=("parallel", "parallel", "arbitrary"), vmem_limit_bytes=VMEM_LIMIT_BYTES),
        name="sb_prompt",
    )(h, h, h, jnp.asarray(tri, BF16))


CONV_HALO = 8


def _split3(x):
    x1 = x.astype(BF16)
    r = x - x1.astype(F32)
    x2 = r.astype(BF16)
    x3 = (r - x2.astype(F32)).astype(BF16)
    return x1, x2, x3


def _ssd_body(z_ref, x_ref, bc_ref, xh_ref, bch_ref, cv_ref, tail_ref, h0_ref, cw_ref, cb_ref, dtb_r_ref, alog_r_ref,
              dtb_c_ref, alog_c_ref, dsk_ref, ng_ref, tril_ref, y_ref, ht_ref, st_ref, *, n_valid):
    c = pl.program_id(1)
    q = SSM_CHUNK
    p_dim = SSM_HEAD_DIM
    hpg = SSM_HEADS // SSM_GROUPS
    gn = SSM_GROUPS * SSM_STATE

    @pl.when(c == 0)
    def _():
        st_ref[...] = h0_ref[0]

    first = c == 0
    cv = cv_ref[0]
    xe = jnp.concatenate([jnp.where(first, cv[:, :SSM_D_INNER], xh_ref[...]), x_ref[...]], axis=0)
    bce = jnp.concatenate([jnp.where(first, cv[:, SSM_D_INNER:], bch_ref[...]), bc_ref[...]], axis=0)

    def conv(e, lo, hi):
        acc = cb_ref[:, lo:hi]
        for k in range(SSM_CONV):
            r0 = CONV_HALO - (SSM_CONV - 1) + k
            acc = acc + e[r0:r0 + q, :] * cw_ref[k:k + 1, lo:hi]
        return acc * jax.nn.sigmoid(acc)

    xs = conv(xe, 0, SSM_D_INNER)
    bcm = conv(bce, SSM_D_INNER, SSM_CONV_DIM)

    tail = tail_ref[...]
    t_row = lax.broadcasted_iota(jnp.int32, (q, LANES), 0)
    t_lane = lax.broadcasted_iota(jnp.int32, (LANES, q), 1)
    dt_r = jnp.where(t_row < n_valid, _softplus1p(tail + dtb_r_ref[...]), 0.0)
    dt_c = jnp.where(t_lane < n_valid, _softplus1p(tail.T + dtb_c_ref[...]), 0.0)
    da_r = dt_r * (-jnp.exp(alog_r_ref[...]))
    da_c = dt_c * (-jnp.exp(alog_c_ref[...]))
    tril = tril_ref[...]
    a1, a2, a3 = _split3(da_r)
    acum = (_dot(tril, a1) + _dot(tril, a2)) + _dot(tril, a3)
    c1, c2, c3 = _split3(da_c)
    acum_t = (_dot_nt(c1, tril) + _dot_nt(c2, tril)) + _dot_nt(c3, tril)
    last = acum[q - 1:q, :]
    causal = lax.broadcasted_iota(jnp.int32, (q, q), 1) <= lax.broadcasted_iota(jnp.int32, (q, q), 0)

    for g in range(SSM_GROUPS):
        bg = bcm[:, g * SSM_STATE:(g + 1) * SSM_STATE]
        cg = bcm[:, gn + g * SSM_STATE:gn + (g + 1) * SSM_STATE].astype(BF16)
        cbg = _dot_nt(cg, bg.astype(BF16))
        bg_t = bg.T.astype(BF16)
        for hh in range(hpg):
            h = g * hpg + hh
            col = acum[:, h:h + 1]
            decay = jnp.exp(jnp.where(causal, col - acum_t[h:h + 1, :], NEG_INF))
            xh = xs[:, h * p_dim:(h + 1) * p_dim]
            xdt = xh * dt_r[:, h:h + 1]
            y_diag = _dot((cbg * decay).astype(BF16), xdt.astype(BF16))
            s_t = st_ref[h]
            y_off = _dot(cg, s_t.astype(BF16)) * jnp.exp(col)
            end = last[:, h:h + 1]
            st_ref[h] = s_t * jnp.exp(end) + _dot(bg_t, (xdt * jnp.exp(end - col)).astype(BF16))
            y_ref[:, h * p_dim:(h + 1) * p_dim] = (y_diag + y_off) + dsk_ref[:, h * p_dim:(h + 1) * p_dim] * xh

    z = z_ref[...]
    y = y_ref[...] * (z * jax.nn.sigmoid(z))
    gw = SSM_D_INNER // SSM_GROUPS
    for g in range(SSM_GROUPS):
        yg = y[:, g * gw:(g + 1) * gw]
        yg = yg * lax.rsqrt(jnp.mean(yg * yg, axis=-1, keepdims=True) + EPS)
        y_ref[:, g * gw:(g + 1) * gw] = yg * ng_ref[:, g * gw:(g + 1) * gw]

    @pl.when(c == pl.num_programs(1) - 1)
    def _():
        ht_ref[0] = st_ref[...]


def _softplus1p(x):
    return jnp.maximum(x, 0.0) + jnp.log1p(jnp.exp(-jnp.abs(x)))


def ssd_mix(main, tail, conv_halo, h0_t, pw, bsz, seq, n_valid):
    q = SSM_CHUNK
    nc = seq // q
    per = q // CONV_HALO
    bcw = SSM_CONV_DIM - SSM_D_INNER
    pad_h = LANES - SSM_HEADS
    dtb = jnp.pad(pw['dt_bias'], (0, pad_h))
    alog = jnp.pad(pw['a_log'], (0, pad_h))
    tt = np.arange(q)
    tril = (tt[None, :] <= tt[:, None]).astype(np.float32)
    halo = lambda b, c: (jnp.maximum((b * nc + c) * per - 1, 0), 0)
    st_shape = (SSM_HEADS, SSM_STATE, SSM_HEAD_DIM)
    vec = lambda n: pl.BlockSpec((1, n), lambda b, c: (0, 0))
    colv = pl.BlockSpec((LANES, 1), lambda b, c: (0, 0))
    y, ht = pl.pallas_call(
        functools.partial(_ssd_body, n_valid=n_valid),
        grid=(bsz, nc),
        in_specs=[
            pl.BlockSpec((q, SSM_D_INNER), lambda b, c: (b * nc + c, 0)),
            pl.BlockSpec((q, SSM_D_INNER), lambda b, c: (b * nc + c, 1)),
            pl.BlockSpec((q, bcw), lambda b, c: (b * nc + c, 2 * SSM_D_INNER // bcw)),
            pl.BlockSpec((CONV_HALO, SSM_D_INNER), lambda b, c: (halo(b, c)[0], 1)),
            pl.BlockSpec((CONV_HALO, bcw), lambda b, c: (halo(b, c)[0], 2 * SSM_D_INNER // bcw)),
            pl.BlockSpec((1, CONV_HALO, SSM_CONV_DIM), lambda b, c: (b, 0, 0)),
            pl.BlockSpec((q, LANES), lambda b, c: (b * nc + c, 0)),
            pl.BlockSpec((1,) + st_shape, lambda b, c: (b, 0, 0, 0)),
            pl.BlockSpec((SSM_CONV, SSM_CONV_DIM), lambda b, c: (0, 0)),
            vec(SSM_CONV_DIM), vec(LANES), vec(LANES), colv, colv, vec(SSM_D_INNER), vec(SSM_D_INNER),
            pl.BlockSpec((q, q), lambda b, c: (0, 0)),
        ],
        out_specs=[pl.BlockSpec((q, SSM_D_INNER), lambda b, c: (b * nc + c, 0)),
                   pl.BlockSpec((1,) + st_shape, lambda b, c: (b, 0, 0, 0))],
        out_shape=[jax.ShapeDtypeStruct((bsz * seq, SSM_D_INNER), F32),
                   jax.ShapeDtypeStruct((bsz,) + st_shape, F32)],
        scratch_shapes=[pltpu.VMEM(st_shape, F32)],
        compiler_params=pltpu.CompilerParams(
            dimension_semantics=("parallel", "arbitrary"), vmem_limit_bytes=VMEM_LIMIT_BYTES),
        name="ssd_mix",
    )(main, main, main, main, main, conv_halo, tail, h0_t, pw['conv_w'], pw['conv_b'].reshape(1, -1),
      dtb.reshape(1, LANES), alog.reshape(1, LANES), dtb.reshape(LANES, 1), alog.reshape(LANES, 1),
      jnp.repeat(pw['d_skip'], SSM_HEAD_DIM).reshape(1, -1), pw['ssm_g'].reshape(1, -1), jnp.asarray(tril, BF16))
    return y, ht


POOL_HALO = 16


def _pool_body(u_ref, halo_ref, buf_ref, w_ref, sc_ref, o_ref, *, pos0):
    i = pl.program_id(1)
    tl = u_ref.shape[0]
    u = u_ref[...]
    prev = jnp.where(i == 0, buf_ref[0], halo_ref[...])
    x = jnp.concatenate([prev, u], axis=0)
    pos = pos0 + i * tl + lax.broadcasted_iota(jnp.int32, (tl, 1), 0)
    for g, w in enumerate(POOL_WINDOWS):
        cols = slice(g * POOL_GROUP_DIM, (g + 1) * POOL_GROUP_DIM)
        s = x[:, cols]
        span = 1
        while span < w:
            s = s[span:, :] + s[:-span, :]
            span *= 2
        win = s[POOL_HALO - (w - 1):, :]
        cnt = jnp.minimum(pos + 1, w).astype(F32)
        diff = win / cnt - u[:, cols]
        y = _dot(diff.astype(BF16), w_ref[g])
        o_ref[:, cols] = y * sc_ref[:, cols]


def pool_mix_pallas(h, buf16, pool_w, pool_scale, bsz, seq, pos0, *, tl=512):
    tl = min(tl, seq)
    nt = seq // tl
    per = max(tl // POOL_HALO, 1)
    if tl < POOL_HALO:
        assert nt == 1
        halo_src = buf16.reshape(bsz * POOL_HALO, POOL_DIM)
        halo_map = lambda b, i: (b, 0)
    else:
        halo_src = h
        halo_map = lambda b, i: (jnp.maximum((b * nt + i) * per - 1, 0), 0)
    return pl.pallas_call(
        functools.partial(_pool_body, pos0=pos0),
        grid=(bsz, nt),
        in_specs=[
            pl.BlockSpec((tl, POOL_DIM), lambda b, i: (b * nt + i, 0)),
            pl.BlockSpec((POOL_HALO, POOL_DIM), halo_map),
            pl.BlockSpec((1, POOL_HALO, POOL_DIM), lambda b, i: (b, 0, 0)),
            pl.BlockSpec((POOL_GROUPS, POOL_GROUP_DIM, POOL_GROUP_DIM), lambda b, i: (0, 0, 0)),
            pl.BlockSpec((1, POOL_DIM), lambda b, i: (0, 0)),
        ],
        out_specs=pl.BlockSpec((tl, POOL_DIM), lambda b, i: (b * nt + i, 0)),
        out_shape=jax.ShapeDtypeStruct((bsz * seq, POOL_DIM), F32),
        compiler_params=pltpu.CompilerParams(
            dimension_semantics=("parallel", "arbitrary"), vmem_limit_bytes=VMEM_LIMIT_BYTES),
        name="pool_mix",
    )(h, halo_src, buf16, pool_w.astype(BF16), pool_scale.reshape(1, POOL_DIM))


def _sb_decode_body(pt_ref, qbd_ref, new_ref, page_ref, tri_ref, o_ref, run_ref, acc_ref, *, n_tok):
    p = pl.program_id(1)
    rows = SB_HEADS * n_tok
    tri = tri_ref[...]
    qbd = qbd_ref[0]

    def slabs(ref, part):
        return jnp.concatenate([ref[0, :, part * SB_HEADS + h, :] for h in range(SB_HEADS)], axis=1).astype(BF16)

    def page_update(k_all, v_all, masked):
        z = _dot_nt(qbd, k_all)
        sp = _softplus(z)
        if masked:
            tok = lax.broadcasted_iota(jnp.int32, (rows, PAGE_SIZE), 0) % n_tok
            key = lax.broadcasted_iota(jnp.int32, (rows, PAGE_SIZE), 1)
            strict = key < tok
            go = jnp.where(strict, sp, 0.0)
        else:
            go = sp
        hi = go.astype(BF16)
        lo = (go - hi.astype(F32)).astype(BF16)
        ct = _dot(hi, tri) + _dot(lo, tri)
        run = run_ref[...]
        att = jnp.exp((z - sp) - ((ct[:, :PAGE_SIZE] - go) + run))
        if masked:
            att = jnp.where(strict, att, 0.0)
        run_ref[...] = run + ct[:, PAGE_SIZE:]
        acc_ref[...] += _dot(att.astype(BF16), v_all)

    @pl.when(p == 0)
    def _():
        run_ref[...] = jnp.zeros_like(run_ref)
        acc_ref[...] = jnp.zeros_like(acc_ref)
        page_update(new_ref[0, :, :SB_DIM].astype(BF16), new_ref[0, :, SB_DIM:].astype(BF16), True)

    page_update(slabs(page_ref, 0), slabs(page_ref, 1), False)

    @pl.when(p == pl.num_programs(1) - 1)
    def _():
        for h in range(SB_HEADS):
            o_ref[0, :, h * SB_HEAD_DIM:(h + 1) * SB_HEAD_DIM] = (
                acc_ref[h * n_tok:(h + 1) * n_tok, h * SB_HEAD_DIM:(h + 1) * SB_HEAD_DIM])


def sb_decode(h, cache, page_table, bsz, n_tok):
    n_pages = page_table.shape[1]
    rows = SB_HEADS * n_tok
    assert rows == PAGE_SIZE and n_tok <= PAGE_SIZE
    h3 = h.reshape(bsz, n_tok, -1)
    q = h3[..., POOL_DIM:POOL_DIM + SB_DIM].reshape(bsz, n_tok, SB_HEADS, SB_HEAD_DIM) * (SB_HEAD_DIM ** -0.5)
    eye = jnp.eye(SB_HEADS, dtype=F32)
    qbd = jnp.einsum('bthd,hg->bhtgd', q, eye).reshape(bsz, rows, SB_DIM).astype(BF16)
    new_kv = jnp.pad(h3[..., POOL_DIM + SB_DIM:], ((0, 0), (0, PAGE_SIZE - n_tok), (0, 0)))
    jj = np.arange(PAGE_SIZE)
    tri = np.concatenate([(jj[:, None] >= jj[None, :]).astype(np.float32),
                          np.ones((PAGE_SIZE, PAGE_SIZE), np.float32)], axis=1)
    return pl.pallas_call(
        functools.partial(_sb_decode_body, n_tok=n_tok),
        grid_spec=pltpu.PrefetchScalarGridSpec(
            num_scalar_prefetch=1,
            grid=(bsz, n_pages),
            in_specs=[
                pl.BlockSpec((1, rows, SB_DIM), lambda b, p, pt: (b, 0, 0)),
                pl.BlockSpec((1, PAGE_SIZE, 2 * SB_DIM), lambda b, p, pt: (b, 0, 0)),
                pl.BlockSpec((1, PAGE_SIZE, 2 * SB_HEADS, SB_HEAD_DIM),
                             lambda b, p, pt: (pt[b, n_pages - 1 - p], 0, 0, 0)),
                pl.BlockSpec((PAGE_SIZE, 2 * PAGE_SIZE), lambda b, p, pt: (0, 0)),
            ],
            out_specs=pl.BlockSpec((1, n_tok, SB_DIM), lambda b, p, pt: (b, 0, 0)),
            scratch_shapes=[pltpu.VMEM((rows, PAGE_SIZE), F32), pltpu.VMEM((rows, SB_DIM), F32)],
        ),
        out_shape=jax.ShapeDtypeStruct((bsz, n_tok, SB_DIM), F32),
        compiler_params=pltpu.CompilerParams(
            dimension_semantics=("parallel", "arbitrary"), vmem_limit_bytes=VMEM_LIMIT_BYTES),
        name="sb_decode",
    )(page_table, qbd, new_kv, cache, jnp.asarray(tri, BF16))


def _rel_bucket_np(dist):
    n = np.maximum(dist, 0)
    exact = REL_BUCKETS // 2
    nf = np.maximum(n, 1).astype(np.float32)
    log_b = exact + (np.log(nf / np.float32(exact)) / np.float32(math.log(REL_MAX_DIST / exact))
                     * np.float32(REL_BUCKETS - exact)).astype(np.int32)
    return np.where(n < exact, n, np.minimum(log_b, REL_BUCKETS - 1)).astype(np.int32)


def _bias_select(rows, bucket):
    ids = jnp.asarray(bucket)
    out = jnp.broadcast_to(rows[REL_BUCKETS - 1], bucket.shape)
    for k in range(REL_BUCKETS - 2, -1, -1):
        out = jnp.where(ids == k, rows[k], out)
    return out


def nsa_bias_tables(rel_bias, seq):
    s = np.arange(Q_BLOCK)
    rows = jnp.repeat(rel_bias.reshape(REL_BUCKETS, NSA_GROUPS, NSA_HPG).transpose(1, 0, 2), Q_BLOCK, axis=-1)
    far_from = (CMP_BAND // 2 + 1) * CMP_STRIDE - (CMP_BLOCK - 1)
    assert (_rel_bucket_np(np.arange(far_from, 2 * seq)) == REL_BUCKETS - 1).all()
    near_idx = _rel_bucket_np(np.arange(2)[:, None, None] * Q_BLOCK + s[None, None, :] - s[None, :, None])
    near_idx = np.tile(near_idx, (1, 1, NSA_HPG))
    near = jnp.stack([_bias_select(rows[g], near_idx) for g in range(NSA_GROUPS)])
    far = rows[:, REL_BUCKETS - 1:, :]
    m_rel = np.arange(-CMP_BAND // 2, CMP_BAND // 2).reshape(CMP_BAND, 1)
    dist_c = s.reshape(1, Q_BLOCK) - (m_rel * CMP_STRIDE + CMP_BLOCK - 1)
    cb_idx = np.tile(_rel_bucket_np(dist_c), (1, NSA_HPG))
    cb_ok = jnp.asarray(np.tile(dist_c >= 0, (1, NSA_HPG)))
    cb = jnp.stack([jnp.where(cb_ok, _bias_select(rows[g], cb_idx), NEG_INF) for g in range(NSA_GROUPS)])
    return near, far, cb


def _nsa_prompt_body(q_ref, tail_ref, ks_ref, vs_ref, kw_ref, vw_ref, kcmp_ref, vcmpt_ref, cb_ref, nb_ref, fb_ref,
                     at_ref, o_ref, vst_ref, vwt_ref, sel_ref, gt_ref, cbs_ref, *, seq):
    g = pl.program_id(1)
    i = pl.program_id(2)
    blk = Q_BLOCK
    hw = NSA_HPG * blk
    n_slc = seq // SLC_BLOCK

    @pl.when(i == 0)
    def _():
        for c in range(seq // blk):
            vst_ref[:, c * blk:(c + 1) * blk] = vs_ref[c * blk:(c + 1) * blk, :].T.astype(BF16)
            vwt_ref[:, c * blk:(c + 1) * blk] = vw_ref[c * blk:(c + 1) * blk, :].T.astype(BF16)

    q4 = jnp.concatenate([q_ref[:, h * blk:(h + 1) * blk] for h in range(NSA_HPG)], axis=0).astype(BF16)

    n_pad = seq // CMP_STRIDE
    half = CMP_BAND // 2
    rows = lax.broadcasted_iota(jnp.int32, (n_pad + half, hw), 0)
    cbs_ref[...] = jnp.where(rows < half * i, jnp.broadcast_to(fb_ref[0], (n_pad + half, hw)), NEG_INF)
    cbs_ref[pl.ds(pl.multiple_of(half * i, half), CMP_BAND), :] = cb_ref[0]
    cb = cbs_ref[half:, :]
    sc = _dot_nt(kcmp_ref[0, 0].astype(BF16), q4) + cb
    mc = jnp.max(sc, axis=0, keepdims=True)
    pc = jnp.exp(sc - mc)
    pc = pc / jnp.sum(pc, axis=0, keepdims=True)
    pc = pc * jnp.where(cb > 0.5 * NEG_INF, 1.0, 0.0)
    o_c = _dot(vcmpt_ref[0, 0].astype(BF16), pc.astype(BF16))
    psum = pc[:, 0:blk]
    for h in range(1, NSA_HPG):
        psum = psum + pc[:, h * blk:(h + 1) * blk]
    p1 = psum.astype(BF16)
    r1 = psum - p1.astype(F32)
    p2 = r1.astype(BF16)
    p3 = (r1 - p2.astype(F32)).astype(BF16)
    at = at_ref[...]
    imp = (_dot(at, p1) + _dot(at, p2)) + _dot(at, p3)

    jdx = lax.broadcasted_iota(jnp.int32, (n_slc, blk), 0)
    tpos = i * blk + lax.broadcasted_iota(jnp.int32, (n_slc, blk), 1)
    cur = tpos // SLC_BLOCK
    forced = jnp.where(jdx == 0, 1.0, 0.0) + jnp.where(jdx == cur, 1.0, 0.0) + jnp.where(jdx == cur - 1, 1.0, 0.0)
    score = jnp.where(forced > 0.0, FORCE_SCORE, jnp.where(jdx <= cur, imp, -1.0))
    rank = jnp.zeros((n_slc, blk), F32)
    for r in range(n_slc):
        row = jnp.broadcast_to(score[r:r + 1, :], (n_slc, blk))
        gt = jnp.where(row > score, 1.0, 0.0)
        ge = jnp.where(row >= score, 1.0, 0.0)
        rank = rank + jnp.where(jdx > r, ge, gt)
    sel_ref[...] = jnp.where(rank < float(min(SLC_TOPN, n_slc)), 1.0, 0.0)

    ss = lax.broadcasted_iota(jnp.int32, (blk, blk), 0)
    tt = lax.broadcasted_iota(jnp.int32, (blk, blk), 1)
    causal = jnp.where(ss <= tt, 1.0, 0.0)
    anti = jnp.where(ss >= tt, 1.0, 0.0)
    bias_far = fb_ref[0]

    def tile4(mk):
        return jnp.concatenate([mk] * NSA_HPG, axis=1)

    def scores(k_ref, kb, bias):
        r0 = pl.multiple_of(kb * blk, blk)
        return _dot_nt(k_ref[pl.ds(r0, blk), :].astype(BF16), q4) + bias

    def pv(vt_ref, kb, p):
        r0 = pl.multiple_of(kb * blk, blk)
        return _dot(vt_ref[:, pl.ds(r0, blk)], p.astype(BF16))

    def first(s, mk):
        s = jnp.where(tile4(mk) > 0.0, s, NEG_INF)
        m = jnp.max(s, axis=0, keepdims=True)
        p = jnp.exp(s - m)
        return m, jnp.sum(p, axis=0, keepdims=True), p

    def update(carry, s, mk, vt_ref, kb):
        m, l, acc = carry
        s = jnp.where(tile4(mk) > 0.0, s, NEG_INF)
        m_new = jnp.maximum(m, jnp.max(s, axis=0, keepdims=True))
        alpha = jnp.exp(m - m_new)
        p = jnp.exp(s - m_new)
        return m_new, alpha * l + jnp.sum(p, axis=0, keepdims=True), alpha * acc + pv(vt_ref, kb, p)

    def sel_mask(kb):
        r0 = sel_ref[pl.ds(2 * kb, 1), :]
        r1 = sel_ref[pl.ds(2 * kb + 1, 1), :]
        half = blk // 2
        return jnp.concatenate([jnp.broadcast_to(r0, (half, blk)), jnp.broadcast_to(r1, (half, blk))], axis=0)

    m, l, p = first(scores(ks_ref, i, nb_ref[0, 0]), sel_mask(i) * causal)
    carry = (m, l, pv(vst_ref, i, p))
    kb1 = jnp.maximum(i - 1, 0)
    ok1 = jnp.where(i >= 1, 1.0, 0.0)
    carry = update(carry, scores(ks_ref, kb1, nb_ref[0, 1]), sel_mask(kb1) * ok1, vst_ref, kb1)

    def far_body(r, c):
        m, l, acc = c
        r0 = pl.multiple_of(r * 2 * blk, 2 * blk)
        s = _dot_nt(ks_ref[pl.ds(r0, 2 * blk), :].astype(BF16), q4) + bias_far
        ok2 = jnp.where(2 * r + 1 <= i - 2, 1.0, 0.0)
        mk = jnp.concatenate([sel_mask(2 * r), sel_mask(2 * r + 1) * ok2], axis=0)
        s = jnp.where(tile4(mk) > 0.0, s, NEG_INF)
        m_new = jnp.maximum(m, jnp.max(s, axis=0, keepdims=True))
        alpha = jnp.exp(m - m_new)
        p = jnp.exp(s - m_new)
        pvv = _dot(vst_ref[:, pl.ds(r0, 2 * blk)], p.astype(BF16))
        return m_new, alpha * l + jnp.sum(p, axis=0, keepdims=True), alpha * acc + pvv

    m, l, acc = lax.fori_loop(0, i // 2, far_body, carry)
    o_s = acc / l

    m, l, p = first(scores(kw_ref, i, nb_ref[0, 0]), causal)
    carry = (m, l, pv(vwt_ref, i, p))
    ones = jnp.ones((blk, blk), F32)
    for d in range(1, WINDOW // blk + 1):
        kb = jnp.maximum(i - d, 0)
        ok = jnp.where(i >= d, 1.0, 0.0)
        bias = nb_ref[0, 1] if d == 1 else bias_far
        mk = (anti if d == WINDOW // blk else ones) * ok
        carry = update(carry, scores(kw_ref, kb, bias), mk, vwt_ref, kb)
    m, l, acc = carry
    o_w = acc / l

    gt_ref[...] = tail_ref[...].T

    def gate(c):
        rows = [gt_ref[pl.ds(SSM_HEADS + 3 * (NSA_HPG * g + h) + c, 1), :] for h in range(NSA_HPG)]
        return jax.nn.sigmoid(jnp.concatenate(rows, axis=1))

    o_t = (gate(0) * o_c + gate(1) * o_s) + gate(2) * o_w
    for h in range(NSA_HPG):
        o_ref[:, h * blk:(h + 1) * blk] = o_t[:, h * blk:(h + 1) * blk].T


def nsa_prompt(main, tail, kcmp, vcmpt, tables, bsz, seq):
    near, far, cb = tables
    qb = seq // Q_BLOCK
    n_pad = seq // CMP_STRIDE
    n_slc = seq // SLC_BLOCK
    hw = NSA_HPG * Q_BLOCK
    ratio = SLC_BLOCK // CMP_STRIDE
    lo = CMP_BLOCK // CMP_STRIDE - 1
    jj = np.arange(n_slc)[:, None]
    nn = np.arange(n_pad)[None, :]
    a_t = ((nn >= ratio * jj - lo) & (nn <= ratio * jj + ratio - 1) & (nn < n_pad - 1)).astype(np.float32)
    dh = NSA_HEAD_DIM
    col = lambda off: off // dh
    return pl.pallas_call(
        functools.partial(_nsa_prompt_body, seq=seq),
        grid=(bsz, NSA_GROUPS, qb),
        in_specs=[
            pl.BlockSpec((Q_BLOCK, hw), lambda b, g, i: (b * qb + i, OFF_Q // hw + g)),
            pl.BlockSpec((Q_BLOCK, EVEN_TAIL), lambda b, g, i: (b * qb + i, 0)),
            pl.BlockSpec((seq, dh), lambda b, g, i: (b, col(OFF_KV + 2 * NSA_KV) + g)),
            pl.BlockSpec((seq, dh), lambda b, g, i: (b, col(OFF_KV + 3 * NSA_KV) + g)),
            pl.BlockSpec((seq, dh), lambda b, g, i: (b, col(OFF_WIN) + g)),
            pl.BlockSpec((seq, dh), lambda b, g, i: (b, col(OFF_WIN + NSA_KV) + g)),
            pl.BlockSpec((1, 1, n_pad, dh), lambda b, g, i: (b, g, 0, 0)),
            pl.BlockSpec((1, 1, dh, n_pad), lambda b, g, i: (b, g, 0, 0)),
            pl.BlockSpec((1, CMP_BAND, hw), lambda b, g, i: (g, 0, 0)),
            pl.BlockSpec((1, 2, Q_BLOCK, hw), lambda b, g, i: (g, 0, 0, 0)),
            pl.BlockSpec((1, 1, hw), lambda b, g, i: (g, 0, 0)),
            pl.BlockSpec((n_slc, n_pad), lambda b, g, i: (0, 0)),
        ],
        out_specs=pl.BlockSpec((Q_BLOCK, hw), lambda b, g, i: (b * qb + i, g)),
        out_shape=jax.ShapeDtypeStruct((bsz * seq, NSA_Q), F32),
        scratch_shapes=[pltpu.VMEM((dh, seq), BF16), pltpu.VMEM((dh, seq), BF16),
                        pltpu.VMEM((n_slc, Q_BLOCK), F32), pltpu.VMEM((EVEN_TAIL, Q_BLOCK), F32),
                        pltpu.VMEM((n_pad + CMP_BAND // 2, hw), F32)],
        compiler_params=pltpu.CompilerParams(
            dimension_semantics=("parallel", "parallel", "arbitrary"), vmem_limit_bytes=VMEM_LIMIT_BYTES),
        name="nsa_prompt",
    )(main, tail, main, main, main, main, kcmp, vcmpt, cb, near, far, jnp.asarray(a_t, BF16))


CMP_COLS = 2 * NSA_KV
SUBS_PER_PAGE = PAGE_SIZE // CMP_STRIDE


def _cmp_pool_body(idx_ref, src_ref, a1_ref, a2_ref, o_ref, prev_ref):
    p = pl.program_id(1)
    x = src_ref[0]
    first = (x * a1_ref[...]).reshape(SUBS_PER_PAGE, CMP_STRIDE, CMP_COLS).sum(axis=1)
    second = (x * a2_ref[...]).reshape(SUBS_PER_PAGE, CMP_STRIDE, CMP_COLS).sum(axis=1)
    prev = jnp.where(p == 0, 0.0, prev_ref[...])
    shifted = jnp.concatenate([prev[SUBS_PER_PAGE - 1:, :], first[:SUBS_PER_PAGE - 1, :]], axis=0)
    o_ref[0] = shifted + second
    prev_ref[...] = first


def _cmp_pool_slab_body(idx_ref, src_ref, a1_ref, a2_ref, o_ref, prev_ref):
    p = pl.program_id(1)
    x = src_ref[0].reshape(SUBS_PER_PAGE, CMP_STRIDE, 2 * NSA_GROUPS, NSA_HEAD_DIM)
    first = (x * a1_ref[...][None]).sum(axis=1)
    second = (x * a2_ref[...][None]).sum(axis=1)
    prev = jnp.where(p == 0, 0.0, prev_ref[...])
    shifted = jnp.concatenate([prev[SUBS_PER_PAGE - 1:], first[:SUBS_PER_PAGE - 1]], axis=0)
    o_ref[0] = shifted + second
    prev_ref[...] = first


def cmp_pool_pages(cache, page_idx, cmp_alpha):
    bsz, n = page_idx.shape
    slabs = 2 * NSA_GROUPS
    dh = NSA_HEAD_DIM

    def tiled(half):
        return jnp.concatenate([jnp.repeat(cmp_alpha[0][half][:, None, :], NSA_GROUPS, axis=1),
                                jnp.repeat(cmp_alpha[1][half][:, None, :], NSA_GROUPS, axis=1)], axis=1)

    a1 = tiled(slice(0, CMP_STRIDE))
    a2 = tiled(slice(CMP_STRIDE, CMP_BLOCK))
    return pl.pallas_call(
        _cmp_pool_slab_body,
        grid_spec=pltpu.PrefetchScalarGridSpec(
            num_scalar_prefetch=1,
            grid=(bsz, n),
            in_specs=[
                pl.BlockSpec((1, PAGE_SIZE, slabs, dh), lambda b, p, idx: (idx[b, p], 0, 0, 0)),
                pl.BlockSpec((CMP_STRIDE, slabs, dh), lambda b, p, idx: (0, 0, 0)),
                pl.BlockSpec((CMP_STRIDE, slabs, dh), lambda b, p, idx: (0, 0, 0)),
            ],
            out_specs=pl.BlockSpec((1, SUBS_PER_PAGE, slabs, dh), lambda b, p, idx: (b, p, 0, 0)),
            scratch_shapes=[pltpu.VMEM((SUBS_PER_PAGE, slabs, dh), F32)],
        ),
        out_shape=jax.ShapeDtypeStruct((bsz, n * SUBS_PER_PAGE, slabs, dh), F32),
        compiler_params=pltpu.CompilerParams(
            dimension_semantics=("parallel", "arbitrary"), vmem_limit_bytes=VMEM_LIMIT_BYTES),
        name="cmp_pool_pages",
    )(page_idx, cache, a1, a2)


def cmp_pool(src, block_idx, col_block, cmp_alpha):
    bsz, n = block_idx.shape

    def tiled(half):
        a = jnp.concatenate([jnp.tile(cmp_alpha[0][half], (1, NSA_GROUPS)),
                             jnp.tile(cmp_alpha[1][half], (1, NSA_GROUPS))], axis=1)
        return jnp.tile(a, (SUBS_PER_PAGE, 1))

    a1 = tiled(slice(0, CMP_STRIDE))
    a2 = tiled(slice(CMP_STRIDE, CMP_BLOCK))
    return pl.pallas_call(
        _cmp_pool_body,
        grid_spec=pltpu.PrefetchScalarGridSpec(
            num_scalar_prefetch=1,
            grid=(bsz, n),
            in_specs=[
                pl.BlockSpec((1, PAGE_SIZE, CMP_COLS), lambda b, p, idx: (idx[b, p], 0, col_block)),
                pl.BlockSpec((PAGE_SIZE, CMP_COLS), lambda b, p, idx: (0, 0)),
                pl.BlockSpec((PAGE_SIZE, CMP_COLS), lambda b, p, idx: (0, 0)),
            ],
            out_specs=pl.BlockSpec((1, SUBS_PER_PAGE, CMP_COLS), lambda b, p, idx: (b, p, 0)),
            scratch_shapes=[pltpu.VMEM((SUBS_PER_PAGE, CMP_COLS), F32)],
        ),
        out_shape=jax.ShapeDtypeStruct((bsz, n * SUBS_PER_PAGE, CMP_COLS), F32),
        compiler_params=pltpu.CompilerParams(
            dimension_semantics=("parallel", "arbitrary"), vmem_limit_bytes=VMEM_LIMIT_BYTES),
        name="cmp_pool",
    )(block_idx, src, a1, a2)


def _cmp_project_body(p_ref, wk_ref, wv_ref, gk_ref, k_ref, vt_ref):
    n = p_ref.shape[1]
    dh = NSA_HEAD_DIM
    zero = jnp.zeros((1, dh), F32)
    pk = jnp.concatenate([p_ref[0, 1:, 0:dh], zero], axis=0)
    pv = jnp.concatenate([p_ref[0, 1:, dh:2 * dh], zero], axis=0)
    kp = _dot(pk.astype(BF16), wk_ref[...])
    k_ref[0, 0] = (kp * lax.rsqrt(jnp.mean(kp * kp, axis=-1, keepdims=True) + EPS)) * gk_ref[...]
    vp = _dot(pv.astype(BF16), wv_ref[...])
    for c in range(n // LANES):
        vt_ref[0, 0, :, c * LANES:(c + 1) * LANES] = vp[c * LANES:(c + 1) * LANES, :].T


def cmp_project(pooled, cmp_w, g_kcmp):
    bsz, n, _ = pooled.shape
    dh = NSA_HEAD_DIM
    pg = pooled.reshape(bsz, n, 2, NSA_GROUPS, dh).transpose(0, 3, 1, 2, 4).reshape(bsz * NSA_GROUPS, n, 2 * dh)
    k, vt = pl.pallas_call(
        _cmp_project_body,
        grid=(bsz, NSA_GROUPS),
        in_specs=[
            pl.BlockSpec((1, n, 2 * dh), lambda b, g: (b * NSA_GROUPS + g, 0, 0)),
            pl.BlockSpec((dh, dh), lambda b, g: (0, 0)),
            pl.BlockSpec((dh, dh), lambda b, g: (0, 0)),
            pl.BlockSpec((1, dh), lambda b, g: (0, 0)),
        ],
        out_specs=[pl.BlockSpec((1, 1, n, dh), lambda b, g: (b, g, 0, 0)),
                   pl.BlockSpec((1, 1, dh, n), lambda b, g: (b, g, 0, 0))],
        out_shape=[jax.ShapeDtypeStruct((bsz, NSA_GROUPS, n, dh), F32),
                   jax.ShapeDtypeStruct((bsz, NSA_GROUPS, dh, n), F32)],
        compiler_params=pltpu.CompilerParams(
            dimension_semantics=("parallel", "parallel"), vmem_limit_bytes=VMEM_LIMIT_BYTES),
        name="cmp_project",
    )(pg, cmp_w[0].astype(BF16), cmp_w[1].astype(BF16), g_kcmp.reshape(1, dh))
    return k, vt


def _nsa_decode_body(pt_ref, qbd_ref, pool_ref, wk_ref, wv_ref, gk_ref, cbias_ref, at_ref, gsum_ref, gexp_ref,
                     new_ref, nbias_ref, page_ref, lbias_ref, far_ref, win_ref, wbias_ref, gate_ref, o_ref,
                     kc_ref, vct_ref, oc_ref, score_ref, sel_ref, m_ref, l_ref, acc_ref, *, n_tok, pos0):
    s = pl.program_id(1)
    n_pages = pl.num_programs(1) - 1
    dh = NSA_HEAD_DIM
    gd = NSA_GROUPS * dh
    lanes = NSA_HEADS * n_tok
    n_cmp = pool_ref.shape[1]
    n_rows = score_ref.shape[0]
    qbd = qbd_ref[0]

    def kv_t(v):
        return jnp.concatenate([v[:, g * dh:(g + 1) * dh].T for g in range(NSA_GROUPS)], axis=0).astype(BF16)

    def masked_update(sc, mask_rows, v, init):
        sc = jnp.where(mask_rows > 0.0, sc, NEG_INF)
        m_old = jnp.full((1, lanes), NEG_INF, F32) if init else m_ref[...]
        m_new = jnp.maximum(m_old, jnp.max(sc, axis=0, keepdims=True))
        p = jnp.exp(sc - m_new)
        pv = _dot(kv_t(v), p.astype(BF16))
        if init:
            l_ref[...] = jnp.sum(p, axis=0, keepdims=True)
            acc_ref[...] = pv
        else:
            alpha = jnp.exp(m_old - m_new)
            l_ref[...] = alpha * l_ref[...] + jnp.sum(p, axis=0, keepdims=True)
            acc_ref[...] = alpha * acc_ref[...] + pv
        m_ref[...] = m_new

    @pl.when(s == 0)
    def _():
        for g in range(NSA_GROUPS):
            kp = _dot(pool_ref[0, :, g, :].astype(BF16), wk_ref[...])
            kp = (kp * lax.rsqrt(jnp.mean(kp * kp, axis=-1, keepdims=True) + EPS)) * gk_ref[...]
            kc_ref[:, g * dh:(g + 1) * dh] = kp.astype(BF16)
            vp = _dot(pool_ref[0, :, NSA_GROUPS + g, :].astype(BF16), wv_ref[...])
            for c in range(n_cmp // LANES):
                vct_ref[g * dh:(g + 1) * dh, c * LANES:(c + 1) * LANES] = vp[c * LANES:(c + 1) * LANES, :].T.astype(BF16)
        cb = cbias_ref[...]
        sc = _dot(kc_ref[...], qbd) + cb
        pc = jnp.exp(sc - jnp.max(sc, axis=0, keepdims=True))
        pc = pc / jnp.sum(pc, axis=0, keepdims=True)
        pc = pc * jnp.where(cb > 0.5 * NEG_INF, 1.0, 0.0)
        oc_ref[...] = _dot(vct_ref[...], pc.astype(BF16))
        at = at_ref[...]
        p1, p2, p3 = _split3(pc)
        u = (_dot(at, p1) + _dot(at, p2)) + _dot(at, p3)
        gs = gsum_ref[...]
        u1, u2, u3 = _split3(u)
        imp = (_dot(u1, gs) + _dot(u2, gs)) + _dot(u3, gs)
        jdx = lax.broadcasted_iota(jnp.int32, (n_rows, lanes), 0)
        tok = lax.broadcasted_iota(jnp.int32, (n_rows, lanes), 1) % n_tok
        cur = (pos0 + tok) // SLC_BLOCK
        n_blocks = (pos0 + n_tok + SLC_BLOCK - 1) // SLC_BLOCK
        forced = (jnp.where(jdx == 0, 1.0, 0.0) + jnp.where(jdx == cur, 1.0, 0.0)
                  + jnp.where(jdx == cur - 1, 1.0, 0.0))
        score = jnp.where(forced > 0.0, FORCE_SCORE, jnp.where(jdx <= cur, imp, -1.0))
        score = jnp.where(jdx < n_blocks, score, -2.0)
        score_ref[...] = score

        def rank_body(r, rank):
            row = jnp.broadcast_to(score_ref[pl.ds(r, 1), :], (n_rows, lanes))
            gt = jnp.where(row > score, 1.0, 0.0)
            ge = jnp.where(row >= score, 1.0, 0.0)
            return rank + jnp.where(jdx > r, ge, gt)

        rank = lax.fori_loop(0, n_blocks, rank_body, jnp.zeros((n_rows, lanes), F32))
        sel = jnp.where(rank < float(SLC_TOPN), 1.0, 0.0).astype(BF16)
        sel_ref[...] = _dot(sel, gexp_ref[...])
        new = new_ref[0]
        sc = _dot(new[:, :gd].astype(BF16), qbd) + nbias_ref[...]
        rows = jnp.broadcast_to(sel_ref[pl.ds(n_blocks - 1, 1), :], (PAGE_SIZE, lanes))
        masked_update(sc, rows, new[:, gd:], True)

    @pl.when(s > 0)
    def _():
        p = s - 1
        k_pg = jnp.concatenate([page_ref[0, :, g, :] for g in range(NSA_GROUPS)], axis=1)
        v_pg = jnp.concatenate([page_ref[0, :, NSA_GROUPS + g, :] for g in range(NSA_GROUPS)], axis=1)
        bias = jnp.where(p == n_pages - 1, lbias_ref[...], jnp.broadcast_to(far_ref[...], (PAGE_SIZE, lanes)))
        sc = _dot(k_pg.astype(BF16), qbd) + bias
        half = PAGE_SIZE // 2
        rows = jnp.concatenate([jnp.broadcast_to(sel_ref[pl.ds(2 * p, 1), :], (half, lanes)),
                                jnp.broadcast_to(sel_ref[pl.ds(2 * p + 1, 1), :], (half, lanes))], axis=0)
        masked_update(sc, rows, v_pg, False)

    @pl.when(s == n_pages)
    def _():
        o_s = acc_ref[...] / l_ref[...]
        win = win_ref[0]
        sw = _dot(win[:, :gd].astype(BF16), qbd) + wbias_ref[...]
        pw = jnp.exp(sw - jnp.max(sw, axis=0, keepdims=True))
        pw = pw / jnp.sum(pw, axis=0, keepdims=True)
        o_w = _dot(kv_t(win[:, gd:]), pw.astype(BF16))
        gates = jax.nn.sigmoid(gate_ref[0])
        o_t = (gates[0:1, :] * oc_ref[...] + gates[1:2, :] * o_s) + gates[2:3, :] * o_w
        for g in range(NSA_GROUPS):
            blk = o_t[g * dh:(g + 1) * dh, :].T
            for hh in range(NSA_HPG):
                h = g * NSA_HPG + hh
                o_ref[0, :, h * dh:(h + 1) * dh] = blk[h * n_tok:(h + 1) * n_tok, :]


def nsa_decode(main, tail, pooled, cache, page_table, win_state, pw, bsz, n_tok):
    n_pages = page_table.shape[1]
    pos0 = n_pages * PAGE_SIZE
    wb = win_state.shape[1]
    dh = NSA_HEAD_DIM
    gd = NSA_GROUPS * dh
    lanes = NSA_HEADS * n_tok
    assert lanes == LANES and pos0 % PAGE_SIZE == 0 and n_tok <= SLC_BLOCK
    n_cmp = pooled.shape[1]
    n_blocks = (pos0 + n_tok + SLC_BLOCK - 1) // SLC_BLOCK
    n_rows = -(-n_blocks // 8) * 8
    rel = pw['rel_table']
    m3 = main.reshape(bsz, n_tok, -1)
    q = m3[..., OFF_Q:OFF_KV].reshape(bsz, n_tok, NSA_HEADS, dh)
    grp = np.repeat(np.eye(NSA_GROUPS, dtype=np.float32), NSA_HPG, axis=0)
    qbd = jnp.einsum('bthd,hg->bgdht', q, jnp.asarray(grp)).reshape(bsz, gd, lanes).astype(BF16)
    t_l = np.tile(np.arange(n_tok), NSA_HEADS)[None, :]
    h_l = np.repeat(np.arange(NSA_HEADS), n_tok)

    rows = jnp.repeat(rel, n_tok, axis=1)

    def bias_table(dist, ok):
        return jnp.where(jnp.asarray(ok), _bias_select(rows, _rel_bucket_np(dist)), NEG_INF)

    r_c = np.arange(n_cmp)[:, None]
    dist_c = pos0 + t_l - ((r_c - 1) * CMP_STRIDE + CMP_BLOCK - 1)
    cbias = bias_table(dist_c, (r_c >= 1) & (dist_c >= 0))
    rr = np.arange(PAGE_SIZE)[:, None]
    nbias = bias_table(t_l - rr, (t_l - rr) >= 0)
    lbias = bias_table(PAGE_SIZE + t_l - rr, np.ones((PAGE_SIZE, lanes), bool))
    assert (_rel_bucket_np(np.arange(PAGE_SIZE + 1, 2 * PAGE_SIZE)) == REL_BUCKETS - 1).all()
    far = rows[REL_BUCKETS - 1:, :]
    n_win = -(-(wb + n_tok) // PAGE_SIZE) * PAGE_SIZE
    w_r = np.arange(n_win)[:, None]
    dist_w = pos0 + t_l - (pos0 - wb + w_r)
    wbias = bias_table(dist_w, (dist_w >= 0) & (dist_w <= WINDOW) & (pos0 - wb + w_r >= 0) & (w_r < wb + n_tok))
    ratio = SLC_BLOCK // CMP_STRIDE
    lo = CMP_BLOCK // CMP_STRIDE - 1
    jj = np.arange(n_rows)[:, None]
    nn = np.arange(n_cmp)[None, :] - 1
    a_t = ((nn >= ratio * jj - lo) & (nn <= ratio * jj + ratio - 1) & (nn >= 0) & (jj < n_blocks)).astype(np.float32)
    g_l = h_l // NSA_HPG
    gsum = np.zeros((lanes, lanes), np.float32)
    gsum[np.arange(lanes), g_l * n_tok + t_l[0]] = 1.0
    gexp = gsum.T.copy()
    new_kv = jnp.pad(m3[..., OFF_KV + 2 * NSA_KV:OFF_KV + 4 * NSA_KV], ((0, 0), (0, PAGE_SIZE - n_tok), (0, 0)))
    win_all = jnp.concatenate([win_state, m3[..., OFF_WIN:OFF_WIN + 2 * NSA_KV]], axis=1)
    win_all = jnp.pad(win_all, ((0, 0), (0, n_win - wb - n_tok), (0, 0)))
    gates = tail.reshape(bsz, n_tok, -1)[..., SSM_HEADS:SSM_HEADS + 3 * NSA_HEADS]
    gates = gates.reshape(bsz, n_tok, NSA_HEADS, 3).transpose(0, 3, 2, 1).reshape(bsz, 3, lanes)
    gates = jnp.pad(gates, ((0, 0), (0, 5), (0, 0)))
    const = lambda shape: pl.BlockSpec(shape, lambda b, s, pt: (0,) * len(shape))
    per_b = lambda shape: pl.BlockSpec((1,) + shape, lambda b, s, pt: (b,) + (0,) * len(shape))
    return pl.pallas_call(
        functools.partial(_nsa_decode_body, n_tok=n_tok, pos0=pos0),
        grid_spec=pltpu.PrefetchScalarGridSpec(
            num_scalar_prefetch=1,
            grid=(bsz, n_pages + 1),
            in_specs=[
                per_b((gd, lanes)), per_b((n_cmp, 2 * NSA_GROUPS, dh)), const((dh, dh)), const((dh, dh)),
                const((1, dh)),
                const((n_cmp, lanes)), const((n_rows, n_cmp)), const((lanes, lanes)), const((lanes, lanes)),
                per_b((PAGE_SIZE, 2 * NSA_KV)), const((PAGE_SIZE, lanes)),
                pl.BlockSpec((1, PAGE_SIZE, 2 * NSA_GROUPS, dh),
                             lambda b, s, pt: (pt[b, jnp.maximum(s - 1, 0)], 0, 1, 0)),
                const((PAGE_SIZE, lanes)), const((1, lanes)),
                per_b((n_win, 2 * NSA_KV)), const((n_win, lanes)), per_b((8, lanes)),
            ],
            out_specs=pl.BlockSpec((1, n_tok, NSA_Q), lambda b, s, pt: (b, 0, 0)),
            scratch_shapes=[
                pltpu.VMEM((n_cmp, gd), BF16), pltpu.VMEM((gd, n_cmp), BF16), pltpu.VMEM((gd, lanes), F32),
                pltpu.VMEM((n_rows, lanes), F32), pltpu.VMEM((n_rows, lanes), F32),
                pltpu.VMEM((1, lanes), F32), pltpu.VMEM((1, lanes), F32), pltpu.VMEM((gd, lanes), F32),
            ],
        ),
        out_shape=jax.ShapeDtypeStruct((bsz, n_tok, NSA_Q), F32),
        compiler_params=pltpu.CompilerParams(
            dimension_semantics=("parallel", "arbitrary"), vmem_limit_bytes=VMEM_LIMIT_BYTES),
        name="nsa_decode",
    )(page_table, qbd, pooled, pw['cmp_w'][0].astype(BF16), pw['cmp_w'][1].astype(BF16),
      pw['g_kcmp'].reshape(1, dh), cbias, jnp.asarray(a_t, BF16), jnp.asarray(gsum, BF16), jnp.asarray(gexp, BF16),
      new_kv, nbias, cache, lbias, far, win_all, wbias, gates)


def split_cols(h, sizes):
    offs = np.cumsum(sizes)[:-1].tolist()
    return jnp.split(h, offs, axis=-1)


def even_prompt(x, bsz, seq, wb, pw):
    main, tail = in_proj(x, pw['g_mix'], pw['w_main'], pw['flags'], pw['post_gain'], pw['post_scale'], pw['w_tail'])
    y_a, h_t = ssd_mix(main, tail, jnp.zeros((bsz, CONV_HALO, SSM_CONV_DIM), F32),
                       jnp.zeros((bsz, SSM_HEADS, SSM_STATE, SSM_HEAD_DIM), F32), pw, bsz, seq, seq)
    nb = seq // PAGE_SIZE
    idx = (jnp.arange(bsz, dtype=jnp.int32)[:, None] * nb + jnp.arange(nb, dtype=jnp.int32)[None, :])
    pooled = cmp_pool(main.reshape(bsz * nb, PAGE_SIZE, EVEN_MAIN), idx, OFF_KV // CMP_COLS, pw['cmp_alpha'])
    kcmp, vcmpt = cmp_project(pooled, pw['cmp_w'], pw['g_kcmp'])
    o_b = nsa_prompt(main, tail, kcmp, vcmpt, pw['nsa_tables'], bsz, seq)
    y = out_proj(x, y_a, o_b, pw['w_out_a'], pw['w_out_b'])
    m3 = main.reshape(bsz, seq, EVEN_MAIN)
    kv = m3[..., OFF_KV:OFF_WIN].reshape(bsz, seq, NSA_KV_PARTS, NSA_GROUPS, NSA_HEAD_DIM)
    new_win = m3[:, seq - wb:, OFF_WIN:].reshape(bsz, wb, 2, NSA_GROUPS, NSA_HEAD_DIM)
    new_conv = m3[:, seq - (SSM_CONV - 1):, OFF_XBC:OFF_Q]
    return y, kv, new_win, new_conv, h_t.transpose(0, 1, 3, 2)


def even_decode(x, bsz, n_tok, cache, page_idx, win_state, conv_state, ssm_state, pw):
    wb = win_state.shape[1]
    main, tail = in_proj(x, pw['g_mix'], pw['w_main'], pw['flags'], pw['post_gain'], pw['post_scale'], pw['w_tail'])
    m3 = main.reshape(bsz, n_tok, EVEN_MAIN)
    pad_rows = ((0, 0), (0, SSM_CHUNK - n_tok), (0, 0))
    main_p = jnp.pad(m3[..., :OFF_Q], pad_rows).reshape(bsz * SSM_CHUNK, OFF_Q)
    tail_p = jnp.pad(tail.reshape(bsz, n_tok, EVEN_TAIL), pad_rows).reshape(bsz * SSM_CHUNK, EVEN_TAIL)
    halo = jnp.pad(conv_state, ((0, 0), (CONV_HALO - (SSM_CONV - 1), 0), (0, 0)))
    y_a, h_t = ssd_mix(main_p, tail_p, halo, ssm_state.transpose(0, 1, 3, 2), pw, bsz, SSM_CHUNK, n_tok)
    y_a = y_a.reshape(bsz, SSM_CHUNK, SSM_D_INNER)[:, :n_tok].reshape(bsz * n_tok, SSM_D_INNER)
    pooled = cmp_pool_pages(cache, page_idx, pw['cmp_alpha'])
    win2 = win_state.reshape(bsz, wb, 2 * NSA_KV)
    o_b = nsa_decode(main, tail, pooled, cache, page_idx, win2, pw, bsz, n_tok)
    y = out_proj(x, y_a, o_b.reshape(bsz * n_tok, NSA_Q), pw['w_out_a'], pw['w_out_b'])
    kv = m3[..., OFF_KV:OFF_WIN].reshape(bsz, n_tok, NSA_KV_PARTS, NSA_GROUPS, NSA_HEAD_DIM)
    new_win = jnp.concatenate([win2, m3[..., OFF_WIN:]], axis=1)[:, -wb:]
    new_win = new_win.reshape(bsz, wb, 2, NSA_GROUPS, NSA_HEAD_DIM)
    new_conv = jnp.concatenate([conv_state, m3[..., OFF_XBC:OFF_Q]], axis=1)[:, -(SSM_CONV - 1):]
    return y, kv, new_win, new_conv, h_t.transpose(0, 1, 3, 2)


def odd_prompt(x, bsz, seq, pw):
    h = in_proj(x, pw['g_mix'], pw['w_in'], pw['flags'], pw['post_gain'], pw['post_scale'])
    y_c = pool_mix_pallas(h, jnp.zeros((bsz, POOL_HALO, POOL_DIM), F32), pw['pool_w'], pw['pool_scale'], bsz, seq, 0)
    o = sb_prompt(h, bsz, seq)
    y = out_proj(x, y_c, o, pw['w_out_a'], pw['w_out_b'])
    h3 = h.reshape(bsz, seq, -1)
    kv = h3[..., POOL_DIM + SB_DIM:].reshape(bsz, seq, 2, SB_HEADS, SB_HEAD_DIM)
    return y, kv, h3[:, seq - POOL_BUF:, :POOL_DIM]


def odd_decode(x, bsz, n_tok, cache, page_idx, pool_state, pw):
    pos0 = page_idx.shape[1] * PAGE_SIZE
    h = in_proj(x, pw['g_mix'], pw['w_in'], pw['flags'], pw['post_gain'], pw['post_scale'])
    buf16 = jnp.pad(pool_state, ((0, 0), (POOL_HALO - POOL_BUF, 0), (0, 0)))
    y_c = pool_mix_pallas(h, buf16, pw['pool_w'], pw['pool_scale'], bsz, n_tok, pos0)
    o = sb_decode(h, cache, page_idx, bsz, n_tok)
    y = out_proj(x, y_c, o.reshape(bsz * n_tok, SB_DIM), pw['w_out_a'], pw['w_out_b'])
    h3 = h.reshape(bsz, n_tok, -1)
    kv = h3[..., POOL_DIM + SB_DIM:].reshape(bsz, n_tok, 2, SB_HEADS, SB_HEAD_DIM)
    new_pool = jnp.concatenate([pool_state, h3[..., :POOL_DIM]], axis=1)[:, -POOL_BUF:]
    return y, kv, new_pool


def _even_weights(l, e, mix_norm, w_in_even, w_out_even, ssm_conv_w, ssm_conv_b, ssm_dt_bias, ssm_a_log, ssm_d,
                  ssm_norm, nsa_cmp_alpha, nsa_cmp_w, nsa_qk_gain, rel_bias, nsa_tables, tn):
    w = w_in_even[e]
    sizes = (SSM_D_INNER, SSM_CONV_DIM, SSM_HEADS, NSA_Q) + (NSA_KV,) * 6 + (3 * NSA_HEADS,)
    wz, wxbc, wdt, wq, wkc, wvc, wks, wvs, wkw, wvw, wg = split_cols(w, sizes)
    w_main = jnp.concatenate([wz, wxbc, wq, wkc, wvc, wks, wvs, wkw, wvw], axis=1).astype(BF16)
    pad = EVEN_TAIL - SSM_HEADS - 3 * NSA_HEADS
    w_tail = jnp.concatenate([wdt, wg, jnp.zeros((D_MODEL, pad), F32)], axis=1).astype(BF16)
    gain = nsa_qk_gain[e]
    ones = jnp.ones((NSA_KV,), F32)
    post_gain = jnp.concatenate([
        jnp.ones((OFF_Q,), F32), jnp.tile(gain[0], NSA_HEADS), ones, ones, jnp.tile(gain[2], NSA_GROUPS), ones,
        jnp.tile(gain[3], NSA_GROUPS), ones]).reshape(1, EVEN_MAIN)
    post_scale = jnp.concatenate([
        jnp.ones((OFF_Q,), F32), jnp.full((NSA_Q,), NSA_HEAD_DIM ** -0.5, F32),
        jnp.ones((6 * NSA_KV,), F32)]).reshape(1, EVEN_MAIN)
    flags = np.zeros((EVEN_MAIN // tn,), np.int32)
    for lo_, hi_ in ((OFF_Q, OFF_KV), (OFF_KV + 2 * NSA_KV, OFF_KV + 3 * NSA_KV), (OFF_WIN, OFF_WIN + NSA_KV)):
        assert lo_ % tn == 0 and hi_ % tn == 0
        flags[lo_ // tn:hi_ // tn] = 1
    wo = w_out_even[e].astype(BF16)
    return dict(g_mix=mix_norm[l], w_main=w_main, w_tail=w_tail, flags=jnp.asarray(flags), post_gain=post_gain,
                post_scale=post_scale, conv_w=ssm_conv_w[e], conv_b=ssm_conv_b[e], dt_bias=ssm_dt_bias[e],
                a_log=ssm_a_log[e], d_skip=ssm_d[e], ssm_g=ssm_norm[e], cmp_alpha=nsa_cmp_alpha[e],
                cmp_w=nsa_cmp_w[e], g_kcmp=gain[1], rel_table=rel_bias, nsa_tables=nsa_tables,
                w_out_a=wo[:SSM_D_INNER], w_out_b=wo[SSM_D_INNER:])


def _odd_weights(l, o, mix_norm, w_in_odd, w_out_odd, pool_w, pool_scale, tn):
    n = w_in_odd.shape[2]
    wo = w_out_odd[o].astype(BF16)
    return dict(g_mix=mix_norm[l], w_in=w_in_odd[o].astype(BF16), flags=jnp.zeros((n // tn,), jnp.int32),
                post_gain=jnp.ones((1, n), F32), post_scale=jnp.ones((1, n), F32),
                pool_w=pool_w[o], pool_scale=pool_scale[o], w_out_a=wo[:POOL_DIM], w_out_b=wo[POOL_DIM:])


def kernel(x_prompt, x_sample, cache_nsa_kv, cache_sb_kv, state_nsa_win, state_ssm, state_conv, state_pool, page_table, ffn_norm, ffn_w_gate, ffn_w_up, ffn_w_down, mix_norm, w_in_even, w_out_even, ssm_conv_w, ssm_conv_b, ssm_dt_bias, ssm_a_log, ssm_d, ssm_norm, nsa_cmp_alpha, nsa_cmp_w, nsa_qk_gain, rel_bias, w_in_odd, w_out_odd, pool_w, pool_scale):
    bp, lp, _ = x_prompt.shape
    bs, ls, _ = x_sample.shape
    wb = state_nsa_win.shape[2]
    n_phys = cache_nsa_kv.shape[1]
    tn = 512
    xp = x_prompt.reshape(bp * lp, D_MODEL)
    xs = x_sample.reshape(bs * ls, D_MODEL)
    wg_all = ffn_w_gate.astype(BF16)
    wu_all = ffn_w_up.astype(BF16)
    wd_all = ffn_w_down.astype(BF16)
    nsa_tables = nsa_bias_tables(rel_bias, lp)
    nsa_pages = cache_nsa_kv.reshape(-1, PAGE_SIZE, NSA_KV_PARTS * NSA_GROUPS, NSA_HEAD_DIM)
    sb_pages = cache_sb_kv.reshape(-1, PAGE_SIZE, 2 * SB_HEADS, SB_HEAD_DIM)
    kv_p, kv_s, sb_p, sb_s, win_p, win_s = [], [], [], [], [], []
    ssm_p, ssm_s, conv_p, conv_s, pool_p, pool_s = [], [], [], [], [], []
    for l in range(DEPTH):
        fa = (ffn_norm[l, 0], wg_all[l, 0], wu_all[l, 0], wd_all[l, 0])
        xp = ffn_half(xp, *fa)
        xs = ffn_half(xs, *fa)
        if l % 2 == 0:
            e = l // 2
            pw = _even_weights(l, e, mix_norm, w_in_even, w_out_even, ssm_conv_w, ssm_conv_b, ssm_dt_bias,
                               ssm_a_log, ssm_d, ssm_norm, nsa_cmp_alpha, nsa_cmp_w, nsa_qk_gain, rel_bias, nsa_tables, tn)
            xp, a_kv, a_win, a_conv, a_h = even_prompt(xp, bp, lp, wb, pw)
            xs, b_kv, b_win, b_conv, b_h = even_decode(xs, bs, ls, nsa_pages, page_table + e * n_phys,
                                                       state_nsa_win[e], state_conv[e], state_ssm[e], pw)
            kv_p.append(a_kv); kv_s.append(b_kv)
            win_p.append(a_win); win_s.append(b_win)
            conv_p.append(a_conv); conv_s.append(b_conv)
            ssm_p.append(a_h); ssm_s.append(b_h)
        else:
            o = l // 2
            pw = _odd_weights(l, o, mix_norm, w_in_odd, w_out_odd, pool_w, pool_scale, tn)
            xp, a_kv, a_pool = odd_prompt(xp, bp, lp, pw)
            xs, b_kv, b_pool = odd_decode(xs, bs, ls, sb_pages, page_table + o * n_phys, state_pool[o], pw)
            sb_p.append(a_kv); sb_s.append(b_kv)
            pool_p.append(a_pool); pool_s.append(b_pool)
        fb = (ffn_norm[l, 1], wg_all[l, 1], wu_all[l, 1], wd_all[l, 1])
        xp = ffn_half(xp, *fb)
        xs = ffn_half(xs, *fb)
    return (xp.reshape(bp, lp, D_MODEL), xs.reshape(bs, ls, D_MODEL), jnp.stack(kv_p), jnp.stack(kv_s),
            jnp.stack(sb_p), jnp.stack(sb_s), jnp.stack(win_p), jnp.stack(win_s), jnp.stack(ssm_p), jnp.stack(ssm_s),
            jnp.stack(conv_p), jnp.stack(conv_s), jnp.stack(pool_p), jnp.stack(pool_s))
```

```python
import functools
import math

import jax
import jax.numpy as jnp
import numpy as np
from jax import lax
from jax.experimental import pallas as pl
from jax.experimental.pallas import tpu as pltpu

F32 = jnp.float32
BF16 = jnp.bfloat16

D_MODEL = 2048
DEPTH = 4
PAGE_SIZE = 128
EPS = 1e-6
NEG_INF = -1e30
Q_BLOCK = 128
FFN_RESIDUAL = 0.5
D_FF = 5632

SSM_HEADS = 32
SSM_HEAD_DIM = 64
SSM_D_INNER = SSM_HEADS * SSM_HEAD_DIM
SSM_GROUPS = 4
SSM_STATE = 128
SSM_CONV = 4
SSM_CONV_DIM = SSM_D_INNER + 2 * SSM_GROUPS * SSM_STATE
SSM_CHUNK = 128

NSA_HEADS = 16
NSA_GROUPS = 4
NSA_HPG = NSA_HEADS // NSA_GROUPS
NSA_HEAD_DIM = 128
NSA_Q = NSA_HEADS * NSA_HEAD_DIM
NSA_KV = NSA_GROUPS * NSA_HEAD_DIM
NSA_KV_PARTS = 4
CMP_BLOCK = 32
CMP_STRIDE = 16
SLC_BLOCK = 64
SLC_TOPN = 16
WINDOW = 512
FORCE_SCORE = 1e4
REL_BUCKETS = 32
REL_MAX_DIST = 128
CMP_BAND = 2 * (Q_BLOCK // CMP_STRIDE)

POOL_WINDOWS = (2, 4, 8, 16)
POOL_GROUPS = len(POOL_WINDOWS)
POOL_GROUP_DIM = 256
POOL_DIM = POOL_GROUPS * POOL_GROUP_DIM
POOL_BUF = max(POOL_WINDOWS) - 1

SB_HEADS = 16
SB_HEAD_DIM = 128
SB_DIM = SB_HEADS * SB_HEAD_DIM

LANES = 128
VMEM_LIMIT_BYTES = 56 * 1024 * 1024

EVEN_MAIN = SSM_D_INNER + SSM_CONV_DIM + NSA_Q + 6 * NSA_KV
EVEN_TAIL = LANES
OFF_Z = 0
OFF_XBC = SSM_D_INNER
OFF_Q = OFF_XBC + SSM_CONV_DIM
OFF_KV = OFF_Q + NSA_Q
OFF_WIN = OFF_KV + 4 * NSA_KV


def _row_tile(m, want):
    t = min(m, want)
    assert m % t == 0
    return t


def _ffn_body(x_ref, g_ref, wg_ref, wu_ref, wd_ref, o_ref, h_ref, acc_ref):
    j = pl.program_id(1)

    @pl.when(j == 0)
    def _():
        x = x_ref[...]
        y = x * lax.rsqrt(jnp.mean(x * x, axis=-1, keepdims=True) + EPS)
        h_ref[...] = (y * g_ref[...]).astype(BF16)
        acc_ref[...] = jnp.zeros_like(acc_ref)

    h = h_ref[...]
    a = jnp.dot(h, wg_ref[...], preferred_element_type=F32)
    b = jnp.dot(h, wu_ref[...], preferred_element_type=F32)
    t = (a * jax.nn.sigmoid(a)) * b
    acc_ref[...] += jnp.dot(t.astype(BF16), wd_ref[...], preferred_element_type=F32)

    @pl.when(j == pl.num_programs(1) - 1)
    def _():
        o_ref[...] = x_ref[...] + FFN_RESIDUAL * acc_ref[...]


def ffn_half(x, g, wg, wu, wd, *, tm=512, tf=512):
    m, d = x.shape
    f = wg.shape[1]
    tm = _row_tile(m, tm)
    assert f % tf == 0
    return pl.pallas_call(
        _ffn_body,
        grid=(m // tm, f // tf),
        in_specs=[
            pl.BlockSpec((tm, d), lambda i, j: (i, 0)),
            pl.BlockSpec((1, d), lambda i, j: (0, 0)),
            pl.BlockSpec((d, tf), lambda i, j: (0, j)),
            pl.BlockSpec((d, tf), lambda i, j: (0, j)),
            pl.BlockSpec((tf, d), lambda i, j: (j, 0)),
        ],
        out_specs=pl.BlockSpec((tm, d), lambda i, j: (i, 0)),
        out_shape=jax.ShapeDtypeStruct((m, d), F32),
        scratch_shapes=[pltpu.VMEM((tm, d), BF16), pltpu.VMEM((tm, d), F32)],
        compiler_params=pltpu.CompilerParams(
            dimension_semantics=("parallel", "arbitrary"), vmem_limit_bytes=VMEM_LIMIT_BYTES),
        name="ffn_half",
    )(x, g.reshape(1, d), wg, wu, wd)


def _proj_body(flag_ref, x_ref, g_ref, w_ref, pg_ref, ps_ref, *rest, has_tail):
    if has_tail:
        wt_ref, o_ref, ot_ref, h_ref = rest
    else:
        o_ref, h_ref = rest
    j = pl.program_id(1)

    @pl.when(j == 0)
    def _():
        x = x_ref[...]
        y = x * lax.rsqrt(jnp.mean(x * x, axis=-1, keepdims=True) + EPS)
        h = (y * g_ref[...]).astype(BF16)
        h_ref[...] = h
        if has_tail:
            ot_ref[...] = jnp.dot(h, wt_ref[...], preferred_element_type=F32)

    r = jnp.dot(h_ref[...], w_ref[...], preferred_element_type=F32)

    @pl.when(flag_ref[j] == 0)
    def _():
        o_ref[...] = r

    @pl.when(flag_ref[j] != 0)
    def _():
        tm, tn = r.shape
        for c in range(tn // LANES):
            rc = r[:, c * LANES:(c + 1) * LANES]
            yc = rc * lax.rsqrt(jnp.mean(rc * rc, axis=-1, keepdims=True) + EPS)
            yc = (yc * pg_ref[:, c * LANES:(c + 1) * LANES]) * ps_ref[:, c * LANES:(c + 1) * LANES]
            o_ref[:, c * LANES:(c + 1) * LANES] = yc


def in_proj(x, g, w, flags, post_gain, post_scale, w_tail=None, *, tm=512, tn=512):
    m, d = x.shape
    n = w.shape[1]
    tm = _row_tile(m, tm)
    assert n % tn == 0
    has_tail = w_tail is not None
    in_specs = [
        pl.BlockSpec((tm, d), lambda i, j, fl: (i, 0)),
        pl.BlockSpec((1, d), lambda i, j, fl: (0, 0)),
        pl.BlockSpec((d, tn), lambda i, j, fl: (0, j)),
        pl.BlockSpec((1, tn), lambda i, j, fl: (0, j)),
        pl.BlockSpec((1, tn), lambda i, j, fl: (0, j)),
    ]
    out_specs = [pl.BlockSpec((tm, tn), lambda i, j, fl: (i, j))]
    out_shape = [jax.ShapeDtypeStruct((m, n), F32)]
    args = [x, g.reshape(1, d), w, post_gain, post_scale]
    if has_tail:
        nt = w_tail.shape[1]
        in_specs.append(pl.BlockSpec((d, nt), lambda i, j, fl: (0, 0)))
        out_specs.append(pl.BlockSpec((tm, nt), lambda i, j, fl: (i, 0)))
        out_shape.append(jax.ShapeDtypeStruct((m, nt), F32))
        args.append(w_tail)
    res = pl.pallas_call(
        functools.partial(_proj_body, has_tail=has_tail),
        grid_spec=pltpu.PrefetchScalarGridSpec(
            num_scalar_prefetch=1,
            grid=(m // tm, n // tn),
            in_specs=in_specs,
            out_specs=out_specs,
            scratch_shapes=[pltpu.VMEM((tm, d), BF16)],
        ),
        out_shape=out_shape,
        compiler_params=pltpu.CompilerParams(
            dimension_semantics=("parallel", "arbitrary"), vmem_limit_bytes=VMEM_LIMIT_BYTES),
        name="in_proj",
    )(flags, *args)
    return res if has_tail else res[0]


def _out_proj_body(x_ref, a1_ref, a2_ref, w1_ref, w2_ref, o_ref):
    r = jnp.dot(a1_ref[...].astype(BF16), w1_ref[...], preferred_element_type=F32)
    r = r + jnp.dot(a2_ref[...].astype(BF16), w2_ref[...], preferred_element_type=F32)
    o_ref[...] = x_ref[...] + r


def out_proj(x, a1, a2, w1, w2, *, tm=512, tn=512):
    m, d = x.shape
    k1, k2 = a1.shape[1], a2.shape[1]
    tm = _row_tile(m, tm)
    return pl.pallas_call(
        _out_proj_body,
        grid=(m // tm, d // tn),
        in_specs=[
            pl.BlockSpec((tm, tn), lambda i, j: (i, j)),
            pl.BlockSpec((tm, k1), lambda i, j: (i, 0)),
            pl.BlockSpec((tm, k2), lambda i, j: (i, 0)),
            pl.BlockSpec((k1, tn), lambda i, j: (0, j)),
            pl.BlockSpec((k2, tn), lambda i, j: (0, j)),
        ],
        out_specs=pl.BlockSpec((tm, tn), lambda i, j: (i, j)),
        out_shape=jax.ShapeDtypeStruct((m, d), F32),
        compiler_params=pltpu.CompilerParams(
            dimension_semantics=("parallel", "arbitrary"), vmem_limit_bytes=VMEM_LIMIT_BYTES),
        name="out_proj",
    )(x, a1, a2, w1, w2)


def _dot_nt(a, b):
    return lax.dot_general(a, b, (((1,), (1,)), ((), ())), preferred_element_type=F32)


def _dot(a, b):
    return jnp.dot(a, b, preferred_element_type=F32)


def _softplus(z):
    return jnp.maximum(z, 0.0) + jnp.log(1.0 + jnp.exp(-jnp.abs(z)))


SB_KEY_CHUNK = 4 * Q_BLOCK
SB_HEADS_PER_STEP = 2
SB_DEAD_RUN = 110.0


def _sb_prompt_body(q_ref, k_ref, v_ref, tri_ref, o_ref):
    i = pl.program_id(2)
    blk = Q_BLOCK
    ck = SB_KEY_CHUNK
    nsub = ck // blk
    dh = SB_HEAD_DIM
    heads = range(SB_HEADS_PER_STEP)
    qs = [(q_ref[:, h * dh:(h + 1) * dh] * (dh ** -0.5)).astype(BF16) for h in heads]
    tri = tri_ref[...]

    def chunk_head(h, c, run, acc, diag):
        r0 = pl.multiple_of(c * ck, ck)
        z = _dot_nt(qs[h], k_ref[pl.ds(r0, ck), h * dh:(h + 1) * dh].astype(BF16))
        sp = _softplus(z)
        if diag:
            t_pos = i * blk + lax.broadcasted_iota(jnp.int32, (blk, ck), 0)
            s_pos = c * ck + lax.broadcasted_iota(jnp.int32, (blk, ck), 1)
            strict = s_pos < t_pos
            go = jnp.where(strict, sp, 0.0)
        else:
            go = sp
        hi = go.astype(BF16)
        lo = (go - hi.astype(F32)).astype(BF16)
        suffix = [None] * nsub
        for j in reversed(range(nsub)):
            sl = slice(j * blk, (j + 1) * blk)
            ct = _dot(hi[:, sl], tri) + _dot(lo[:, sl], tri)
            suffix[j] = (ct[:, :blk] - go[:, sl]) + run
            run = run + ct[:, blk:]
        att = jnp.exp((z - sp) - jnp.concatenate(suffix, axis=1))
        if diag:
            att = jnp.where(strict, att, 0.0)
        acc = acc + _dot(att.astype(BF16), v_ref[pl.ds(r0, ck), h * dh:(h + 1) * dh].astype(BF16))
        return run, acc

    def chunk(c, carry, diag):
        return tuple(chunk_head(h, c, carry[h][0], carry[h][1], diag) for h in heads)

    cd = i // nsub
    init = tuple((jnp.zeros((blk, blk), F32), jnp.zeros((blk, dh), F32)) for _ in heads)
    carry = chunk(cd, init, True)

    def live(state):
        r, cr = state
        lowest = functools.reduce(jnp.minimum, [jnp.min(cr[h][0]) for h in heads])
        return jnp.logical_and(r < cd, lowest <= SB_DEAD_RUN)

    _, carry = lax.while_loop(live, lambda st: (st[0] + 1, chunk(cd - 1 - st[0], st[1], False)), (jnp.int32(0), carry))
    for h in heads:
        o_ref[:, h * dh:(h + 1) * dh] = carry[h][1]


def sb_prompt(h, bsz, seq):
    qb = seq // Q_BLOCK
    hw = SB_HEADS_PER_STEP * SB_HEAD_DIM
    q_off = POOL_DIM // hw
    k_off = q_off + SB_DIM // hw
    v_off = k_off + SB_DIM // hw
    jj = np.arange(Q_BLOCK)
    tri = np.concatenate([(jj[:, None] >= jj[None, :]).astype(np.float32),
                          np.ones((Q_BLOCK, Q_BLOCK), np.float32)], axis=1)
    return pl.pallas_call(
        _sb_prompt_body,
        grid=(bsz, SB_DIM // hw, qb),
        in_specs=[
            pl.BlockSpec((Q_BLOCK, hw), lambda b, hh, i: (b * qb + i, q_off + hh)),
            pl.BlockSpec((seq, hw), lambda b, hh, i: (b, k_off + hh)),
            pl.BlockSpec((seq, hw), lambda b, hh, i: (b, v_off + hh)),
            pl.BlockSpec((Q_BLOCK, 2 * Q_BLOCK), lambda b, hh, i: (0, 0)),
        ],
        out_specs=pl.BlockSpec((Q_BLOCK, hw), lambda b, hh, i: (b * qb + i, hh)),
        out_shape=jax.ShapeDtypeStruct((bsz * seq, SB_DIM), F32),
        compiler_params=pltpu.CompilerParams(
            dimension_semantics=("parallel", "parallel", "arbitrary"), vmem_limit_bytes=VMEM_LIMIT_BYTES),
        name="sb_prompt",
    )(h, h, h, jnp.asarray(tri, BF16))


CONV_HALO = 8


def _split3(x):
    x1 = x.astype(BF16)
    r = x - x1.astype(F32)
    x2 = r.astype(BF16)
    x3 = (r - x2.astype(F32)).astype(BF16)
    return x1, x2, x3


def _ssd_body(z_ref, x_ref, bc_ref, xh_ref, bch_ref, cv_ref, tail_ref, h0_ref, cw_ref, cb_ref, dtb_r_ref, alog_r_ref,
              dtb_c_ref, alog_c_ref, dsk_ref, ng_ref, tril_ref, y_ref, ht_ref, st_ref, *, n_valid):
    c = pl.program_id(1)
    q = SSM_CHUNK
    p_dim = SSM_HEAD_DIM
    hpg = SSM_HEADS // SSM_GROUPS
    gn = SSM_GROUPS * SSM_STATE

    @pl.when(c == 0)
    def _():
        st_ref[...] = h0_ref[0]

    first = c == 0
    cv = cv_ref[0]
    xe = jnp.concatenate([jnp.where(first, cv[:, :SSM_D_INNER], xh_ref[...]), x_ref[...]], axis=0)
    bce = jnp.concatenate([jnp.where(first, cv[:, SSM_D_INNER:], bch_ref[...]), bc_ref[...]], axis=0)

    def conv(e, lo, hi):
        acc = cb_ref[:, lo:hi]
        for k in range(SSM_CONV):
            r0 = CONV_HALO - (SSM_CONV - 1) + k
            acc = acc + e[r0:r0 + q, :] * cw_ref[k:k + 1, lo:hi]
        return acc * jax.nn.sigmoid(acc)

    xs = conv(xe, 0, SSM_D_INNER)
    bcm = conv(bce, SSM_D_INNER, SSM_CONV_DIM)

    tail = tail_ref[...]
    t_row = lax.broadcasted_iota(jnp.int32, (q, LANES), 0)
    t_lane = lax.broadcasted_iota(jnp.int32, (LANES, q), 1)
    dt_r = jnp.where(t_row < n_valid, _softplus1p(tail + dtb_r_ref[...]), 0.0)
    dt_c = jnp.where(t_lane < n_valid, _softplus1p(tail.T + dtb_c_ref[...]), 0.0)
    da_r = dt_r * (-jnp.exp(alog_r_ref[...]))
    da_c = dt_c * (-jnp.exp(alog_c_ref[...]))
    tril = tril_ref[...]
    a1, a2, a3 = _split3(da_r)
    acum = (_dot(tril, a1) + _dot(tril, a2)) + _dot(tril, a3)
    c1, c2, c3 = _split3(da_c)
    acum_t = (_dot_nt(c1, tril) + _dot_nt(c2, tril)) + _dot_nt(c3, tril)
    last = acum[q - 1:q, :]
    causal = lax.broadcasted_iota(jnp.int32, (q, q), 1) <= lax.broadcasted_iota(jnp.int32, (q, q), 0)

    for g in range(SSM_GROUPS):
        bg = bcm[:, g * SSM_STATE:(g + 1) * SSM_STATE]
        cg = bcm[:, gn + g * SSM_STATE:gn + (g + 1) * SSM_STATE].astype(BF16)
        cbg = _dot_nt(cg, bg.astype(BF16))
        bg_t = bg.T.astype(BF16)
        for hh in range(hpg):
            h = g * hpg + hh
            col = acum[:, h:h + 1]
            decay = jnp.exp(jnp.where(causal, col - acum_t[h:h + 1, :], NEG_INF))
            xh = xs[:, h * p_dim:(h + 1) * p_dim]
            xdt = xh * dt_r[:, h:h + 1]
            y_diag = _dot((cbg * decay).astype(BF16), xdt.astype(BF16))
            s_t = st_ref[h]
            y_off = _dot(cg, s_t.astype(BF16)) * jnp.exp(col)
            end = last[:, h:h + 1]
            st_ref[h] = s_t * jnp.exp(end) + _dot(bg_t, (xdt * jnp.exp(end - col)).astype(BF16))
            y_ref[:, h * p_dim:(h + 1) * p_dim] = (y_diag + y_off) + dsk_ref[:, h * p_dim:(h + 1) * p_dim] * xh

    z = z_ref[...]
    y = y_ref[...] * (z * jax.nn.sigmoid(z))
    gw = SSM_D_INNER // SSM_GROUPS
    for g in range(SSM_GROUPS):
        yg = y[:, g * gw:(g + 1) * gw]
        yg = yg * lax.rsqrt(jnp.mean(yg * yg, axis=-1, keepdims=True) + EPS)
        y_ref[:, g * gw:(g + 1) * gw] = yg * ng_ref[:, g * gw:(g + 1) * gw]

    @pl.when(c == pl.num_programs(1) - 1)
    def _():
        ht_ref[0] = st_ref[...]


def _softplus1p(x):
    return jnp.maximum(x, 0.0) + jnp.log1p(jnp.exp(-jnp.abs(x)))


def ssd_mix(main, tail, conv_halo, h0_t, pw, bsz, seq, n_valid):
    q = SSM_CHUNK
    nc = seq // q
    per = q // CONV_HALO
    bcw = SSM_CONV_DIM - SSM_D_INNER
    pad_h = LANES - SSM_HEADS
    dtb = jnp.pad(pw['dt_bias'], (0, pad_h))
    alog = jnp.pad(pw['a_log'], (0, pad_h))
    tt = np.arange(q)
    tril = (tt[None, :] <= tt[:, None]).astype(np.float32)
    halo = lambda b, c: (jnp.maximum((b * nc + c) * per - 1, 0), 0)
    st_shape = (SSM_HEADS, SSM_STATE, SSM_HEAD_DIM)
    vec = lambda n: pl.BlockSpec((1, n), lambda b, c: (0, 0))
    colv = pl.BlockSpec((LANES, 1), lambda b, c: (0, 0))
    y, ht = pl.pallas_call(
        functools.partial(_ssd_body, n_valid=n_valid),
        grid=(bsz, nc),
        in_specs=[
            pl.BlockSpec((q, SSM_D_INNER), lambda b, c: (b * nc + c, 0)),
            pl.BlockSpec((q, SSM_D_INNER), lambda b, c: (b * nc + c, 1)),
            pl.BlockSpec((q, bcw), lambda b, c: (b * nc + c, 2 * SSM_D_INNER // bcw)),
            pl.BlockSpec((CONV_HALO, SSM_D_INNER), lambda b, c: (halo(b, c)[0], 1)),
            pl.BlockSpec((CONV_HALO, bcw), lambda b, c: (halo(b, c)[0], 2 * SSM_D_INNER // bcw)),
            pl.BlockSpec((1, CONV_HALO, SSM_CONV_DIM), lambda b, c: (b, 0, 0)),
            pl.BlockSpec((q, LANES), lambda b, c: (b * nc + c, 0)),
            pl.BlockSpec((1,) + st_shape, lambda b, c: (b, 0, 0, 0)),
            pl.BlockSpec((SSM_CONV, SSM_CONV_DIM), lambda b, c: (0, 0)),
            vec(SSM_CONV_DIM), vec(LANES), vec(LANES), colv, colv, vec(SSM_D_INNER), vec(SSM_D_INNER),
            pl.BlockSpec((q, q), lambda b, c: (0, 0)),
        ],
        out_specs=[pl.BlockSpec((q, SSM_D_INNER), lambda b, c: (b * nc + c, 0)),
                   pl.BlockSpec((1,) + st_shape, lambda b, c: (b, 0, 0, 0))],
        out_shape=[jax.ShapeDtypeStruct((bsz * seq, SSM_D_INNER), F32),
                   jax.ShapeDtypeStruct((bsz,) + st_shape, F32)],
        scratch_shapes=[pltpu.VMEM(st_shape, F32)],
        compiler_params=pltpu.CompilerParams(
            dimension_semantics=("parallel", "arbitrary"), vmem_limit_bytes=VMEM_LIMIT_BYTES),
        name="ssd_mix",
    )(main, main, main, main, main, conv_halo, tail, h0_t, pw['conv_w'], pw['conv_b'].reshape(1, -1),
      dtb.reshape(1, LANES), alog.reshape(1, LANES), dtb.reshape(LANES, 1), alog.reshape(LANES, 1),
      jnp.repeat(pw['d_skip'], SSM_HEAD_DIM).reshape(1, -1), pw['ssm_g'].reshape(1, -1), jnp.asarray(tril, BF16))
    return y, ht


POOL_HALO = 16


def _pool_body(u_ref, halo_ref, buf_ref, w_ref, sc_ref, o_ref, *, pos0):
    i = pl.program_id(1)
    tl = u_ref.shape[0]
    u = u_ref[...]
    prev = jnp.where(i == 0, buf_ref[0], halo_ref[...])
    x = jnp.concatenate([prev, u], axis=0)
    pos = pos0 + i * tl + lax.broadcasted_iota(jnp.int32, (tl, 1), 0)
    for g, w in enumerate(POOL_WINDOWS):
        cols = slice(g * POOL_GROUP_DIM, (g + 1) * POOL_GROUP_DIM)
        s = x[:, cols]
        span = 1
        while span < w:
            s = s[span:, :] + s[:-span, :]
            span *= 2
        win = s[POOL_HALO - (w - 1):, :]
        cnt = jnp.minimum(pos + 1, w).astype(F32)
        diff = win / cnt - u[:, cols]
        y = _dot(diff.astype(BF16), w_ref[g])
        o_ref[:, cols] = y * sc_ref[:, cols]


def pool_mix_pallas(h, buf16, pool_w, pool_scale, bsz, seq, pos0, *, tl=512):
    tl = min(tl, seq)
    nt = seq // tl
    per = max(tl // POOL_HALO, 1)
    if tl < POOL_HALO:
        assert nt == 1
        halo_src = buf16.reshape(bsz * POOL_HALO, POOL_DIM)
        halo_map = lambda b, i: (b, 0)
    else:
        halo_src = h
        halo_map = lambda b, i: (jnp.maximum((b * nt + i) * per - 1, 0), 0)
    return pl.pallas_call(
        functools.partial(_pool_body, pos0=pos0),
        grid=(bsz, nt),
        in_specs=[
            pl.BlockSpec((tl, POOL_DIM), lambda b, i: (b * nt + i, 0)),
            pl.BlockSpec((POOL_HALO, POOL_DIM), halo_map),
            pl.BlockSpec((1, POOL_HALO, POOL_DIM), lambda b, i: (b, 0, 0)),
            pl.BlockSpec((POOL_GROUPS, POOL_GROUP_DIM, POOL_GROUP_DIM), lambda b, i: (0, 0, 0)),
            pl.BlockSpec((1, POOL_DIM), lambda b, i: (0, 0)),
        ],
        out_specs=pl.BlockSpec((tl, POOL_DIM), lambda b, i: (b * nt + i, 0)),
        out_shape=jax.ShapeDtypeStruct((bsz * seq, POOL_DIM), F32),
        compiler_params=pltpu.CompilerParams(
            dimension_semantics=("parallel", "arbitrary"), vmem_limit_bytes=VMEM_LIMIT_BYTES),
        name="pool_mix",
    )(h, halo_src, buf16, pool_w.astype(BF16), pool_scale.reshape(1, POOL_DIM))


def _sb_decode_body(pt_ref, qbd_ref, new_ref, page_ref, tri_ref, o_ref, run_ref, acc_ref, *, n_tok):
    p = pl.program_id(1)
    rows = SB_HEADS * n_tok
    tri = tri_ref[...]
    qbd = qbd_ref[0]

    def slabs(ref, part):
        return jnp.concatenate([ref[0, :, part * SB_HEADS + h, :] for h in range(SB_HEADS)], axis=1).astype(BF16)

    def page_update(k_all, v_all, masked):
        z = _dot_nt(qbd, k_all)
        sp = _softplus(z)
        if masked:
            tok = lax.broadcasted_iota(jnp.int32, (rows, PAGE_SIZE), 0) % n_tok
            key = lax.broadcasted_iota(jnp.int32, (rows, PAGE_SIZE), 1)
            strict = key < tok
            go = jnp.where(strict, sp, 0.0)
        else:
            go = sp
        hi = go.astype(BF16)
        lo = (go - hi.astype(F32)).astype(BF16)
        ct = _dot(hi, tri) + _dot(lo, tri)
        run = run_ref[...]
        att = jnp.exp((z - sp) - ((ct[:, :PAGE_SIZE] - go) + run))
        if masked:
            att = jnp.where(strict, att, 0.0)
        run_ref[...] = run + ct[:, PAGE_SIZE:]
        acc_ref[...] += _dot(att.astype(BF16), v_all)

    @pl.when(p == 0)
    def _():
        run_ref[...] = jnp.zeros_like(run_ref)
        acc_ref[...] = jnp.zeros_like(acc_ref)
        page_update(new_ref[0, :, :SB_DIM].astype(BF16), new_ref[0, :, SB_DIM:].astype(BF16), True)

    @pl.when(jnp.min(run_ref[...]) <= SB_DEAD_RUN)
    def _():
        page_update(slabs(page_ref, 0), slabs(page_ref, 1), False)

    @pl.when(p == pl.num_programs(1) - 1)
    def _():
        for h in range(SB_HEADS):
            o_ref[0, :, h * SB_HEAD_DIM:(h + 1) * SB_HEAD_DIM] = (
                acc_ref[h * n_tok:(h + 1) * n_tok, h * SB_HEAD_DIM:(h + 1) * SB_HEAD_DIM])


def sb_decode(h, cache, page_table, bsz, n_tok):
    n_pages = page_table.shape[1]
    rows = SB_HEADS * n_tok
    assert rows == PAGE_SIZE and n_tok <= PAGE_SIZE
    h3 = h.reshape(bsz, n_tok, -1)
    q = h3[..., POOL_DIM:POOL_DIM + SB_DIM].reshape(bsz, n_tok, SB_HEADS, SB_HEAD_DIM) * (SB_HEAD_DIM ** -0.5)
    eye = jnp.eye(SB_HEADS, dtype=F32)
    qbd = jnp.einsum('bthd,hg->bhtgd', q, eye).reshape(bsz, rows, SB_DIM).astype(BF16)
    new_kv = jnp.pad(h3[..., POOL_DIM + SB_DIM:], ((0, 0), (0, PAGE_SIZE - n_tok), (0, 0)))
    jj = np.arange(PAGE_SIZE)
    tri = np.concatenate([(jj[:, None] >= jj[None, :]).astype(np.float32),
                          np.ones((PAGE_SIZE, PAGE_SIZE), np.float32)], axis=1)
    return pl.pallas_call(
        functools.partial(_sb_decode_body, n_tok=n_tok),
        grid_spec=pltpu.PrefetchScalarGridSpec(
            num_scalar_prefetch=1,
            grid=(bsz, n_pages),
            in_specs=[
                pl.BlockSpec((1, rows, SB_DIM), lambda b, p, pt: (b, 0, 0)),
                pl.BlockSpec((1, PAGE_SIZE, 2 * SB_DIM), lambda b, p, pt: (b, 0, 0)),
                pl.BlockSpec((1, PAGE_SIZE, 2 * SB_HEADS, SB_HEAD_DIM),
                             lambda b, p, pt: (pt[b, n_pages - 1 - p], 0, 0, 0)),
                pl.BlockSpec((PAGE_SIZE, 2 * PAGE_SIZE), lambda b, p, pt: (0, 0)),
            ],
            out_specs=pl.BlockSpec((1, n_tok, SB_DIM), lambda b, p, pt: (b, 0, 0)),
            scratch_shapes=[pltpu.VMEM((rows, PAGE_SIZE), F32), pltpu.VMEM((rows, SB_DIM), F32)],
        ),
        out_shape=jax.ShapeDtypeStruct((bsz, n_tok, SB_DIM), F32),
        compiler_params=pltpu.CompilerParams(
            dimension_semantics=("parallel", "arbitrary"), vmem_limit_bytes=VMEM_LIMIT_BYTES),
        name="sb_decode",
    )(page_table, qbd, new_kv, cache, jnp.asarray(tri, BF16))


def _rel_bucket_np(dist):
    n = np.maximum(dist, 0)
    exact = REL_BUCKETS // 2
    nf = np.maximum(n, 1).astype(np.float32)
    log_b = exact + (np.log(nf / np.float32(exact)) / np.float32(math.log(REL_MAX_DIST / exact))
                     * np.float32(REL_BUCKETS - exact)).astype(np.int32)
    return np.where(n < exact, n, np.minimum(log_b, REL_BUCKETS - 1)).astype(np.int32)


def _bias_select(rows, bucket):
    ids = jnp.asarray(bucket)
    out = jnp.broadcast_to(rows[REL_BUCKETS - 1], bucket.shape)
    for k in range(REL_BUCKETS - 2, -1, -1):
        out = jnp.where(ids == k, rows[k], out)
    return out


def nsa_bias_tables(rel_bias, seq):
    s = np.arange(Q_BLOCK)
    rows = jnp.repeat(rel_bias.reshape(REL_BUCKETS, NSA_GROUPS, NSA_HPG).transpose(1, 0, 2), Q_BLOCK, axis=-1)
    far_from = (CMP_BAND // 2 + 1) * CMP_STRIDE - (CMP_BLOCK - 1)
    assert (_rel_bucket_np(np.arange(far_from, 2 * seq)) == REL_BUCKETS - 1).all()
    near_idx = _rel_bucket_np(np.arange(2)[:, None, None] * Q_BLOCK + s[None, None, :] - s[None, :, None])
    near_idx = np.tile(near_idx, (1, 1, NSA_HPG))
    near = jnp.stack([_bias_select(rows[g], near_idx) for g in range(NSA_GROUPS)])
    far = rows[:, REL_BUCKETS - 1:, :]
    m_rel = np.arange(-CMP_BAND // 2, CMP_BAND // 2).reshape(CMP_BAND, 1)
    dist_c = s.reshape(1, Q_BLOCK) - (m_rel * CMP_STRIDE + CMP_BLOCK - 1)
    cb_idx = np.tile(_rel_bucket_np(dist_c), (1, NSA_HPG))
    cb_ok = jnp.asarray(np.tile(dist_c >= 0, (1, NSA_HPG)))
    cb = jnp.stack([jnp.where(cb_ok, _bias_select(rows[g], cb_idx), NEG_INF) for g in range(NSA_GROUPS)])
    return near, far, cb


def _nsa_prompt_body(q_ref, tail_ref, ks_ref, vs_ref, kw_ref, vw_ref, kcmp_ref, vcmpt_ref, cb_ref, nb_ref, fb_ref,
                     at_ref, o_ref, vst_ref, vwt_ref, sel_ref, gt_ref, cbs_ref, kwp_ref, *, seq):
    g = pl.program_id(1)
    i = pl.program_id(2)
    blk = Q_BLOCK
    hw = NSA_HPG * blk
    n_slc = seq // SLC_BLOCK

    @pl.when(i == 0)
    def _():
        kwp_ref[0:WINDOW, :] = jnp.zeros((WINDOW, NSA_HEAD_DIM), BF16)
        vwt_ref[:, 0:WINDOW] = jnp.zeros((NSA_HEAD_DIM, WINDOW), BF16)
        for c in range(seq // blk):
            vst_ref[:, c * blk:(c + 1) * blk] = vs_ref[c * blk:(c + 1) * blk, :].T.astype(BF16)
            kwp_ref[WINDOW + c * blk:WINDOW + (c + 1) * blk, :] = kw_ref[c * blk:(c + 1) * blk, :].astype(BF16)
            vwt_ref[:, WINDOW + c * blk:WINDOW + (c + 1) * blk] = vw_ref[c * blk:(c + 1) * blk, :].T.astype(BF16)

    q4 = jnp.concatenate([q_ref[:, h * blk:(h + 1) * blk] for h in range(NSA_HPG)], axis=0).astype(BF16)

    n_pad = seq // CMP_STRIDE
    half = CMP_BAND // 2
    rows = lax.broadcasted_iota(jnp.int32, (n_pad + half, hw), 0)
    cbs_ref[...] = jnp.where(rows < half * i, jnp.broadcast_to(fb_ref[0], (n_pad + half, hw)), NEG_INF)
    cbs_ref[pl.ds(pl.multiple_of(half * i, half), CMP_BAND), :] = cb_ref[0]
    cb = cbs_ref[half:, :]
    sc = _dot_nt(kcmp_ref[0, 0].astype(BF16), q4) + cb
    mc = jnp.max(sc, axis=0, keepdims=True)
    pc = jnp.exp(sc - mc)
    pc = pc / jnp.sum(pc, axis=0, keepdims=True)
    pc = pc * jnp.where(cb > 0.5 * NEG_INF, 1.0, 0.0)
    o_c = _dot(vcmpt_ref[0, 0].astype(BF16), pc.astype(BF16))
    psum = pc[:, 0:blk]
    for h in range(1, NSA_HPG):
        psum = psum + pc[:, h * blk:(h + 1) * blk]
    p1 = psum.astype(BF16)
    r1 = psum - p1.astype(F32)
    p2 = r1.astype(BF16)
    p3 = (r1 - p2.astype(F32)).astype(BF16)
    at = at_ref[...]
    imp = (_dot(at, p1) + _dot(at, p2)) + _dot(at, p3)

    jdx = lax.broadcasted_iota(jnp.int32, (n_slc, blk), 0)
    tpos = i * blk + lax.broadcasted_iota(jnp.int32, (n_slc, blk), 1)
    cur = tpos // SLC_BLOCK
    forced = jnp.where(jdx == 0, 1.0, 0.0) + jnp.where(jdx == cur, 1.0, 0.0) + jnp.where(jdx == cur - 1, 1.0, 0.0)
    score = jnp.where(forced > 0.0, FORCE_SCORE, jnp.where(jdx <= cur, imp, -1.0))
    rank = jnp.zeros((n_slc, blk), F32)
    for r in range(n_slc):
        row = jnp.broadcast_to(score[r:r + 1, :], (n_slc, blk))
        gt = jnp.where(row > score, 1.0, 0.0)
        ge = jnp.where(row >= score, 1.0, 0.0)
        rank = rank + jnp.where(jdx > r, ge, gt)
    sel_ref[...] = jnp.where(rank < float(min(SLC_TOPN, n_slc)), 1.0, 0.0)

    ss = lax.broadcasted_iota(jnp.int32, (blk, blk), 0)
    tt = lax.broadcasted_iota(jnp.int32, (blk, blk), 1)
    causal = jnp.where(ss <= tt, 1.0, 0.0)
    anti = jnp.where(ss >= tt, 1.0, 0.0)
    bias_far = fb_ref[0]

    def tile4(mk):
        return jnp.concatenate([mk] * NSA_HPG, axis=1)

    def scores(k_ref, kb, bias):
        r0 = pl.multiple_of(kb * blk, blk)
        return _dot_nt(k_ref[pl.ds(r0, blk), :].astype(BF16), q4) + bias

    def pv(vt_ref, kb, p):
        r0 = pl.multiple_of(kb * blk, blk)
        return _dot(vt_ref[:, pl.ds(r0, blk)], p.astype(BF16))

    def first(s, mk):
        s = jnp.where(tile4(mk) > 0.0, s, NEG_INF)
        m = jnp.max(s, axis=0, keepdims=True)
        p = jnp.exp(s - m)
        return m, jnp.sum(p, axis=0, keepdims=True), p

    def update(carry, s, mk, vt_ref, kb):
        m, l, acc = carry
        s = jnp.where(tile4(mk) > 0.0, s, NEG_INF)
        m_new = jnp.maximum(m, jnp.max(s, axis=0, keepdims=True))
        alpha = jnp.exp(m - m_new)
        p = jnp.exp(s - m_new)
        return m_new, alpha * l + jnp.sum(p, axis=0, keepdims=True), alpha * acc + pv(vt_ref, kb, p)

    def sel_mask(kb):
        r0 = sel_ref[pl.ds(2 * kb, 1), :]
        r1 = sel_ref[pl.ds(2 * kb + 1, 1), :]
        half = blk // 2
        return jnp.concatenate([jnp.broadcast_to(r0, (half, blk)), jnp.broadcast_to(r1, (half, blk))], axis=0)

    m, l, p = first(scores(ks_ref, i, nb_ref[0, 0]), sel_mask(i) * causal)
    carry = (m, l, pv(vst_ref, i, p))
    kb1 = jnp.maximum(i - 1, 0)
    ok1 = jnp.where(i >= 1, 1.0, 0.0)
    carry = update(carry, scores(ks_ref, kb1, nb_ref[0, 1]), sel_mask(kb1) * ok1, vst_ref, kb1)

    def far_body(r, c):
        m, l, acc = c
        r0 = pl.multiple_of(r * 2 * blk, 2 * blk)
        s = _dot_nt(ks_ref[pl.ds(r0, 2 * blk), :].astype(BF16), q4) + bias_far
        ok2 = jnp.where(2 * r + 1 <= i - 2, 1.0, 0.0)
        mk = jnp.concatenate([sel_mask(2 * r), sel_mask(2 * r + 1) * ok2], axis=0)
        s = jnp.where(tile4(mk) > 0.0, s, NEG_INF)
        m_new = jnp.maximum(m, jnp.max(s, axis=0, keepdims=True))
        alpha = jnp.exp(m - m_new)
        p = jnp.exp(s - m_new)
        pvv = _dot(vst_ref[:, pl.ds(r0, 2 * blk)], p.astype(BF16))
        return m_new, alpha * l + jnp.sum(p, axis=0, keepdims=True), alpha * acc + pvv

    m, l, acc = lax.fori_loop(0, i // 2, far_body, carry)
    o_s = acc / l

    nwb = WINDOW // blk
    w0 = pl.multiple_of(i * blk, blk)
    ones = jnp.ones((blk, blk), F32)
    bias_w = jnp.concatenate([jnp.broadcast_to(bias_far, ((nwb - 1) * blk, hw)), nb_ref[0, 1], nb_ref[0, 0]], axis=0)
    mask_w = jnp.concatenate(
        [(anti if d == nwb else ones) * jnp.where(i >= d, 1.0, 0.0) for d in range(nwb, 0, -1)] + [causal], axis=0)
    sw = _dot_nt(kwp_ref[pl.ds(w0, WINDOW + blk), :], q4) + bias_w
    sw = jnp.where(tile4(mask_w) > 0.0, sw, NEG_INF)
    pw = jnp.exp(sw - jnp.max(sw, axis=0, keepdims=True))
    o_w = _dot(vwt_ref[:, pl.ds(w0, WINDOW + blk)], pw.astype(BF16)) / jnp.sum(pw, axis=0, keepdims=True)

    gt_ref[...] = tail_ref[...].T

    def gate(c):
        rows = [gt_ref[pl.ds(SSM_HEADS + 3 * (NSA_HPG * g + h) + c, 1), :] for h in range(NSA_HPG)]
        return jax.nn.sigmoid(jnp.concatenate(rows, axis=1))

    o_t = (gate(0) * o_c + gate(1) * o_s) + gate(2) * o_w
    for h in range(NSA_HPG):
        o_ref[:, h * blk:(h + 1) * blk] = o_t[:, h * blk:(h + 1) * blk].T


def nsa_prompt(main, tail, kcmp, vcmpt, tables, bsz, seq):
    near, far, cb = tables
    qb = seq // Q_BLOCK
    n_pad = seq // CMP_STRIDE
    n_slc = seq // SLC_BLOCK
    hw = NSA_HPG * Q_BLOCK
    ratio = SLC_BLOCK // CMP_STRIDE
    lo = CMP_BLOCK // CMP_STRIDE - 1
    jj = np.arange(n_slc)[:, None]
    nn = np.arange(n_pad)[None, :]
    a_t = ((nn >= ratio * jj - lo) & (nn <= ratio * jj + ratio - 1) & (nn < n_pad - 1)).astype(np.float32)
    dh = NSA_HEAD_DIM
    col = lambda off: off // dh
    return pl.pallas_call(
        functools.partial(_nsa_prompt_body, seq=seq),
        grid=(bsz, NSA_GROUPS, qb),
        in_specs=[
            pl.BlockSpec((Q_BLOCK, hw), lambda b, g, i: (b * qb + i, OFF_Q // hw + g)),
            pl.BlockSpec((Q_BLOCK, EVEN_TAIL), lambda b, g, i: (b * qb + i, 0)),
            pl.BlockSpec((seq, dh), lambda b, g, i: (b, col(OFF_KV + 2 * NSA_KV) + g)),
            pl.BlockSpec((seq, dh), lambda b, g, i: (b, col(OFF_KV + 3 * NSA_KV) + g)),
            pl.BlockSpec((seq, dh), lambda b, g, i: (b, col(OFF_WIN) + g)),
            pl.BlockSpec((seq, dh), lambda b, g, i: (b, col(OFF_WIN + NSA_KV) + g)),
            pl.BlockSpec((1, 1, n_pad, dh), lambda b, g, i: (b, g, 0, 0)),
            pl.BlockSpec((1, 1, dh, n_pad), lambda b, g, i: (b, g, 0, 0)),
            pl.BlockSpec((1, CMP_BAND, hw), lambda b, g, i: (g, 0, 0)),
            pl.BlockSpec((1, 2, Q_BLOCK, hw), lambda b, g, i: (g, 0, 0, 0)),
            pl.BlockSpec((1, 1, hw), lambda b, g, i: (g, 0, 0)),
            pl.BlockSpec((n_slc, n_pad), lambda b, g, i: (0, 0)),
        ],
        out_specs=pl.BlockSpec((Q_BLOCK, hw), lambda b, g, i: (b * qb + i, g)),
        out_shape=jax.ShapeDtypeStruct((bsz * seq, NSA_Q), F32),
        scratch_shapes=[pltpu.VMEM((dh, seq), BF16), pltpu.VMEM((dh, seq + WINDOW), BF16),
                        pltpu.VMEM((n_slc, Q_BLOCK), F32), pltpu.VMEM((EVEN_TAIL, Q_BLOCK), F32),
                        pltpu.VMEM((n_pad + CMP_BAND // 2, hw), F32), pltpu.VMEM((seq + WINDOW, dh), BF16)],
        compiler_params=pltpu.CompilerParams(
            dimension_semantics=("parallel", "parallel", "arbitrary"), vmem_limit_bytes=VMEM_LIMIT_BYTES),
        name="nsa_prompt",
    )(main, tail, main, main, main, main, kcmp, vcmpt, cb, near, far, jnp.asarray(a_t, BF16))


CMP_COLS = 2 * NSA_KV
SUBS_PER_PAGE = PAGE_SIZE // CMP_STRIDE


def _cmp_pool_body(idx_ref, src_ref, a1_ref, a2_ref, o_ref, prev_ref):
    p = pl.program_id(1)
    x = src_ref[0]
    first = (x * a1_ref[...]).reshape(SUBS_PER_PAGE, CMP_STRIDE, CMP_COLS).sum(axis=1)
    second = (x * a2_ref[...]).reshape(SUBS_PER_PAGE, CMP_STRIDE, CMP_COLS).sum(axis=1)
    prev = jnp.where(p == 0, 0.0, prev_ref[...])
    shifted = jnp.concatenate([prev[SUBS_PER_PAGE - 1:, :], first[:SUBS_PER_PAGE - 1, :]], axis=0)
    o_ref[0] = shifted + second
    prev_ref[...] = first


CMP_PAGES_PER_STEP = 4


def _cmp_pool_slab_body(idx_ref, *refs):
    src_refs = refs[:CMP_PAGES_PER_STEP]
    a1_ref, a2_ref, o_ref, prev_ref = refs[CMP_PAGES_PER_STEP:]
    p = pl.program_id(1)
    prev = jnp.where(p == 0, 0.0, prev_ref[...])
    for j, src_ref in enumerate(src_refs):
        x = src_ref[0].reshape(SUBS_PER_PAGE, CMP_STRIDE, 2 * NSA_GROUPS, NSA_HEAD_DIM)
        first = (x * a1_ref[...][None]).sum(axis=1)
        second = (x * a2_ref[...][None]).sum(axis=1)
        shifted = jnp.concatenate([prev[SUBS_PER_PAGE - 1:], first[:SUBS_PER_PAGE - 1]], axis=0)
        o_ref[0, j * SUBS_PER_PAGE:(j + 1) * SUBS_PER_PAGE] = shifted + second
        prev = first
    prev_ref[...] = prev


def cmp_pool_pages(cache, page_idx, cmp_alpha):
    bsz, n = page_idx.shape
    pps = CMP_PAGES_PER_STEP
    assert n % pps == 0
    slabs = 2 * NSA_GROUPS
    dh = NSA_HEAD_DIM

    def tiled(half):
        return jnp.concatenate([jnp.repeat(cmp_alpha[0][half][:, None, :], NSA_GROUPS, axis=1),
                                jnp.repeat(cmp_alpha[1][half][:, None, :], NSA_GROUPS, axis=1)], axis=1)

    a1 = tiled(slice(0, CMP_STRIDE))
    a2 = tiled(slice(CMP_STRIDE, CMP_BLOCK))

    def page_spec(j):
        return pl.BlockSpec((1, PAGE_SIZE, slabs, dh), lambda b, p, idx: (idx[b, p * pps + j], 0, 0, 0))

    return pl.pallas_call(
        _cmp_pool_slab_body,
        grid_spec=pltpu.PrefetchScalarGridSpec(
            num_scalar_prefetch=1,
            grid=(bsz, n // pps),
            in_specs=[page_spec(j) for j in range(pps)] + [
                pl.BlockSpec((CMP_STRIDE, slabs, dh), lambda b, p, idx: (0, 0, 0)),
                pl.BlockSpec((CMP_STRIDE, slabs, dh), lambda b, p, idx: (0, 0, 0)),
            ],
            out_specs=pl.BlockSpec((1, pps * SUBS_PER_PAGE, slabs, dh), lambda b, p, idx: (b, p, 0, 0)),
            scratch_shapes=[pltpu.VMEM((SUBS_PER_PAGE, slabs, dh), F32)],
        ),
        out_shape=jax.ShapeDtypeStruct((bsz, n * SUBS_PER_PAGE, slabs, dh), F32),
        compiler_params=pltpu.CompilerParams(
            dimension_semantics=("parallel", "arbitrary"), vmem_limit_bytes=VMEM_LIMIT_BYTES),
        name="cmp_pool_pages",
    )(page_idx, *([cache] * pps), a1, a2)


def cmp_pool(src, block_idx, col_block, cmp_alpha):
    bsz, n = block_idx.shape

    def tiled(half):
        a = jnp.concatenate([jnp.tile(cmp_alpha[0][half], (1, NSA_GROUPS)),
                             jnp.tile(cmp_alpha[1][half], (1, NSA_GROUPS))], axis=1)
        return jnp.tile(a, (SUBS_PER_PAGE, 1))

    a1 = tiled(slice(0, CMP_STRIDE))
    a2 = tiled(slice(CMP_STRIDE, CMP_BLOCK))
    return pl.pallas_call(
        _cmp_pool_body,
        grid_spec=pltpu.PrefetchScalarGridSpec(
            num_scalar_prefetch=1,
            grid=(bsz, n),
            in_specs=[
                pl.BlockSpec((1, PAGE_SIZE, CMP_COLS), lambda b, p, idx: (idx[b, p], 0, col_block)),
                pl.BlockSpec((PAGE_SIZE, CMP_COLS), lambda b, p, idx: (0, 0)),
                pl.BlockSpec((PAGE_SIZE, CMP_COLS), lambda b, p, idx: (0, 0)),
            ],
            out_specs=pl.BlockSpec((1, SUBS_PER_PAGE, CMP_COLS), lambda b, p, idx: (b, p, 0)),
            scratch_shapes=[pltpu.VMEM((SUBS_PER_PAGE, CMP_COLS), F32)],
        ),
        out_shape=jax.ShapeDtypeStruct((bsz, n * SUBS_PER_PAGE, CMP_COLS), F32),
        compiler_params=pltpu.CompilerParams(
            dimension_semantics=("parallel", "arbitrary"), vmem_limit_bytes=VMEM_LIMIT_BYTES),
        name="cmp_pool",
    )(block_idx, src, a1, a2)


def _cmp_project_body(p_ref, wk_ref, wv_ref, gk_ref, k_ref, vt_ref):
    n = p_ref.shape[1]
    dh = NSA_HEAD_DIM
    zero = jnp.zeros((1, dh), F32)
    pk = jnp.concatenate([p_ref[0, 1:, 0:dh], zero], axis=0)
    pv = jnp.concatenate([p_ref[0, 1:, dh:2 * dh], zero], axis=0)
    kp = _dot(pk.astype(BF16), wk_ref[...])
    k_ref[0, 0] = (kp * lax.rsqrt(jnp.mean(kp * kp, axis=-1, keepdims=True) + EPS)) * gk_ref[...]
    vp = _dot(pv.astype(BF16), wv_ref[...])
    for c in range(n // LANES):
        vt_ref[0, 0, :, c * LANES:(c + 1) * LANES] = vp[c * LANES:(c + 1) * LANES, :].T


def cmp_project(pooled, cmp_w, g_kcmp):
    bsz, n, _ = pooled.shape
    dh = NSA_HEAD_DIM
    pg = pooled.reshape(bsz, n, 2, NSA_GROUPS, dh).transpose(0, 3, 1, 2, 4).reshape(bsz * NSA_GROUPS, n, 2 * dh)
    k, vt = pl.pallas_call(
        _cmp_project_body,
        grid=(bsz, NSA_GROUPS),
        in_specs=[
            pl.BlockSpec((1, n, 2 * dh), lambda b, g: (b * NSA_GROUPS + g, 0, 0)),
            pl.BlockSpec((dh, dh), lambda b, g: (0, 0)),
            pl.BlockSpec((dh, dh), lambda b, g: (0, 0)),
            pl.BlockSpec((1, dh), lambda b, g: (0, 0)),
        ],
        out_specs=[pl.BlockSpec((1, 1, n, dh), lambda b, g: (b, g, 0, 0)),
                   pl.BlockSpec((1, 1, dh, n), lambda b, g: (b, g, 0, 0))],
        out_shape=[jax.ShapeDtypeStruct((bsz, NSA_GROUPS, n, dh), F32),
                   jax.ShapeDtypeStruct((bsz, NSA_GROUPS, dh, n), F32)],
        compiler_params=pltpu.CompilerParams(
            dimension_semantics=("parallel", "parallel"), vmem_limit_bytes=VMEM_LIMIT_BYTES),
        name="cmp_project",
    )(pg, cmp_w[0].astype(BF16), cmp_w[1].astype(BF16), g_kcmp.reshape(1, dh))
    return k, vt


def _nsa_decode_body(pt_ref, qbd_ref, pool_ref, wk_ref, wv_ref, gk_ref, cbias_ref, at_ref, gsum_ref, gexp_ref,
                     new_ref, nbias_ref, page_ref, lbias_ref, far_ref, win_ref, wbias_ref, gate_ref, o_ref,
                     kc_ref, vct_ref, oc_ref, score_ref, sel_ref, m_ref, l_ref, acc_ref, *, n_tok, pos0):
    s = pl.program_id(1)
    n_pages = pl.num_programs(1) - 1
    dh = NSA_HEAD_DIM
    gd = NSA_GROUPS * dh
    lanes = NSA_HEADS * n_tok
    n_cmp = pool_ref.shape[1]
    n_rows = score_ref.shape[0]
    qbd = qbd_ref[0]

    def kv_t(v):
        return jnp.concatenate([v[:, g * dh:(g + 1) * dh].T for g in range(NSA_GROUPS)], axis=0).astype(BF16)

    def masked_update(sc, mask_rows, v, init):
        sc = jnp.where(mask_rows > 0.0, sc, NEG_INF)
        m_old = jnp.full((1, lanes), NEG_INF, F32) if init else m_ref[...]
        m_new = jnp.maximum(m_old, jnp.max(sc, axis=0, keepdims=True))
        p = jnp.exp(sc - m_new)
        pv = _dot(kv_t(v), p.astype(BF16))
        if init:
            l_ref[...] = jnp.sum(p, axis=0, keepdims=True)
            acc_ref[...] = pv
        else:
            alpha = jnp.exp(m_old - m_new)
            l_ref[...] = alpha * l_ref[...] + jnp.sum(p, axis=0, keepdims=True)
            acc_ref[...] = alpha * acc_ref[...] + pv
        m_ref[...] = m_new

    @pl.when(s == 0)
    def _():
        for g in range(NSA_GROUPS):
            kp = _dot(pool_ref[0, :, g, :].astype(BF16), wk_ref[...])
            kp = (kp * lax.rsqrt(jnp.mean(kp * kp, axis=-1, keepdims=True) + EPS)) * gk_ref[...]
            kc_ref[:, g * dh:(g + 1) * dh] = kp.astype(BF16)
            vp = _dot(pool_ref[0, :, NSA_GROUPS + g, :].astype(BF16), wv_ref[...])
            for c in range(n_cmp // LANES):
                vct_ref[g * dh:(g + 1) * dh, c * LANES:(c + 1) * LANES] = vp[c * LANES:(c + 1) * LANES, :].T.astype(BF16)
        cb = cbias_ref[...]
        sc = _dot(kc_ref[...], qbd) + cb
        pc = jnp.exp(sc - jnp.max(sc, axis=0, keepdims=True))
        pc = pc / jnp.sum(pc, axis=0, keepdims=True)
        pc = pc * jnp.where(cb > 0.5 * NEG_INF, 1.0, 0.0)
        oc_ref[...] = _dot(vct_ref[...], pc.astype(BF16))
        at = at_ref[...]
        p1, p2, p3 = _split3(pc)
        u = (_dot(at, p1) + _dot(at, p2)) + _dot(at, p3)
        gs = gsum_ref[...]
        u1, u2, u3 = _split3(u)
        imp = (_dot(u1, gs) + _dot(u2, gs)) + _dot(u3, gs)
        jdx = lax.broadcasted_iota(jnp.int32, (n_rows, lanes), 0)
        tok = lax.broadcasted_iota(jnp.int32, (n_rows, lanes), 1) % n_tok
        cur = (pos0 + tok) // SLC_BLOCK
        n_blocks = (pos0 + n_tok + SLC_BLOCK - 1) // SLC_BLOCK
        forced = (jnp.where(jdx == 0, 1.0, 0.0) + jnp.where(jdx == cur, 1.0, 0.0)
                  + jnp.where(jdx == cur - 1, 1.0, 0.0))
        score = jnp.where(forced > 0.0, FORCE_SCORE, jnp.where(jdx <= cur, imp, -1.0))
        score = jnp.where(jdx < n_blocks, score, -2.0)
        score_ref[...] = score

        def rank_body(r, rank):
            row = jnp.broadcast_to(score_ref[pl.ds(r, 1), :], (n_rows, lanes))
            gt = jnp.where(row > score, 1.0, 0.0)
            ge = jnp.where(row >= score, 1.0, 0.0)
            return rank + jnp.where(jdx > r, ge, gt)

        rank = lax.fori_loop(0, n_blocks, rank_body, jnp.zeros((n_rows, lanes), F32))
        sel = jnp.where(rank < float(SLC_TOPN), 1.0, 0.0).astype(BF16)
        sel_ref[...] = _dot(sel, gexp_ref[...])
        new = new_ref[0]
        sc = _dot(new[:, :gd].astype(BF16), qbd) + nbias_ref[...]
        rows = jnp.broadcast_to(sel_ref[pl.ds(n_blocks - 1, 1), :], (PAGE_SIZE, lanes))
        masked_update(sc, rows, new[:, gd:], True)

    @pl.when(s > 0)
    def _():
        p = s - 1
        k_pg = jnp.concatenate([page_ref[0, :, g, :] for g in range(NSA_GROUPS)], axis=1)
        v_pg = jnp.concatenate([page_ref[0, :, NSA_GROUPS + g, :] for g in range(NSA_GROUPS)], axis=1)
        bias = jnp.where(p == n_pages - 1, lbias_ref[...], jnp.broadcast_to(far_ref[...], (PAGE_SIZE, lanes)))
        sc = _dot(k_pg.astype(BF16), qbd) + bias
        half = PAGE_SIZE // 2
        rows = jnp.concatenate([jnp.broadcast_to(sel_ref[pl.ds(2 * p, 1), :], (half, lanes)),
                                jnp.broadcast_to(sel_ref[pl.ds(2 * p + 1, 1), :], (half, lanes))], axis=0)
        masked_update(sc, rows, v_pg, False)

    @pl.when(s == n_pages)
    def _():
        o_s = acc_ref[...] / l_ref[...]
        win = win_ref[0]
        sw = _dot(win[:, :gd].astype(BF16), qbd) + wbias_ref[...]
        pw = jnp.exp(sw - jnp.max(sw, axis=0, keepdims=True))
        pw = pw / jnp.sum(pw, axis=0, keepdims=True)
        o_w = _dot(kv_t(win[:, gd:]), pw.astype(BF16))
        gates = jax.nn.sigmoid(gate_ref[0])
        o_t = (gates[0:1, :] * oc_ref[...] + gates[1:2, :] * o_s) + gates[2:3, :] * o_w
        for g in range(NSA_GROUPS):
            blk = o_t[g * dh:(g + 1) * dh, :].T
            for hh in range(NSA_HPG):
                h = g * NSA_HPG + hh
                o_ref[0, :, h * dh:(h + 1) * dh] = blk[h * n_tok:(h + 1) * n_tok, :]


def nsa_decode(main, tail, pooled, cache, page_table, win_state, pw, bsz, n_tok):
    n_pages = page_table.shape[1]
    pos0 = n_pages * PAGE_SIZE
    wb = win_state.shape[1]
    dh = NSA_HEAD_DIM
    gd = NSA_GROUPS * dh
    lanes = NSA_HEADS * n_tok
    assert lanes == LANES and pos0 % PAGE_SIZE == 0 and n_tok <= SLC_BLOCK
    n_cmp = pooled.shape[1]
    n_blocks = (pos0 + n_tok + SLC_BLOCK - 1) // SLC_BLOCK
    n_rows = -(-n_blocks // 8) * 8
    rel = pw['rel_table']
    m3 = main.reshape(bsz, n_tok, -1)
    q = m3[..., OFF_Q:OFF_KV].reshape(bsz, n_tok, NSA_HEADS, dh)
    grp = np.repeat(np.eye(NSA_GROUPS, dtype=np.float32), NSA_HPG, axis=0)
    qbd = jnp.einsum('bthd,hg->bgdht', q, jnp.asarray(grp)).reshape(bsz, gd, lanes).astype(BF16)
    t_l = np.tile(np.arange(n_tok), NSA_HEADS)[None, :]
    h_l = np.repeat(np.arange(NSA_HEADS), n_tok)

    rows = jnp.repeat(rel, n_tok, axis=1)

    def bias_table(dist, ok):
        return jnp.where(jnp.asarray(ok), _bias_select(rows, _rel_bucket_np(dist)), NEG_INF)

    r_c = np.arange(n_cmp)[:, None]
    dist_c = pos0 + t_l - ((r_c - 1) * CMP_STRIDE + CMP_BLOCK - 1)
    cbias = bias_table(dist_c, (r_c >= 1) & (dist_c >= 0))
    rr = np.arange(PAGE_SIZE)[:, None]
    nbias = bias_table(t_l - rr, (t_l - rr) >= 0)
    lbias = bias_table(PAGE_SIZE + t_l - rr, np.ones((PAGE_SIZE, lanes), bool))
    assert (_rel_bucket_np(np.arange(PAGE_SIZE + 1, 2 * PAGE_SIZE)) == REL_BUCKETS - 1).all()
    far = rows[REL_BUCKETS - 1:, :]
    n_win = -(-(wb + n_tok) // PAGE_SIZE) * PAGE_SIZE
    w_r = np.arange(n_win)[:, None]
    dist_w = pos0 + t_l - (pos0 - wb + w_r)
    wbias = bias_table(dist_w, (dist_w >= 0) & (dist_w <= WINDOW) & (pos0 - wb + w_r >= 0) & (w_r < wb + n_tok))
    ratio = SLC_BLOCK // CMP_STRIDE
    lo = CMP_BLOCK // CMP_STRIDE - 1
    jj = np.arange(n_rows)[:, None]
    nn = np.arange(n_cmp)[None, :] - 1
    a_t = ((nn >= ratio * jj - lo) & (nn <= ratio * jj + ratio - 1) & (nn >= 0) & (jj < n_blocks)).astype(np.float32)
    g_l = h_l // NSA_HPG
    gsum = np.zeros((lanes, lanes), np.float32)
    gsum[np.arange(lanes), g_l * n_tok + t_l[0]] = 1.0
    gexp = gsum.T.copy()
    new_kv = jnp.pad(m3[..., OFF_KV + 2 * NSA_KV:OFF_KV + 4 * NSA_KV], ((0, 0), (0, PAGE_SIZE - n_tok), (0, 0)))
    win_all = jnp.concatenate([win_state, m3[..., OFF_WIN:OFF_WIN + 2 * NSA_KV]], axis=1)
    win_all = jnp.pad(win_all, ((0, 0), (0, n_win - wb - n_tok), (0, 0)))
    gates = tail.reshape(bsz, n_tok, -1)[..., SSM_HEADS:SSM_HEADS + 3 * NSA_HEADS]
    gates = gates.reshape(bsz, n_tok, NSA_HEADS, 3).transpose(0, 3, 2, 1).reshape(bsz, 3, lanes)
    gates = jnp.pad(gates, ((0, 0), (0, 5), (0, 0)))
    const = lambda shape: pl.BlockSpec(shape, lambda b, s, pt: (0,) * len(shape))
    per_b = lambda shape: pl.BlockSpec((1,) + shape, lambda b, s, pt: (b,) + (0,) * len(shape))
    return pl.pallas_call(
        functools.partial(_nsa_decode_body, n_tok=n_tok, pos0=pos0),
        grid_spec=pltpu.PrefetchScalarGridSpec(
            num_scalar_prefetch=1,
            grid=(bsz, n_pages + 1),
            in_specs=[
                per_b((gd, lanes)), per_b((n_cmp, 2 * NSA_GROUPS, dh)), const((dh, dh)), const((dh, dh)),
                const((1, dh)),
                const((n_cmp, lanes)), const((n_rows, n_cmp)), const((lanes, lanes)), const((lanes, lanes)),
                per_b((PAGE_SIZE, 2 * NSA_KV)), const((PAGE_SIZE, lanes)),
                pl.BlockSpec((1, PAGE_SIZE, 2 * NSA_GROUPS, dh),
                             lambda b, s, pt: (pt[b, jnp.maximum(s - 1, 0)], 0, 1, 0)),
                const((PAGE_SIZE, lanes)), const((1, lanes)),
                per_b((n_win, 2 * NSA_KV)), const((n_win, lanes)), per_b((8, lanes)),
            ],
            out_specs=pl.BlockSpec((1, n_tok, NSA_Q), lambda b, s, pt: (b, 0, 0)),
            scratch_shapes=[
                pltpu.VMEM((n_cmp, gd), BF16), pltpu.VMEM((gd, n_cmp), BF16), pltpu.VMEM((gd, lanes), F32),
                pltpu.VMEM((n_rows, lanes), F32), pltpu.VMEM((n_rows, lanes), F32),
                pltpu.VMEM((1, lanes), F32), pltpu.VMEM((1, lanes), F32), pltpu.VMEM((gd, lanes), F32),
            ],
        ),
        out_shape=jax.ShapeDtypeStruct((bsz, n_tok, NSA_Q), F32),
        compiler_params=pltpu.CompilerParams(
            dimension_semantics=("parallel", "arbitrary"), vmem_limit_bytes=VMEM_LIMIT_BYTES),
        name="nsa_decode",
    )(page_table, qbd, pooled, pw['cmp_w'][0].astype(BF16), pw['cmp_w'][1].astype(BF16),
      pw['g_kcmp'].reshape(1, dh), cbias, jnp.asarray(a_t, BF16), jnp.asarray(gsum, BF16), jnp.asarray(gexp, BF16),
      new_kv, nbias, cache, lbias, far, win_all, wbias, gates)


def split_cols(h, sizes):
    offs = np.cumsum(sizes)[:-1].tolist()
    return jnp.split(h, offs, axis=-1)


def even_prompt(x, bsz, seq, wb, pw):
    main, tail = in_proj(x, pw['g_mix'], pw['w_main'], pw['flags'], pw['post_gain'], pw['post_scale'], pw['w_tail'])
    y_a, h_t = ssd_mix(main, tail, jnp.zeros((bsz, CONV_HALO, SSM_CONV_DIM), F32),
                       jnp.zeros((bsz, SSM_HEADS, SSM_STATE, SSM_HEAD_DIM), F32), pw, bsz, seq, seq)
    nb = seq // PAGE_SIZE
    idx = (jnp.arange(bsz, dtype=jnp.int32)[:, None] * nb + jnp.arange(nb, dtype=jnp.int32)[None, :])
    pooled = cmp_pool(main.reshape(bsz * nb, PAGE_SIZE, EVEN_MAIN), idx, OFF_KV // CMP_COLS, pw['cmp_alpha'])
    kcmp, vcmpt = cmp_project(pooled, pw['cmp_w'], pw['g_kcmp'])
    o_b = nsa_prompt(main, tail, kcmp, vcmpt, pw['nsa_tables'], bsz, seq)
    y = out_proj(x, y_a, o_b, pw['w_out_a'], pw['w_out_b'])
    m3 = main.reshape(bsz, seq, EVEN_MAIN)
    kv = m3[..., OFF_KV:OFF_WIN].reshape(bsz, seq, NSA_KV_PARTS, NSA_GROUPS, NSA_HEAD_DIM)
    new_win = m3[:, seq - wb:, OFF_WIN:].reshape(bsz, wb, 2, NSA_GROUPS, NSA_HEAD_DIM)
    new_conv = m3[:, seq - (SSM_CONV - 1):, OFF_XBC:OFF_Q]
    return y, kv, new_win, new_conv, h_t.transpose(0, 1, 3, 2)


def even_decode(x, bsz, n_tok, cache, page_idx, win_state, conv_state, ssm_state, pw):
    wb = win_state.shape[1]
    main, tail = in_proj(x, pw['g_mix'], pw['w_main'], pw['flags'], pw['post_gain'], pw['post_scale'], pw['w_tail'])
    m3 = main.reshape(bsz, n_tok, EVEN_MAIN)
    pad_rows = ((0, 0), (0, SSM_CHUNK - n_tok), (0, 0))
    main_p = jnp.pad(m3[..., :OFF_Q], pad_rows).reshape(bsz * SSM_CHUNK, OFF_Q)
    tail_p = jnp.pad(tail.reshape(bsz, n_tok, EVEN_TAIL), pad_rows).reshape(bsz * SSM_CHUNK, EVEN_TAIL)
    halo = jnp.pad(conv_state, ((0, 0), (CONV_HALO - (SSM_CONV - 1), 0), (0, 0)))
    y_a, h_t = ssd_mix(main_p, tail_p, halo, ssm_state.transpose(0, 1, 3, 2), pw, bsz, SSM_CHUNK, n_tok)
    y_a = y_a.reshape(bsz, SSM_CHUNK, SSM_D_INNER)[:, :n_tok].reshape(bsz * n_tok, SSM_D_INNER)
    pooled = cmp_pool_pages(cache, page_idx, pw['cmp_alpha'])
    win2 = win_state.reshape(bsz, wb, 2 * NSA_KV)
    o_b = nsa_decode(main, tail, pooled, cache, page_idx, win2, pw, bsz, n_tok)
    y = out_proj(x, y_a, o_b.reshape(bsz * n_tok, NSA_Q), pw['w_out_a'], pw['w_out_b'])
    kv = m3[..., OFF_KV:OFF_WIN].reshape(bsz, n_tok, NSA_KV_PARTS, NSA_GROUPS, NSA_HEAD_DIM)
    new_win = jnp.concatenate([win2, m3[..., OFF_WIN:]], axis=1)[:, -wb:]
    new_win = new_win.reshape(bsz, wb, 2, NSA_GROUPS, NSA_HEAD_DIM)
    new_conv = jnp.concatenate([conv_state, m3[..., OFF_XBC:OFF_Q]], axis=1)[:, -(SSM_CONV - 1):]
    return y, kv, new_win, new_conv, h_t.transpose(0, 1, 3, 2)


def odd_prompt(x, bsz, seq, pw):
    h = in_proj(x, pw['g_mix'], pw['w_in'], pw['flags'], pw['post_gain'], pw['post_scale'])
    y_c = pool_mix_pallas(h, jnp.zeros((bsz, POOL_HALO, POOL_DIM), F32), pw['pool_w'], pw['pool_scale'], bsz, seq, 0)
    o = sb_prompt(h, bsz, seq)
    y = out_proj(x, y_c, o, pw['w_out_a'], pw['w_out_b'])
    h3 = h.reshape(bsz, seq, -1)
    kv = h3[..., POOL_DIM + SB_DIM:].reshape(bsz, seq, 2, SB_HEADS, SB_HEAD_DIM)
    return y, kv, h3[:, seq - POOL_BUF:, :POOL_DIM]


def odd_decode(x, bsz, n_tok, cache, page_idx, pool_state, pw):
    pos0 = page_idx.shape[1] * PAGE_SIZE
    h = in_proj(x, pw['g_mix'], pw['w_in'], pw['flags'], pw['post_gain'], pw['post_scale'])
    buf16 = jnp.pad(pool_state, ((0, 0), (POOL_HALO - POOL_BUF, 0), (0, 0)))
    y_c = pool_mix_pallas(h, buf16, pw['pool_w'], pw['pool_scale'], bsz, n_tok, pos0)
    o = sb_decode(h, cache, page_idx, bsz, n_tok)
    y = out_proj(x, y_c, o.reshape(bsz * n_tok, SB_DIM), pw['w_out_a'], pw['w_out_b'])
    h3 = h.reshape(bsz, n_tok, -1)
    kv = h3[..., POOL_DIM + SB_DIM:].reshape(bsz, n_tok, 2, SB_HEADS, SB_HEAD_DIM)
    new_pool = jnp.concatenate([pool_state, h3[..., :POOL_DIM]], axis=1)[:, -POOL_BUF:]
    return y, kv, new_pool


def _even_weights(l, e, mix_norm, w_in_even, w_out_even, ssm_conv_w, ssm_conv_b, ssm_dt_bias, ssm_a_log, ssm_d,
                  ssm_norm, nsa_cmp_alpha, nsa_cmp_w, nsa_qk_gain, rel_bias, nsa_tables, tn):
    w = w_in_even[e]
    sizes = (SSM_D_INNER, SSM_CONV_DIM, SSM_HEADS, NSA_Q) + (NSA_KV,) * 6 + (3 * NSA_HEADS,)
    wz, wxbc, wdt, wq, wkc, wvc, wks, wvs, wkw, wvw, wg = split_cols(w, sizes)
    w_main = jnp.concatenate([wz, wxbc, wq, wkc, wvc, wks, wvs, wkw, wvw], axis=1).astype(BF16)
    pad = EVEN_TAIL - SSM_HEADS - 3 * NSA_HEADS
    w_tail = jnp.concatenate([wdt, wg, jnp.zeros((D_MODEL, pad), F32)], axis=1).astype(BF16)
    gain = nsa_qk_gain[e]
    ones = jnp.ones((NSA_KV,), F32)
    post_gain = jnp.concatenate([
        jnp.ones((OFF_Q,), F32), jnp.tile(gain[0], NSA_HEADS), ones, ones, jnp.tile(gain[2], NSA_GROUPS), ones,
        jnp.tile(gain[3], NSA_GROUPS), ones]).reshape(1, EVEN_MAIN)
    post_scale = jnp.concatenate([
        jnp.ones((OFF_Q,), F32), jnp.full((NSA_Q,), NSA_HEAD_DIM ** -0.5, F32),
        jnp.ones((6 * NSA_KV,), F32)]).reshape(1, EVEN_MAIN)
    flags = np.zeros((EVEN_MAIN // tn,), np.int32)
    for lo_, hi_ in ((OFF_Q, OFF_KV), (OFF_KV + 2 * NSA_KV, OFF_KV + 3 * NSA_KV), (OFF_WIN, OFF_WIN + NSA_KV)):
        assert lo_ % tn == 0 and hi_ % tn == 0
        flags[lo_ // tn:hi_ // tn] = 1
    wo = w_out_even[e].astype(BF16)
    return dict(g_mix=mix_norm[l], w_main=w_main, w_tail=w_tail, flags=jnp.asarray(flags), post_gain=post_gain,
                post_scale=post_scale, conv_w=ssm_conv_w[e], conv_b=ssm_conv_b[e], dt_bias=ssm_dt_bias[e],
                a_log=ssm_a_log[e], d_skip=ssm_d[e], ssm_g=ssm_norm[e], cmp_alpha=nsa_cmp_alpha[e],
                cmp_w=nsa_cmp_w[e], g_kcmp=gain[1], rel_table=rel_bias, nsa_tables=nsa_tables,
                w_out_a=wo[:SSM_D_INNER], w_out_b=wo[SSM_D_INNER:])


def _odd_weights(l, o, mix_norm, w_in_odd, w_out_odd, pool_w, pool_scale, tn):
    n = w_in_odd.shape[2]
    wo = w_out_odd[o].astype(BF16)
    return dict(g_mix=mix_norm[l], w_in=w_in_odd[o].astype(BF16), flags=jnp.zeros((n // tn,), jnp.int32),
                post_gain=jnp.ones((1, n), F32), post_scale=jnp.ones((1, n), F32),
                pool_w=pool_w[o], pool_scale=pool_scale[o], w_out_a=wo[:POOL_DIM], w_out_b=wo[POOL_DIM:])


def kernel(x_prompt, x_sample, cache_nsa_kv, cache_sb_kv, state_nsa_win, state_ssm, state_conv, state_pool, page_table, ffn_norm, ffn_w_gate, ffn_w_up, ffn_w_down, mix_norm, w_in_even, w_out_even, ssm_conv_w, ssm_conv_b, ssm_dt_bias, ssm_a_log, ssm_d, ssm_norm, nsa_cmp_alpha, nsa_cmp_w, nsa_qk_gain, rel_bias, w_in_odd, w_out_odd, pool_w, pool_scale):
    bp, lp, _ = x_prompt.shape
    bs, ls, _ = x_sample.shape
    wb = state_nsa_win.shape[2]
    n_phys = cache_nsa_kv.shape[1]
    tn = 512
    xp = x_prompt.reshape(bp * lp, D_MODEL)
    xs = x_sample.reshape(bs * ls, D_MODEL)
    wg_all = ffn_w_gate.astype(BF16)
    wu_all = ffn_w_up.astype(BF16)
    wd_all = ffn_w_down.astype(BF16)
    nsa_tables = nsa_bias_tables(rel_bias, lp)
    nsa_pages = cache_nsa_kv.reshape(-1, PAGE_SIZE, NSA_KV_PARTS * NSA_GROUPS, NSA_HEAD_DIM)
    sb_pages = cache_sb_kv.reshape(-1, PAGE_SIZE, 2 * SB_HEADS, SB_HEAD_DIM)
    kv_p, kv_s, sb_p, sb_s, win_p, win_s = [], [], [], [], [], []
    ssm_p, ssm_s, conv_p, conv_s, pool_p, pool_s = [], [], [], [], [], []
    for l in range(DEPTH):
        fa = (ffn_norm[l, 0], wg_all[l, 0], wu_all[l, 0], wd_all[l, 0])
        xp = ffn_half(xp, *fa)
        xs = ffn_half(xs, *fa)
        if l % 2 == 0:
            e = l // 2
            pw = _even_weights(l, e, mix_norm, w_in_even, w_out_even, ssm_conv_w, ssm_conv_b, ssm_dt_bias,
                               ssm_a_log, ssm_d, ssm_norm, nsa_cmp_alpha, nsa_cmp_w, nsa_qk_gain, rel_bias, nsa_tables, tn)
            xp, a_kv, a_win, a_conv, a_h = even_prompt(xp, bp, lp, wb, pw)
            xs, b_kv, b_win, b_conv, b_h = even_decode(xs, bs, ls, nsa_pages, page_table + e * n_phys,
                                                       state_nsa_win[e], state_conv[e], state_ssm[e], pw)
            kv_p.append(a_kv); kv_s.append(b_kv)
            win_p.append(a_win); win_s.append(b_win)
            conv_p.append(a_conv); conv_s.append(b_conv)
            ssm_p.append(a_h); ssm_s.append(b_h)
        else:
            o = l // 2
            pw = _odd_weights(l, o, mix_norm, w_in_odd, w_out_odd, pool_w, pool_scale, tn)
            xp, a_kv, a_pool = odd_prompt(xp, bp, lp, pw)
            xs, b_kv, b_pool = odd_decode(xs, bs, ls, sb_pages, page_table + o * n_phys, state_pool[o], pw)
            sb_p.append(a_kv); sb_s.append(b_kv)
            pool_p.append(a_pool); pool_s.append(b_pool)
        fb = (ffn_norm[l, 1], wg_all[l, 1], wu_all[l, 1], wd_all[l, 1])
        xp = ffn_half(xp, *fb)
        xs = ffn_half(xs, *fb)
    return (xp.reshape(bp, lp, D_MODEL), xs.reshape(bs, ls, D_MODEL), jnp.stack(kv_p), jnp.stack(kv_s),
            jnp.stack(sb_p), jnp.stack(sb_s), jnp.stack(win_p), jnp.stack(win_s), jnp.stack(ssm_p), jnp.stack(ssm_s),
            jnp.stack(conv_p), jnp.stack(conv_s), jnp.stack(pool_p), jnp.stack(pool_s))
```

```python
import functools
import math

import jax
import jax.numpy as jnp
import numpy as np
from jax import lax
from jax.experimental import pallas as pl
from jax.experimental.pallas import tpu as pltpu

F32 = jnp.float32
BF16 = jnp.bfloat16

D_MODEL = 2048
DEPTH = 4
PAGE_SIZE = 128
EPS = 1e-6
NEG_INF = -1e30
Q_BLOCK = 128
FFN_RESIDUAL = 0.5
D_FF = 5632

SSM_HEADS = 32
SSM_HEAD_DIM = 64
SSM_D_INNER = SSM_HEADS * SSM_HEAD_DIM
SSM_GROUPS = 4
SSM_STATE = 128
SSM_CONV = 4
SSM_CONV_DIM = SSM_D_INNER + 2 * SSM_GROUPS * SSM_STATE
SSM_CHUNK = 128

NSA_HEADS = 16
NSA_GROUPS = 4
NSA_HPG = NSA_HEADS // NSA_GROUPS
NSA_HEAD_DIM = 128
NSA_Q = NSA_HEADS * NSA_HEAD_DIM
NSA_KV = NSA_GROUPS * NSA_HEAD_DIM
NSA_KV_PARTS = 4
CMP_BLOCK = 32
CMP_STRIDE = 16
SLC_BLOCK = 64
SLC_TOPN = 16
WINDOW = 512
FORCE_SCORE = 1e4
REL_BUCKETS = 32
REL_MAX_DIST = 128
CMP_BAND = 2 * (Q_BLOCK // CMP_STRIDE)
NSA_FAR_BLOCKS = 4

POOL_WINDOWS = (2, 4, 8, 16)
POOL_GROUPS = len(POOL_WINDOWS)
POOL_GROUP_DIM = 256
POOL_DIM = POOL_GROUPS * POOL_GROUP_DIM
POOL_BUF = max(POOL_WINDOWS) - 1

SB_HEADS = 16
SB_HEAD_DIM = 128
SB_DIM = SB_HEADS * SB_HEAD_DIM

LANES = 128
VMEM_LIMIT_BYTES = 56 * 1024 * 1024

EVEN_MAIN = SSM_D_INNER + SSM_CONV_DIM + NSA_Q + 6 * NSA_KV
EVEN_TAIL = LANES
OFF_Z = 0
OFF_XBC = SSM_D_INNER
OFF_Q = OFF_XBC + SSM_CONV_DIM
OFF_KV = OFF_Q + NSA_Q
OFF_WIN = OFF_KV + 4 * NSA_KV


def _row_tile(m, want):
    t = min(m, want)
    assert m % t == 0
    return t


def _ffn_body(x_ref, g_ref, wg_ref, wu_ref, wd_ref, o_ref, h_ref, acc_ref):
    j = pl.program_id(1)

    @pl.when(j == 0)
    def _():
        x = x_ref[...]
        y = x * lax.rsqrt(jnp.mean(x * x, axis=-1, keepdims=True) + EPS)
        h_ref[...] = (y * g_ref[...]).astype(BF16)
        acc_ref[...] = jnp.zeros_like(acc_ref)

    h = h_ref[...]
    a = jnp.dot(h, wg_ref[...], preferred_element_type=F32)
    b = jnp.dot(h, wu_ref[...], preferred_element_type=F32)
    t = (a * jax.nn.sigmoid(a)) * b
    acc_ref[...] += jnp.dot(t.astype(BF16), wd_ref[...], preferred_element_type=F32)

    @pl.when(j == pl.num_programs(1) - 1)
    def _():
        o_ref[...] = x_ref[...] + FFN_RESIDUAL * acc_ref[...]


def ffn_half(x, g, wg, wu, wd, *, tm=512, tf=512):
    m, d = x.shape
    f = wg.shape[1]
    tm = _row_tile(m, tm)
    assert f % tf == 0
    return pl.pallas_call(
        _ffn_body,
        grid=(m // tm, f // tf),
        in_specs=[
            pl.BlockSpec((tm, d), lambda i, j: (i, 0)),
            pl.BlockSpec((1, d), lambda i, j: (0, 0)),
            pl.BlockSpec((d, tf), lambda i, j: (0, j)),
            pl.BlockSpec((d, tf), lambda i, j: (0, j)),
            pl.BlockSpec((tf, d), lambda i, j: (j, 0)),
        ],
        out_specs=pl.BlockSpec((tm, d), lambda i, j: (i, 0)),
        out_shape=jax.ShapeDtypeStruct((m, d), F32),
        scratch_shapes=[pltpu.VMEM((tm, d), BF16), pltpu.VMEM((tm, d), F32)],
        compiler_params=pltpu.CompilerParams(
            dimension_semantics=("parallel", "arbitrary"), vmem_limit_bytes=VMEM_LIMIT_BYTES),
        name="ffn_half",
    )(x, g.reshape(1, d), wg, wu, wd)


def _proj_body(flag_ref, x_ref, g_ref, w_ref, pg_ref, ps_ref, *rest, has_tail):
    if has_tail:
        wt_ref, o_ref, ot_ref, h_ref = rest
    else:
        o_ref, h_ref = rest
    j = pl.program_id(1)

    @pl.when(j == 0)
    def _():
        x = x_ref[...]
        y = x * lax.rsqrt(jnp.mean(x * x, axis=-1, keepdims=True) + EPS)
        h = (y * g_ref[...]).astype(BF16)
        h_ref[...] = h
        if has_tail:
            ot_ref[...] = jnp.dot(h, wt_ref[...], preferred_element_type=F32)

    r = jnp.dot(h_ref[...], w_ref[...], preferred_element_type=F32)

    @pl.when(flag_ref[j] == 0)
    def _():
        o_ref[...] = r

    @pl.when(flag_ref[j] != 0)
    def _():
        tm, tn = r.shape
        for c in range(tn // LANES):
            rc = r[:, c * LANES:(c + 1) * LANES]
            yc = rc * lax.rsqrt(jnp.mean(rc * rc, axis=-1, keepdims=True) + EPS)
            yc = (yc * pg_ref[:, c * LANES:(c + 1) * LANES]) * ps_ref[:, c * LANES:(c + 1) * LANES]
            o_ref[:, c * LANES:(c + 1) * LANES] = yc


def in_proj(x, g, w, flags, post_gain, post_scale, w_tail=None, *, tm=1024, tn=512):
    m, d = x.shape
    n = w.shape[1]
    tm = _row_tile(m, tm)
    assert n % tn == 0
    has_tail = w_tail is not None
    in_specs = [
        pl.BlockSpec((tm, d), lambda i, j, fl: (i, 0)),
        pl.BlockSpec((1, d), lambda i, j, fl: (0, 0)),
        pl.BlockSpec((d, tn), lambda i, j, fl: (0, j)),
        pl.BlockSpec((1, tn), lambda i, j, fl: (0, j)),
        pl.BlockSpec((1, tn), lambda i, j, fl: (0, j)),
    ]
    out_specs = [pl.BlockSpec((tm, tn), lambda i, j, fl: (i, j))]
    out_shape = [jax.ShapeDtypeStruct((m, n), F32)]
    args = [x, g.reshape(1, d), w, post_gain, post_scale]
    if has_tail:
        nt = w_tail.shape[1]
        in_specs.append(pl.BlockSpec((d, nt), lambda i, j, fl: (0, 0)))
        out_specs.append(pl.BlockSpec((tm, nt), lambda i, j, fl: (i, 0)))
        out_shape.append(jax.ShapeDtypeStruct((m, nt), F32))
        args.append(w_tail)
    res = pl.pallas_call(
        functools.partial(_proj_body, has_tail=has_tail),
        grid_spec=pltpu.PrefetchScalarGridSpec(
            num_scalar_prefetch=1,
            grid=(m // tm, n // tn),
            in_specs=in_specs,
            out_specs=out_specs,
            scratch_shapes=[pltpu.VMEM((tm, d), BF16)],
        ),
        out_shape=out_shape,
        compiler_params=pltpu.CompilerParams(
            dimension_semantics=("parallel", "arbitrary"), vmem_limit_bytes=VMEM_LIMIT_BYTES),
        name="in_proj",
    )(flags, *args)
    return res if has_tail else res[0]


def _out_proj_body(x_ref, a1_ref, a2_ref, w1_ref, w2_ref, o_ref):
    r = jnp.dot(a1_ref[...].astype(BF16), w1_ref[...], preferred_element_type=F32)
    r = r + jnp.dot(a2_ref[...].astype(BF16), w2_ref[...], preferred_element_type=F32)
    o_ref[...] = x_ref[...] + r


def out_proj(x, a1, a2, w1, w2, *, tm=512, tn=512):
    m, d = x.shape
    k1, k2 = a1.shape[1], a2.shape[1]
    tm = _row_tile(m, tm)
    return pl.pallas_call(
        _out_proj_body,
        grid=(m // tm, d // tn),
        in_specs=[
            pl.BlockSpec((tm, tn), lambda i, j: (i, j)),
            pl.BlockSpec((tm, k1), lambda i, j: (i, 0)),
            pl.BlockSpec((tm, k2), lambda i, j: (i, 0)),
            pl.BlockSpec((k1, tn), lambda i, j: (0, j)),
            pl.BlockSpec((k2, tn), lambda i, j: (0, j)),
        ],
        out_specs=pl.BlockSpec((tm, tn), lambda i, j: (i, j)),
        out_shape=jax.ShapeDtypeStruct((m, d), F32),
        compiler_params=pltpu.CompilerParams(
            dimension_semantics=("parallel", "arbitrary"), vmem_limit_bytes=VMEM_LIMIT_BYTES),
        name="out_proj",
    )(x, a1, a2, w1, w2)


def _dot_nt(a, b):
    return lax.dot_general(a, b, (((1,), (1,)), ((), ())), preferred_element_type=F32)


def _dot(a, b):
    return jnp.dot(a, b, preferred_element_type=F32)


def _softplus(z):
    return jnp.maximum(z, 0.0) + jnp.log(1.0 + jnp.exp(-jnp.abs(z)))


SB_KEY_CHUNK = 4 * Q_BLOCK
SB_HEADS_PER_STEP = 4
SB_DEAD_RUN = 110.0


def _sb_prompt_body(q_ref, k_ref, v_ref, tri_ref, o_ref):
    i = pl.program_id(2)
    blk = Q_BLOCK
    ck = SB_KEY_CHUNK
    nsub = ck // blk
    dh = SB_HEAD_DIM
    heads = range(SB_HEADS_PER_STEP)
    qs = [(q_ref[:, h * dh:(h + 1) * dh] * (dh ** -0.5)).astype(BF16) for h in heads]
    tri = tri_ref[...]

    def chunk_head(h, c, run, acc, diag):
        r0 = pl.multiple_of(c * ck, ck)
        z = _dot_nt(qs[h], k_ref[pl.ds(r0, ck), h * dh:(h + 1) * dh].astype(BF16))
        sp = _softplus(z)
        if diag:
            t_pos = i * blk + lax.broadcasted_iota(jnp.int32, (blk, ck), 0)
            s_pos = c * ck + lax.broadcasted_iota(jnp.int32, (blk, ck), 1)
            strict = s_pos < t_pos
            go = jnp.where(strict, sp, 0.0)
        else:
            go = sp
        hi = go.astype(BF16)
        lo = (go - hi.astype(F32)).astype(BF16)
        suffix = [None] * nsub
        for j in reversed(range(nsub)):
            sl = slice(j * blk, (j + 1) * blk)
            ct = _dot(hi[:, sl], tri) + _dot(lo[:, sl], tri)
            suffix[j] = (ct[:, :blk] - go[:, sl]) + run
            run = run + ct[:, blk:]
        att = jnp.exp((z - sp) - jnp.concatenate(suffix, axis=1))
        if diag:
            att = jnp.where(strict, att, 0.0)
        acc = acc + _dot(att.astype(BF16), v_ref[pl.ds(r0, ck), h * dh:(h + 1) * dh].astype(BF16))
        return run, acc

    def chunk(c, carry, diag):
        return tuple(chunk_head(h, c, carry[h][0], carry[h][1], diag) for h in heads)

    cd = i // nsub
    init = tuple((jnp.zeros((blk, blk), F32), jnp.zeros((blk, dh), F32)) for _ in heads)
    carry = chunk(cd, init, True)

    def live(state):
        r, cr = state
        lowest = functools.reduce(jnp.minimum, [jnp.min(cr[h][0]) for h in heads])
        return jnp.logical_and(r < cd, lowest <= SB_DEAD_RUN)

    _, carry = lax.while_loop(live, lambda st: (st[0] + 1, chunk(cd - 1 - st[0], st[1], False)), (jnp.int32(0), carry))
    for h in heads:
        o_ref[:, h * dh:(h + 1) * dh] = carry[h][1]


def sb_prompt(h, bsz, seq):
    qb = seq // Q_BLOCK
    hw = SB_HEADS_PER_STEP * SB_HEAD_DIM
    q_off = POOL_DIM // hw
    k_off = q_off + SB_DIM // hw
    v_off = k_off + SB_DIM // hw
    jj = np.arange(Q_BLOCK)
    tri = np.concatenate([(jj[:, None] >= jj[None, :]).astype(np.float32),
                          np.ones((Q_BLOCK, Q_BLOCK), np.float32)], axis=1)
    return pl.pallas_call(
        _sb_prompt_body,
        grid=(bsz, SB_DIM // hw, qb),
        in_specs=[
            pl.BlockSpec((Q_BLOCK, hw), lambda b, hh, i: (b * qb + i, q_off + hh)),
            pl.BlockSpec((seq, hw), lambda b, hh, i: (b, k_off + hh)),
            pl.BlockSpec((seq, hw), lambda b, hh, i: (b, v_off + hh)),
            pl.BlockSpec((Q_BLOCK, 2 * Q_BLOCK), lambda b, hh, i: (0, 0)),
        ],
        out_specs=pl.BlockSpec((Q_BLOCK, hw), lambda b, hh, i: (b * qb + i, hh)),
        out_shape=jax.ShapeDtypeStruct((bsz * seq, SB_DIM), F32),
        compiler_params=pltpu.CompilerParams(
            dimension_semantics=("parallel", "parallel", "arbitrary"), vmem_limit_bytes=VMEM_LIMIT_BYTES),
        name="sb_prompt",
    )(h, h, h, jnp.asarray(tri, BF16))


CONV_HALO = 8


def _split3(x):
    x1 = x.astype(BF16)
    r = x - x1.astype(F32)
    x2 = r.astype(BF16)
    x3 = (r - x2.astype(F32)).astype(BF16)
    return x1, x2, x3


def _ssd_body(z_ref, x_ref, bc_ref, xh_ref, bch_ref, cv_ref, tail_ref, h0_ref, cw_ref, cb_ref, dtb_r_ref, alog_r_ref,
              dtb_c_ref, alog_c_ref, dsk_ref, ng_ref, tril_ref, y_ref, ht_ref, st_ref, *, n_valid):
    c = pl.program_id(1)
    q = SSM_CHUNK
    p_dim = SSM_HEAD_DIM
    hpg = SSM_HEADS // SSM_GROUPS
    gn = SSM_GROUPS * SSM_STATE

    @pl.when(c == 0)
    def _():
        st_ref[...] = h0_ref[0]

    first = c == 0
    cv = cv_ref[0]
    xe = jnp.concatenate([jnp.where(first, cv[:, :SSM_D_INNER], xh_ref[...]), x_ref[...]], axis=0)
    bce = jnp.concatenate([jnp.where(first, cv[:, SSM_D_INNER:], bch_ref[...]), bc_ref[...]], axis=0)

    def conv(e, lo, hi):
        acc = cb_ref[:, lo:hi]
        for k in range(SSM_CONV):
            r0 = CONV_HALO - (SSM_CONV - 1) + k
            acc = acc + e[r0:r0 + q, :] * cw_ref[k:k + 1, lo:hi]
        return acc * jax.nn.sigmoid(acc)

    xs = conv(xe, 0, SSM_D_INNER)
    bcm = conv(bce, SSM_D_INNER, SSM_CONV_DIM)

    tail = tail_ref[...]
    t_row = lax.broadcasted_iota(jnp.int32, (q, LANES), 0)
    t_lane = lax.broadcasted_iota(jnp.int32, (LANES, q), 1)
    dt_r = jnp.where(t_row < n_valid, _softplus1p(tail + dtb_r_ref[...]), 0.0)
    dt_c = jnp.where(t_lane < n_valid, _softplus1p(tail.T + dtb_c_ref[...]), 0.0)
    da_r = dt_r * (-jnp.exp(alog_r_ref[...]))
    da_c = dt_c * (-jnp.exp(alog_c_ref[...]))
    tril = tril_ref[...]
    a1, a2, a3 = _split3(da_r)
    acum = (_dot(tril, a1) + _dot(tril, a2)) + _dot(tril, a3)
    c1, c2, c3 = _split3(da_c)
    acum_t = (_dot_nt(c1, tril) + _dot_nt(c2, tril)) + _dot_nt(c3, tril)
    last = acum[q - 1:q, :]
    causal = lax.broadcasted_iota(jnp.int32, (q, q), 1) <= lax.broadcasted_iota(jnp.int32, (q, q), 0)

    for g in range(SSM_GROUPS):
        bg = bcm[:, g * SSM_STATE:(g + 1) * SSM_STATE]
        cg = bcm[:, gn + g * SSM_STATE:gn + (g + 1) * SSM_STATE].astype(BF16)
        cbg = _dot_nt(cg, bg.astype(BF16))
        bg_t = bg.T.astype(BF16)
        for hh in range(hpg):
            h = g * hpg + hh
            col = acum[:, h:h + 1]
            decay = jnp.exp(jnp.where(causal, col - acum_t[h:h + 1, :], NEG_INF))
            xh = xs[:, h * p_dim:(h + 1) * p_dim]
            xdt = xh * dt_r[:, h:h + 1]
            y_diag = _dot((cbg * decay).astype(BF16), xdt.astype(BF16))
            s_t = st_ref[h]
            y_off = _dot(cg, s_t.astype(BF16)) * jnp.exp(col)
            end = last[:, h:h + 1]
            st_ref[h] = s_t * jnp.exp(end) + _dot(bg_t, (xdt * jnp.exp(end - col)).astype(BF16))
            y_ref[:, h * p_dim:(h + 1) * p_dim] = (y_diag + y_off) + dsk_ref[:, h * p_dim:(h + 1) * p_dim] * xh

    z = z_ref[...]
    y = y_ref[...] * (z * jax.nn.sigmoid(z))
    gw = SSM_D_INNER // SSM_GROUPS
    for g in range(SSM_GROUPS):
        yg = y[:, g * gw:(g + 1) * gw]
        yg = yg * lax.rsqrt(jnp.mean(yg * yg, axis=-1, keepdims=True) + EPS)
        y_ref[:, g * gw:(g + 1) * gw] = yg * ng_ref[:, g * gw:(g + 1) * gw]

    @pl.when(c == pl.num_programs(1) - 1)
    def _():
        ht_ref[0] = st_ref[...]


def _softplus1p(x):
    return jnp.maximum(x, 0.0) + jnp.log1p(jnp.exp(-jnp.abs(x)))


def ssd_mix(main, tail, conv_halo, h0_t, pw, bsz, seq, n_valid):
    q = SSM_CHUNK
    nc = seq // q
    per = q // CONV_HALO
    bcw = SSM_CONV_DIM - SSM_D_INNER
    pad_h = LANES - SSM_HEADS
    dtb = jnp.pad(pw['dt_bias'], (0, pad_h))
    alog = jnp.pad(pw['a_log'], (0, pad_h))
    tt = np.arange(q)
    tril = (tt[None, :] <= tt[:, None]).astype(np.float32)
    halo = lambda b, c: (jnp.maximum((b * nc + c) * per - 1, 0), 0)
    st_shape = (SSM_HEADS, SSM_STATE, SSM_HEAD_DIM)
    vec = lambda n: pl.BlockSpec((1, n), lambda b, c: (0, 0))
    colv = pl.BlockSpec((LANES, 1), lambda b, c: (0, 0))
    y, ht = pl.pallas_call(
        functools.partial(_ssd_body, n_valid=n_valid),
        grid=(bsz, nc),
        in_specs=[
            pl.BlockSpec((q, SSM_D_INNER), lambda b, c: (b * nc + c, 0)),
            pl.BlockSpec((q, SSM_D_INNER), lambda b, c: (b * nc + c, 1)),
            pl.BlockSpec((q, bcw), lambda b, c: (b * nc + c, 2 * SSM_D_INNER // bcw)),
            pl.BlockSpec((CONV_HALO, SSM_D_INNER), lambda b, c: (halo(b, c)[0], 1)),
            pl.BlockSpec((CONV_HALO, bcw), lambda b, c: (halo(b, c)[0], 2 * SSM_D_INNER // bcw)),
            pl.BlockSpec((1, CONV_HALO, SSM_CONV_DIM), lambda b, c: (b, 0, 0)),
            pl.BlockSpec((q, LANES), lambda b, c: (b * nc + c, 0)),
            pl.BlockSpec((1,) + st_shape, lambda b, c: (b, 0, 0, 0)),
            pl.BlockSpec((SSM_CONV, SSM_CONV_DIM), lambda b, c: (0, 0)),
            vec(SSM_CONV_DIM), vec(LANES), vec(LANES), colv, colv, vec(SSM_D_INNER), vec(SSM_D_INNER),
            pl.BlockSpec((q, q), lambda b, c: (0, 0)),
        ],
        out_specs=[pl.BlockSpec((q, SSM_D_INNER), lambda b, c: (b * nc + c, 0)),
                   pl.BlockSpec((1,) + st_shape, lambda b, c: (b, 0, 0, 0))],
        out_shape=[jax.ShapeDtypeStruct((bsz * seq, SSM_D_INNER), F32),
                   jax.ShapeDtypeStruct((bsz,) + st_shape, F32)],
        scratch_shapes=[pltpu.VMEM(st_shape, F32)],
        compiler_params=pltpu.CompilerParams(
            dimension_semantics=("parallel", "arbitrary"), vmem_limit_bytes=VMEM_LIMIT_BYTES),
        name="ssd_mix",
    )(main, main, main, main, main, conv_halo, tail, h0_t, pw['conv_w'], pw['conv_b'].reshape(1, -1),
      dtb.reshape(1, LANES), alog.reshape(1, LANES), dtb.reshape(LANES, 1), alog.reshape(LANES, 1),
      jnp.repeat(pw['d_skip'], SSM_HEAD_DIM).reshape(1, -1), pw['ssm_g'].reshape(1, -1), jnp.asarray(tril, BF16))
    return y, ht


POOL_HALO = 16


def _pool_body(u_ref, halo_ref, buf_ref, w_ref, sc_ref, o_ref, *, pos0):
    i = pl.program_id(1)
    tl = u_ref.shape[0]
    u = u_ref[...]
    prev = jnp.where(i == 0, buf_ref[0], halo_ref[...])
    x = jnp.concatenate([prev, u], axis=0)
    pos = pos0 + i * tl + lax.broadcasted_iota(jnp.int32, (tl, 1), 0)
    for g, w in enumerate(POOL_WINDOWS):
        cols = slice(g * POOL_GROUP_DIM, (g + 1) * POOL_GROUP_DIM)
        s = x[:, cols]
        span = 1
        while span < w:
            s = s[span:, :] + s[:-span, :]
            span *= 2
        win = s[POOL_HALO - (w - 1):, :]
        cnt = jnp.minimum(pos + 1, w).astype(F32)
        diff = win / cnt - u[:, cols]
        y = _dot(diff.astype(BF16), w_ref[g])
        o_ref[:, cols] = y * sc_ref[:, cols]


def pool_mix_pallas(h, buf16, pool_w, pool_scale, bsz, seq, pos0, *, tl=512):
    tl = min(tl, seq)
    nt = seq // tl
    per = max(tl // POOL_HALO, 1)
    if tl < POOL_HALO:
        assert nt == 1
        halo_src = buf16.reshape(bsz * POOL_HALO, POOL_DIM)
        halo_map = lambda b, i: (b, 0)
    else:
        halo_src = h
        halo_map = lambda b, i: (jnp.maximum((b * nt + i) * per - 1, 0), 0)
    return pl.pallas_call(
        functools.partial(_pool_body, pos0=pos0),
        grid=(bsz, nt),
        in_specs=[
            pl.BlockSpec((tl, POOL_DIM), lambda b, i: (b * nt + i, 0)),
            pl.BlockSpec((POOL_HALO, POOL_DIM), halo_map),
            pl.BlockSpec((1, POOL_HALO, POOL_DIM), lambda b, i: (b, 0, 0)),
            pl.BlockSpec((POOL_GROUPS, POOL_GROUP_DIM, POOL_GROUP_DIM), lambda b, i: (0, 0, 0)),
            pl.BlockSpec((1, POOL_DIM), lambda b, i: (0, 0)),
        ],
        out_specs=pl.BlockSpec((tl, POOL_DIM), lambda b, i: (b * nt + i, 0)),
        out_shape=jax.ShapeDtypeStruct((bsz * seq, POOL_DIM), F32),
        compiler_params=pltpu.CompilerParams(
            dimension_semantics=("parallel", "arbitrary"), vmem_limit_bytes=VMEM_LIMIT_BYTES),
        name="pool_mix",
    )(h, halo_src, buf16, pool_w.astype(BF16), pool_scale.reshape(1, POOL_DIM))


def _sb_decode_body(pt_ref, qbd_ref, new_ref, page_ref, tri_ref, run0_ref, acc0_ref, o_ref, run_ref, acc_ref, *,
                    n_tok, fresh):
    p = pl.program_id(1)
    rows = SB_HEADS * n_tok
    tri = tri_ref[...]
    qbd = qbd_ref[0]

    def slabs(ref, part):
        return jnp.concatenate([ref[0, :, part * SB_HEADS + h, :] for h in range(SB_HEADS)], axis=1).astype(BF16)

    def page_update(k_all, v_all, masked):
        z = _dot_nt(qbd, k_all)
        sp = _softplus(z)
        if masked:
            tok = lax.broadcasted_iota(jnp.int32, (rows, PAGE_SIZE), 0) % n_tok
            key = lax.broadcasted_iota(jnp.int32, (rows, PAGE_SIZE), 1)
            strict = key < tok
            go = jnp.where(strict, sp, 0.0)
        else:
            go = sp
        hi = go.astype(BF16)
        lo = (go - hi.astype(F32)).astype(BF16)
        ct = _dot(hi, tri) + _dot(lo, tri)
        run = run_ref[0]
        att = jnp.exp((z - sp) - ((ct[:, :PAGE_SIZE] - go) + run))
        if masked:
            att = jnp.where(strict, att, 0.0)
        run_ref[0] = run + ct[:, PAGE_SIZE:]
        acc_ref[0] += _dot(att.astype(BF16), v_all)

    @pl.when(p == 0)
    def _():
        run_ref[...] = run0_ref[...]
        acc_ref[...] = acc0_ref[...]
        if fresh:
            page_update(new_ref[0, :, :SB_DIM].astype(BF16), new_ref[0, :, SB_DIM:].astype(BF16), True)

    @pl.when(jnp.min(run_ref[...]) <= SB_DEAD_RUN)
    def _():
        page_update(slabs(page_ref, 0), slabs(page_ref, 1), False)

    @pl.when(p == pl.num_programs(1) - 1)
    def _():
        for h in range(SB_HEADS):
            o_ref[0, :, h * SB_HEAD_DIM:(h + 1) * SB_HEAD_DIM] = (
                acc_ref[0, h * n_tok:(h + 1) * n_tok, h * SB_HEAD_DIM:(h + 1) * SB_HEAD_DIM])


SB_FIRST_PAGES = 4


def sb_decode(h, cache, page_table, bsz, n_tok):
    n_pages = page_table.shape[1]
    rows = SB_HEADS * n_tok
    assert rows == PAGE_SIZE and n_tok <= PAGE_SIZE
    h3 = h.reshape(bsz, n_tok, -1)
    q = h3[..., POOL_DIM:POOL_DIM + SB_DIM].reshape(bsz, n_tok, SB_HEADS, SB_HEAD_DIM) * (SB_HEAD_DIM ** -0.5)
    eye = jnp.eye(SB_HEADS, dtype=F32)
    qbd = jnp.einsum('bthd,hg->bhtgd', q, eye).reshape(bsz, rows, SB_DIM).astype(BF16)
    new_kv = jnp.pad(h3[..., POOL_DIM + SB_DIM:], ((0, 0), (0, PAGE_SIZE - n_tok), (0, 0)))
    jj = np.arange(PAGE_SIZE)
    tri = jnp.asarray(np.concatenate([(jj[:, None] >= jj[None, :]).astype(np.float32),
                                      np.ones((PAGE_SIZE, PAGE_SIZE), np.float32)], axis=1), BF16)
    newest_first = page_table[:, ::-1]

    def walk(pages, run0, acc0, fresh):
        n = pages.shape[1]
        state = lambda w: pl.BlockSpec((1, rows, w), lambda b, p, pt: (b, 0, 0))
        return pl.pallas_call(
            functools.partial(_sb_decode_body, n_tok=n_tok, fresh=fresh),
            grid_spec=pltpu.PrefetchScalarGridSpec(
                num_scalar_prefetch=1,
                grid=(bsz, n),
                in_specs=[
                    state(SB_DIM),
                    pl.BlockSpec((1, PAGE_SIZE, 2 * SB_DIM), lambda b, p, pt: (b, 0, 0)),
                    pl.BlockSpec((1, PAGE_SIZE, 2 * SB_HEADS, SB_HEAD_DIM), lambda b, p, pt: (pt[b, p], 0, 0, 0)),
                    pl.BlockSpec((PAGE_SIZE, 2 * PAGE_SIZE), lambda b, p, pt: (0, 0)),
                    state(PAGE_SIZE), state(SB_DIM),
                ],
                out_specs=[pl.BlockSpec((1, n_tok, SB_DIM), lambda b, p, pt: (b, 0, 0)),
                           state(PAGE_SIZE), state(SB_DIM)],
            ),
            out_shape=[jax.ShapeDtypeStruct((bsz, n_tok, SB_DIM), F32),
                       jax.ShapeDtypeStruct((bsz, rows, PAGE_SIZE), F32),
                       jax.ShapeDtypeStruct((bsz, rows, SB_DIM), F32)],
            compiler_params=pltpu.CompilerParams(
                dimension_semantics=("parallel", "arbitrary"), vmem_limit_bytes=VMEM_LIMIT_BYTES),
            name="sb_decode",
        )(pages, qbd, new_kv, cache, tri, run0, acc0)

    k0 = min(SB_FIRST_PAGES, n_pages)
    o, run, acc = walk(newest_first[:, :k0], jnp.zeros((bsz, rows, PAGE_SIZE), F32),
                       jnp.zeros((bsz, rows, SB_DIM), F32), True)
    if k0 == n_pages:
        return o
    return lax.cond(jnp.min(run) > SB_DEAD_RUN, lambda: o,
                    lambda: walk(newest_first[:, k0:], run, acc, False)[0])


def _rel_bucket_np(dist):
    n = np.maximum(dist, 0)
    exact = REL_BUCKETS // 2
    nf = np.maximum(n, 1).astype(np.float32)
    log_b = exact + (np.log(nf / np.float32(exact)) / np.float32(math.log(REL_MAX_DIST / exact))
                     * np.float32(REL_BUCKETS - exact)).astype(np.int32)
    return np.where(n < exact, n, np.minimum(log_b, REL_BUCKETS - 1)).astype(np.int32)


def _bias_select(rows, bucket):
    ids = jnp.asarray(bucket)
    out = jnp.broadcast_to(rows[REL_BUCKETS - 1], bucket.shape)
    for k in range(REL_BUCKETS - 2, -1, -1):
        out = jnp.where(ids == k, rows[k], out)
    return out


def nsa_bias_tables(rel_bias, seq):
    s = np.arange(Q_BLOCK)
    rows = jnp.repeat(rel_bias.reshape(REL_BUCKETS, NSA_GROUPS, NSA_HPG).transpose(1, 0, 2), Q_BLOCK, axis=-1)
    far_from = (CMP_BAND // 2 + 1) * CMP_STRIDE - (CMP_BLOCK - 1)
    assert (_rel_bucket_np(np.arange(far_from, 2 * seq)) == REL_BUCKETS - 1).all()
    near_idx = _rel_bucket_np(np.arange(2)[:, None, None] * Q_BLOCK + s[None, None, :] - s[None, :, None])
    near_idx = np.tile(near_idx, (1, 1, NSA_HPG))
    near = jnp.stack([_bias_select(rows[g], near_idx) for g in range(NSA_GROUPS)])
    far = rows[:, REL_BUCKETS - 1:, :]
    m_rel = np.arange(-CMP_BAND // 2, CMP_BAND // 2).reshape(CMP_BAND, 1)
    dist_c = s.reshape(1, Q_BLOCK) - (m_rel * CMP_STRIDE + CMP_BLOCK - 1)
    cb_idx = np.tile(_rel_bucket_np(dist_c), (1, NSA_HPG))
    cb_ok = jnp.asarray(np.tile(dist_c >= 0, (1, NSA_HPG)))
    cb = jnp.stack([jnp.where(cb_ok, _bias_select(rows[g], cb_idx), NEG_INF) for g in range(NSA_GROUPS)])
    return near, far, cb


def _nsa_prompt_body(q_ref, tail_ref, ks_ref, vs_ref, kw_ref, vw_ref, kcmp_ref, vcmpt_ref, cb_ref, nb_ref, fb_ref,
                     at_ref, o_ref, vst_ref, vwt_ref, sel_ref, gt_ref, cbs_ref, kwp_ref, *, seq):
    g = pl.program_id(1)
    i = pl.program_id(2)
    blk = Q_BLOCK
    hw = NSA_HPG * blk
    n_slc = seq // SLC_BLOCK

    @pl.when(i == 0)
    def _():
        kwp_ref[0:WINDOW, :] = jnp.zeros((WINDOW, NSA_HEAD_DIM), BF16)
        vwt_ref[:, 0:WINDOW] = jnp.zeros((NSA_HEAD_DIM, WINDOW), BF16)
        for c in range(seq // blk):
            vst_ref[:, c * blk:(c + 1) * blk] = vs_ref[c * blk:(c + 1) * blk, :].T.astype(BF16)
            kwp_ref[WINDOW + c * blk:WINDOW + (c + 1) * blk, :] = kw_ref[c * blk:(c + 1) * blk, :].astype(BF16)
            vwt_ref[:, WINDOW + c * blk:WINDOW + (c + 1) * blk] = vw_ref[c * blk:(c + 1) * blk, :].T.astype(BF16)

    q4 = jnp.concatenate([q_ref[:, h * blk:(h + 1) * blk] for h in range(NSA_HPG)], axis=0).astype(BF16)

    n_pad = seq // CMP_STRIDE
    half = CMP_BAND // 2
    rows = lax.broadcasted_iota(jnp.int32, (n_pad + half, hw), 0)
    cbs_ref[...] = jnp.where(rows < half * i, jnp.broadcast_to(fb_ref[0], (n_pad + half, hw)), NEG_INF)
    cbs_ref[pl.ds(pl.multiple_of(half * i, half), CMP_BAND), :] = cb_ref[0]
    cb = cbs_ref[half:, :]
    sc = _dot_nt(kcmp_ref[0, 0].astype(BF16), q4) + cb
    mc = jnp.max(sc, axis=0, keepdims=True)
    pc = jnp.exp(sc - mc)
    pc = pc / jnp.sum(pc, axis=0, keepdims=True)
    pc = pc * jnp.where(cb > 0.5 * NEG_INF, 1.0, 0.0)
    o_c = _dot(vcmpt_ref[0, 0].astype(BF16), pc.astype(BF16))
    psum = pc[:, 0:blk]
    for h in range(1, NSA_HPG):
        psum = psum + pc[:, h * blk:(h + 1) * blk]
    p1 = psum.astype(BF16)
    r1 = psum - p1.astype(F32)
    p2 = r1.astype(BF16)
    p3 = (r1 - p2.astype(F32)).astype(BF16)
    at = at_ref[...]
    imp = (_dot(at, p1) + _dot(at, p2)) + _dot(at, p3)

    jdx = lax.broadcasted_iota(jnp.int32, (n_slc, blk), 0)
    tpos = i * blk + lax.broadcasted_iota(jnp.int32, (n_slc, blk), 1)
    cur = tpos // SLC_BLOCK
    forced = jnp.where(jdx == 0, 1.0, 0.0) + jnp.where(jdx == cur, 1.0, 0.0) + jnp.where(jdx == cur - 1, 1.0, 0.0)
    score = jnp.where(forced > 0.0, FORCE_SCORE, jnp.where(jdx <= cur, imp, -1.0))
    rank = jnp.zeros((n_slc, blk), F32)
    for r in range(n_slc):
        row = jnp.broadcast_to(score[r:r + 1, :], (n_slc, blk))
        gt = jnp.where(row > score, 1.0, 0.0)
        ge = jnp.where(row >= score, 1.0, 0.0)
        rank = rank + jnp.where(jdx > r, ge, gt)
    sel_ref[...] = jnp.where(rank < float(min(SLC_TOPN, n_slc)), 1.0, 0.0)

    ss = lax.broadcasted_iota(jnp.int32, (blk, blk), 0)
    tt = lax.broadcasted_iota(jnp.int32, (blk, blk), 1)
    causal = jnp.where(ss <= tt, 1.0, 0.0)
    anti = jnp.where(ss >= tt, 1.0, 0.0)
    bias_far = fb_ref[0]

    def tile4(mk):
        return jnp.concatenate([mk] * NSA_HPG, axis=1)

    def scores(k_ref, kb, bias):
        r0 = pl.multiple_of(kb * blk, blk)
        return _dot_nt(k_ref[pl.ds(r0, blk), :].astype(BF16), q4) + bias

    def pv(vt_ref, kb, p):
        r0 = pl.multiple_of(kb * blk, blk)
        return _dot(vt_ref[:, pl.ds(r0, blk)], p.astype(BF16))

    def first(s, mk):
        s = jnp.where(tile4(mk) > 0.0, s, NEG_INF)
        m = jnp.max(s, axis=0, keepdims=True)
        p = jnp.exp(s - m)
        return m, jnp.sum(p, axis=0, keepdims=True), p

    def update(carry, s, mk, vt_ref, kb):
        m, l, acc = carry
        s = jnp.where(tile4(mk) > 0.0, s, NEG_INF)
        m_new = jnp.maximum(m, jnp.max(s, axis=0, keepdims=True))
        alpha = jnp.exp(m - m_new)
        p = jnp.exp(s - m_new)
        return m_new, alpha * l + jnp.sum(p, axis=0, keepdims=True), alpha * acc + pv(vt_ref, kb, p)

    def sel_mask(kb):
        r0 = sel_ref[pl.ds(2 * kb, 1), :]
        r1 = sel_ref[pl.ds(2 * kb + 1, 1), :]
        half = blk // 2
        return jnp.concatenate([jnp.broadcast_to(r0, (half, blk)), jnp.broadcast_to(r1, (half, blk))], axis=0)

    m, l, p = first(scores(ks_ref, i, nb_ref[0, 0]), sel_mask(i) * causal)
    carry = (m, l, pv(vst_ref, i, p))
    kb1 = jnp.maximum(i - 1, 0)
    ok1 = jnp.where(i >= 1, 1.0, 0.0)
    carry = update(carry, scores(ks_ref, kb1, nb_ref[0, 1]), sel_mask(kb1) * ok1, vst_ref, kb1)

    nfar = NSA_FAR_BLOCKS

    def far_body(r, c):
        m, l, acc = c
        r0 = pl.multiple_of(r * nfar * blk, nfar * blk)
        s = _dot_nt(ks_ref[pl.ds(r0, nfar * blk), :].astype(BF16), q4) + bias_far
        mk = jnp.concatenate([sel_mask(nfar * r + j) * jnp.where(nfar * r + j <= i - 2, 1.0, 0.0)
                              for j in range(nfar)], axis=0)
        s = jnp.where(tile4(mk) > 0.0, s, NEG_INF)
        m_new = jnp.maximum(m, jnp.max(s, axis=0, keepdims=True))
        alpha = jnp.exp(m - m_new)
        p = jnp.exp(s - m_new)
        pvv = _dot(vst_ref[:, pl.ds(r0, nfar * blk)], p.astype(BF16))
        return m_new, alpha * l + jnp.sum(p, axis=0, keepdims=True), alpha * acc + pvv

    m, l, acc = lax.fori_loop(0, (i + nfar - 2) // nfar, far_body, carry)
    o_s = acc / l

    nwb = WINDOW // blk
    w0 = pl.multiple_of(i * blk, blk)
    ones = jnp.ones((blk, blk), F32)
    bias_w = jnp.concatenate([jnp.broadcast_to(bias_far, ((nwb - 1) * blk, hw)), nb_ref[0, 1], nb_ref[0, 0]], axis=0)
    mask_w = jnp.concatenate(
        [(anti if d == nwb else ones) * jnp.where(i >= d, 1.0, 0.0) for d in range(nwb, 0, -1)] + [causal], axis=0)
    sw = _dot_nt(kwp_ref[pl.ds(w0, WINDOW + blk), :], q4) + bias_w
    sw = jnp.where(tile4(mask_w) > 0.0, sw, NEG_INF)
    pw = jnp.exp(sw - jnp.max(sw, axis=0, keepdims=True))
    o_w = _dot(vwt_ref[:, pl.ds(w0, WINDOW + blk)], pw.astype(BF16)) / jnp.sum(pw, axis=0, keepdims=True)

    gt_ref[...] = tail_ref[...].T

    def gate(c):
        rows = [gt_ref[pl.ds(SSM_HEADS + 3 * (NSA_HPG * g + h) + c, 1), :] for h in range(NSA_HPG)]
        return jax.nn.sigmoid(jnp.concatenate(rows, axis=1))

    o_t = (gate(0) * o_c + gate(1) * o_s) + gate(2) * o_w
    for h in range(NSA_HPG):
        o_ref[:, h * blk:(h + 1) * blk] = o_t[:, h * blk:(h + 1) * blk].T


def nsa_prompt(main, tail, kcmp, vcmpt, tables, bsz, seq):
    near, far, cb = tables
    qb = seq // Q_BLOCK
    n_pad = seq // CMP_STRIDE
    n_slc = seq // SLC_BLOCK
    hw = NSA_HPG * Q_BLOCK
    ratio = SLC_BLOCK // CMP_STRIDE
    lo = CMP_BLOCK // CMP_STRIDE - 1
    jj = np.arange(n_slc)[:, None]
    nn = np.arange(n_pad)[None, :]
    a_t = ((nn >= ratio * jj - lo) & (nn <= ratio * jj + ratio - 1) & (nn < n_pad - 1)).astype(np.float32)
    dh = NSA_HEAD_DIM
    col = lambda off: off // dh
    return pl.pallas_call(
        functools.partial(_nsa_prompt_body, seq=seq),
        grid=(bsz, NSA_GROUPS, qb),
        in_specs=[
            pl.BlockSpec((Q_BLOCK, hw), lambda b, g, i: (b * qb + i, OFF_Q // hw + g)),
            pl.BlockSpec((Q_BLOCK, EVEN_TAIL), lambda b, g, i: (b * qb + i, 0)),
            pl.BlockSpec((seq, dh), lambda b, g, i: (b, col(OFF_KV + 2 * NSA_KV) + g)),
            pl.BlockSpec((seq, dh), lambda b, g, i: (b, col(OFF_KV + 3 * NSA_KV) + g)),
            pl.BlockSpec((seq, dh), lambda b, g, i: (b, col(OFF_WIN) + g)),
            pl.BlockSpec((seq, dh), lambda b, g, i: (b, col(OFF_WIN + NSA_KV) + g)),
            pl.BlockSpec((1, 1, n_pad, dh), lambda b, g, i: (b, g, 0, 0)),
            pl.BlockSpec((1, 1, dh, n_pad), lambda b, g, i: (b, g, 0, 0)),
            pl.BlockSpec((1, CMP_BAND, hw), lambda b, g, i: (g, 0, 0)),
            pl.BlockSpec((1, 2, Q_BLOCK, hw), lambda b, g, i: (g, 0, 0, 0)),
            pl.BlockSpec((1, 1, hw), lambda b, g, i: (g, 0, 0)),
            pl.BlockSpec((n_slc, n_pad), lambda b, g, i: (0, 0)),
        ],
        out_specs=pl.BlockSpec((Q_BLOCK, hw), lambda b, g, i: (b * qb + i, g)),
        out_shape=jax.ShapeDtypeStruct((bsz * seq, NSA_Q), F32),
        scratch_shapes=[pltpu.VMEM((dh, seq), BF16), pltpu.VMEM((dh, seq + WINDOW), BF16),
                        pltpu.VMEM((n_slc, Q_BLOCK), F32), pltpu.VMEM((EVEN_TAIL, Q_BLOCK), F32),
                        pltpu.VMEM((n_pad + CMP_BAND // 2, hw), F32), pltpu.VMEM((seq + WINDOW, dh), BF16)],
        compiler_params=pltpu.CompilerParams(
            dimension_semantics=("parallel", "parallel", "arbitrary"), vmem_limit_bytes=VMEM_LIMIT_BYTES),
        name="nsa_prompt",
    )(main, tail, main, main, main, main, kcmp, vcmpt, cb, near, far, jnp.asarray(a_t, BF16))


CMP_COLS = 2 * NSA_KV
SUBS_PER_PAGE = PAGE_SIZE // CMP_STRIDE


def _cmp_pool_body(idx_ref, src_ref, a1_ref, a2_ref, o_ref, prev_ref):
    p = pl.program_id(1)
    x = src_ref[0]
    first = (x * a1_ref[...]).reshape(SUBS_PER_PAGE, CMP_STRIDE, CMP_COLS).sum(axis=1)
    second = (x * a2_ref[...]).reshape(SUBS_PER_PAGE, CMP_STRIDE, CMP_COLS).sum(axis=1)
    prev = jnp.where(p == 0, 0.0, prev_ref[...])
    shifted = jnp.concatenate([prev[SUBS_PER_PAGE - 1:, :], first[:SUBS_PER_PAGE - 1, :]], axis=0)
    o_ref[0] = shifted + second
    prev_ref[...] = first


CMP_PAGES_PER_STEP = 4


def _cmp_pool_slab_body(idx_ref, *refs):
    src_refs = refs[:CMP_PAGES_PER_STEP]
    a1_ref, a2_ref, o_ref, prev_ref = refs[CMP_PAGES_PER_STEP:]
    p = pl.program_id(1)
    prev = jnp.where(p == 0, 0.0, prev_ref[...])
    for j, src_ref in enumerate(src_refs):
        x = src_ref[0].reshape(SUBS_PER_PAGE, CMP_STRIDE, 2 * NSA_GROUPS, NSA_HEAD_DIM)
        first = (x * a1_ref[...][None]).sum(axis=1)
        second = (x * a2_ref[...][None]).sum(axis=1)
        shifted = jnp.concatenate([prev[SUBS_PER_PAGE - 1:], first[:SUBS_PER_PAGE - 1]], axis=0)
        o_ref[0, j * SUBS_PER_PAGE:(j + 1) * SUBS_PER_PAGE] = shifted + second
        prev = first
    prev_ref[...] = prev


def cmp_pool_pages(cache, page_idx, cmp_alpha):
    bsz, n = page_idx.shape
    pps = CMP_PAGES_PER_STEP
    assert n % pps == 0
    slabs = 2 * NSA_GROUPS
    dh = NSA_HEAD_DIM

    def tiled(half):
        return jnp.concatenate([jnp.repeat(cmp_alpha[0][half][:, None, :], NSA_GROUPS, axis=1),
                                jnp.repeat(cmp_alpha[1][half][:, None, :], NSA_GROUPS, axis=1)], axis=1)

    a1 = tiled(slice(0, CMP_STRIDE))
    a2 = tiled(slice(CMP_STRIDE, CMP_BLOCK))

    def page_spec(j):
        return pl.BlockSpec((1, PAGE_SIZE, slabs, dh), lambda b, p, idx: (idx[b, p * pps + j], 0, 0, 0))

    return pl.pallas_call(
        _cmp_pool_slab_body,
        grid_spec=pltpu.PrefetchScalarGridSpec(
            num_scalar_prefetch=1,
            grid=(bsz, n // pps),
            in_specs=[page_spec(j) for j in range(pps)] + [
                pl.BlockSpec((CMP_STRIDE, slabs, dh), lambda b, p, idx: (0, 0, 0)),
                pl.BlockSpec((CMP_STRIDE, slabs, dh), lambda b, p, idx: (0, 0, 0)),
            ],
            out_specs=pl.BlockSpec((1, pps * SUBS_PER_PAGE, slabs, dh), lambda b, p, idx: (b, p, 0, 0)),
            scratch_shapes=[pltpu.VMEM((SUBS_PER_PAGE, slabs, dh), F32)],
        ),
        out_shape=jax.ShapeDtypeStruct((bsz, n * SUBS_PER_PAGE, slabs, dh), F32),
        compiler_params=pltpu.CompilerParams(
            dimension_semantics=("parallel", "arbitrary"), vmem_limit_bytes=VMEM_LIMIT_BYTES),
        name="cmp_pool_pages",
    )(page_idx, *([cache] * pps), a1, a2)


def cmp_pool(src, block_idx, col_block, cmp_alpha):
    bsz, n = block_idx.shape

    def tiled(half):
        a = jnp.concatenate([jnp.tile(cmp_alpha[0][half], (1, NSA_GROUPS)),
                             jnp.tile(cmp_alpha[1][half], (1, NSA_GROUPS))], axis=1)
        return jnp.tile(a, (SUBS_PER_PAGE, 1))

    a1 = tiled(slice(0, CMP_STRIDE))
    a2 = tiled(slice(CMP_STRIDE, CMP_BLOCK))
    return pl.pallas_call(
        _cmp_pool_body,
        grid_spec=pltpu.PrefetchScalarGridSpec(
            num_scalar_prefetch=1,
            grid=(bsz, n),
            in_specs=[
                pl.BlockSpec((1, PAGE_SIZE, CMP_COLS), lambda b, p, idx: (idx[b, p], 0, col_block)),
                pl.BlockSpec((PAGE_SIZE, CMP_COLS), lambda b, p, idx: (0, 0)),
                pl.BlockSpec((PAGE_SIZE, CMP_COLS), lambda b, p, idx: (0, 0)),
            ],
            out_specs=pl.BlockSpec((1, SUBS_PER_PAGE, CMP_COLS), lambda b, p, idx: (b, p, 0)),
            scratch_shapes=[pltpu.VMEM((SUBS_PER_PAGE, CMP_COLS), F32)],
        ),
        out_shape=jax.ShapeDtypeStruct((bsz, n * SUBS_PER_PAGE, CMP_COLS), F32),
        compiler_params=pltpu.CompilerParams(
            dimension_semantics=("parallel", "arbitrary"), vmem_limit_bytes=VMEM_LIMIT_BYTES),
        name="cmp_pool",
    )(block_idx, src, a1, a2)


def _cmp_project_body(p_ref, wk_ref, wv_ref, gk_ref, k_ref, vt_ref):
    n = p_ref.shape[1]
    dh = NSA_HEAD_DIM
    zero = jnp.zeros((1, dh), F32)
    pk = jnp.concatenate([p_ref[0, 1:, 0:dh], zero], axis=0)
    pv = jnp.concatenate([p_ref[0, 1:, dh:2 * dh], zero], axis=0)
    kp = _dot(pk.astype(BF16), wk_ref[...])
    k_ref[0, 0] = (kp * lax.rsqrt(jnp.mean(kp * kp, axis=-1, keepdims=True) + EPS)) * gk_ref[...]
    vp = _dot(pv.astype(BF16), wv_ref[...])
    for c in range(n // LANES):
        vt_ref[0, 0, :, c * LANES:(c + 1) * LANES] = vp[c * LANES:(c + 1) * LANES, :].T


def cmp_project(pooled, cmp_w, g_kcmp):
    bsz, n, _ = pooled.shape
    dh = NSA_HEAD_DIM
    pg = pooled.reshape(bsz, n, 2, NSA_GROUPS, dh).transpose(0, 3, 1, 2, 4).reshape(bsz * NSA_GROUPS, n, 2 * dh)
    k, vt = pl.pallas_call(
        _cmp_project_body,
        grid=(bsz, NSA_GROUPS),
        in_specs=[
            pl.BlockSpec((1, n, 2 * dh), lambda b, g: (b * NSA_GROUPS + g, 0, 0)),
            pl.BlockSpec((dh, dh), lambda b, g: (0, 0)),
            pl.BlockSpec((dh, dh), lambda b, g: (0, 0)),
            pl.BlockSpec((1, dh), lambda b, g: (0, 0)),
        ],
        out_specs=[pl.BlockSpec((1, 1, n, dh), lambda b, g: (b, g, 0, 0)),
                   pl.BlockSpec((1, 1, dh, n), lambda b, g: (b, g, 0, 0))],
        out_shape=[jax.ShapeDtypeStruct((bsz, NSA_GROUPS, n, dh), F32),
                   jax.ShapeDtypeStruct((bsz, NSA_GROUPS, dh, n), F32)],
        compiler_params=pltpu.CompilerParams(
            dimension_semantics=("parallel", "parallel"), vmem_limit_bytes=VMEM_LIMIT_BYTES),
        name="cmp_project",
    )(pg, cmp_w[0].astype(BF16), cmp_w[1].astype(BF16), g_kcmp.reshape(1, dh))
    return k, vt


def _nsa_decode_body(pt_ref, qbd_ref, pool_ref, wk_ref, wv_ref, gk_ref, cbias_ref, at_ref, gsum_ref, gexp_ref,
                     new_ref, nbias_ref, page_ref, lbias_ref, far_ref, win_ref, wbias_ref, gate_ref, o_ref,
                     kc_ref, vct_ref, oc_ref, score_ref, sel_ref, m_ref, l_ref, acc_ref, *, n_tok, pos0):
    s = pl.program_id(1)
    n_pages = pl.num_programs(1) - 1
    dh = NSA_HEAD_DIM
    gd = NSA_GROUPS * dh
    lanes = NSA_HEADS * n_tok
    n_cmp = pool_ref.shape[1]
    n_rows = score_ref.shape[0]
    qbd = qbd_ref[0]

    def kv_t(v):
        return jnp.concatenate([v[:, g * dh:(g + 1) * dh].T for g in range(NSA_GROUPS)], axis=0).astype(BF16)

    def masked_update(sc, mask_rows, v, init):
        sc = jnp.where(mask_rows > 0.0, sc, NEG_INF)
        m_old = jnp.full((1, lanes), NEG_INF, F32) if init else m_ref[...]
        m_new = jnp.maximum(m_old, jnp.max(sc, axis=0, keepdims=True))
        p = jnp.exp(sc - m_new)
        pv = _dot(kv_t(v), p.astype(BF16))
        if init:
            l_ref[...] = jnp.sum(p, axis=0, keepdims=True)
            acc_ref[...] = pv
        else:
            alpha = jnp.exp(m_old - m_new)
            l_ref[...] = alpha * l_ref[...] + jnp.sum(p, axis=0, keepdims=True)
            acc_ref[...] = alpha * acc_ref[...] + pv
        m_ref[...] = m_new

    @pl.when(s == 0)
    def _():
        for g in range(NSA_GROUPS):
            kp = _dot(pool_ref[0, :, g, :].astype(BF16), wk_ref[...])
            kp = (kp * lax.rsqrt(jnp.mean(kp * kp, axis=-1, keepdims=True) + EPS)) * gk_ref[...]
            kc_ref[:, g * dh:(g + 1) * dh] = kp.astype(BF16)
            vp = _dot(pool_ref[0, :, NSA_GROUPS + g, :].astype(BF16), wv_ref[...])
            for c in range(n_cmp // LANES):
                vct_ref[g * dh:(g + 1) * dh, c * LANES:(c + 1) * LANES] = vp[c * LANES:(c + 1) * LANES, :].T.astype(BF16)
        cb = cbias_ref[...]
        sc = _dot(kc_ref[...], qbd) + cb
        pc = jnp.exp(sc - jnp.max(sc, axis=0, keepdims=True))
        pc = pc / jnp.sum(pc, axis=0, keepdims=True)
        pc = pc * jnp.where(cb > 0.5 * NEG_INF, 1.0, 0.0)
        oc_ref[...] = _dot(vct_ref[...], pc.astype(BF16))
        at = at_ref[...]
        p1, p2, p3 = _split3(pc)
        u = (_dot(at, p1) + _dot(at, p2)) + _dot(at, p3)
        gs = gsum_ref[...]
        u1, u2, u3 = _split3(u)
        imp = (_dot(u1, gs) + _dot(u2, gs)) + _dot(u3, gs)
        jdx = lax.broadcasted_iota(jnp.int32, (n_rows, lanes), 0)
        tok = lax.broadcasted_iota(jnp.int32, (n_rows, lanes), 1) % n_tok
        cur = (pos0 + tok) // SLC_BLOCK
        n_blocks = (pos0 + n_tok + SLC_BLOCK - 1) // SLC_BLOCK
        forced = (jnp.where(jdx == 0, 1.0, 0.0) + jnp.where(jdx == cur, 1.0, 0.0)
                  + jnp.where(jdx == cur - 1, 1.0, 0.0))
        score = jnp.where(forced > 0.0, FORCE_SCORE, jnp.where(jdx <= cur, imp, -1.0))
        score = jnp.where(jdx < n_blocks, score, -2.0)
        score_ref[...] = score

        def rank_body(r, rank):
            row = jnp.broadcast_to(score_ref[pl.ds(r, 1), :], (n_rows, lanes))
            gt = jnp.where(row > score, 1.0, 0.0)
            ge = jnp.where(row >= score, 1.0, 0.0)
            return rank + jnp.where(jdx > r, ge, gt)

        rank = lax.fori_loop(0, n_blocks, rank_body, jnp.zeros((n_rows, lanes), F32))
        sel = jnp.where(rank < float(SLC_TOPN), 1.0, 0.0).astype(BF16)
        sel_ref[...] = _dot(sel, gexp_ref[...])
        new = new_ref[0]
        sc = _dot(new[:, :gd].astype(BF16), qbd) + nbias_ref[...]
        rows = jnp.broadcast_to(sel_ref[pl.ds(n_blocks - 1, 1), :], (PAGE_SIZE, lanes))
        masked_update(sc, rows, new[:, gd:], True)

    @pl.when(s > 0)
    def _():
        p = s - 1
        k_pg = jnp.concatenate([page_ref[0, :, g, :] for g in range(NSA_GROUPS)], axis=1)
        v_pg = jnp.concatenate([page_ref[0, :, NSA_GROUPS + g, :] for g in range(NSA_GROUPS)], axis=1)
        bias = jnp.where(p == n_pages - 1, lbias_ref[...], jnp.broadcast_to(far_ref[...], (PAGE_SIZE, lanes)))
        sc = _dot(k_pg.astype(BF16), qbd) + bias
        half = PAGE_SIZE // 2
        rows = jnp.concatenate([jnp.broadcast_to(sel_ref[pl.ds(2 * p, 1), :], (half, lanes)),
                                jnp.broadcast_to(sel_ref[pl.ds(2 * p + 1, 1), :], (half, lanes))], axis=0)
        masked_update(sc, rows, v_pg, False)

    @pl.when(s == n_pages)
    def _():
        o_s = acc_ref[...] / l_ref[...]
        win = win_ref[0]
        sw = _dot(win[:, :gd].astype(BF16), qbd) + wbias_ref[...]
        pw = jnp.exp(sw - jnp.max(sw, axis=0, keepdims=True))
        pw = pw / jnp.sum(pw, axis=0, keepdims=True)
        o_w = _dot(kv_t(win[:, gd:]), pw.astype(BF16))
        gates = jax.nn.sigmoid(gate_ref[0])
        o_t = (gates[0:1, :] * oc_ref[...] + gates[1:2, :] * o_s) + gates[2:3, :] * o_w
        for g in range(NSA_GROUPS):
            blk = o_t[g * dh:(g + 1) * dh, :].T
            for hh in range(NSA_HPG):
                h = g * NSA_HPG + hh
                o_ref[0, :, h * dh:(h + 1) * dh] = blk[h * n_tok:(h + 1) * n_tok, :]


def nsa_decode(main, tail, pooled, cache, page_table, win_state, pw, bsz, n_tok):
    n_pages = page_table.shape[1]
    pos0 = n_pages * PAGE_SIZE
    wb = win_state.shape[1]
    dh = NSA_HEAD_DIM
    gd = NSA_GROUPS * dh
    lanes = NSA_HEADS * n_tok
    assert lanes == LANES and pos0 % PAGE_SIZE == 0 and n_tok <= SLC_BLOCK
    n_cmp = pooled.shape[1]
    n_blocks = (pos0 + n_tok + SLC_BLOCK - 1) // SLC_BLOCK
    n_rows = -(-n_blocks // 8) * 8
    rel = pw['rel_table']
    m3 = main.reshape(bsz, n_tok, -1)
    q = m3[..., OFF_Q:OFF_KV].reshape(bsz, n_tok, NSA_HEADS, dh)
    grp = np.repeat(np.eye(NSA_GROUPS, dtype=np.float32), NSA_HPG, axis=0)
    qbd = jnp.einsum('bthd,hg->bgdht', q, jnp.asarray(grp)).reshape(bsz, gd, lanes).astype(BF16)
    t_l = np.tile(np.arange(n_tok), NSA_HEADS)[None, :]
    h_l = np.repeat(np.arange(NSA_HEADS), n_tok)

    rows = jnp.repeat(rel, n_tok, axis=1)

    def bias_table(dist, ok):
        return jnp.where(jnp.asarray(ok), _bias_select(rows, _rel_bucket_np(dist)), NEG_INF)

    r_c = np.arange(n_cmp)[:, None]
    dist_c = pos0 + t_l - ((r_c - 1) * CMP_STRIDE + CMP_BLOCK - 1)
    cbias = bias_table(dist_c, (r_c >= 1) & (dist_c >= 0))
    rr = np.arange(PAGE_SIZE)[:, None]
    nbias = bias_table(t_l - rr, (t_l - rr) >= 0)
    lbias = bias_table(PAGE_SIZE + t_l - rr, np.ones((PAGE_SIZE, lanes), bool))
    assert (_rel_bucket_np(np.arange(PAGE_SIZE + 1, 2 * PAGE_SIZE)) == REL_BUCKETS - 1).all()
    far = rows[REL_BUCKETS - 1:, :]
    n_win = -(-(wb + n_tok) // PAGE_SIZE) * PAGE_SIZE
    w_r = np.arange(n_win)[:, None]
    dist_w = pos0 + t_l - (pos0 - wb + w_r)
    wbias = bias_table(dist_w, (dist_w >= 0) & (dist_w <= WINDOW) & (pos0 - wb + w_r >= 0) & (w_r < wb + n_tok))
    ratio = SLC_BLOCK // CMP_STRIDE
    lo = CMP_BLOCK // CMP_STRIDE - 1
    jj = np.arange(n_rows)[:, None]
    nn = np.arange(n_cmp)[None, :] - 1
    a_t = ((nn >= ratio * jj - lo) & (nn <= ratio * jj + ratio - 1) & (nn >= 0) & (jj < n_blocks)).astype(np.float32)
    g_l = h_l // NSA_HPG
    gsum = np.zeros((lanes, lanes), np.float32)
    gsum[np.arange(lanes), g_l * n_tok + t_l[0]] = 1.0
    gexp = gsum.T.copy()
    new_kv = jnp.pad(m3[..., OFF_KV + 2 * NSA_KV:OFF_KV + 4 * NSA_KV], ((0, 0), (0, PAGE_SIZE - n_tok), (0, 0)))
    win_all = jnp.concatenate([win_state, m3[..., OFF_WIN:OFF_WIN + 2 * NSA_KV]], axis=1)
    win_all = jnp.pad(win_all, ((0, 0), (0, n_win - wb - n_tok), (0, 0)))
    gates = tail.reshape(bsz, n_tok, -1)[..., SSM_HEADS:SSM_HEADS + 3 * NSA_HEADS]
    gates = gates.reshape(bsz, n_tok, NSA_HEADS, 3).transpose(0, 3, 2, 1).reshape(bsz, 3, lanes)
    gates = jnp.pad(gates, ((0, 0), (0, 5), (0, 0)))
    const = lambda shape: pl.BlockSpec(shape, lambda b, s, pt: (0,) * len(shape))
    per_b = lambda shape: pl.BlockSpec((1,) + shape, lambda b, s, pt: (b,) + (0,) * len(shape))
    return pl.pallas_call(
        functools.partial(_nsa_decode_body, n_tok=n_tok, pos0=pos0),
        grid_spec=pltpu.PrefetchScalarGridSpec(
            num_scalar_prefetch=1,
            grid=(bsz, n_pages + 1),
            in_specs=[
                per_b((gd, lanes)), per_b((n_cmp, 2 * NSA_GROUPS, dh)), const((dh, dh)), const((dh, dh)),
                const((1, dh)),
                const((n_cmp, lanes)), const((n_rows, n_cmp)), const((lanes, lanes)), const((lanes, lanes)),
                per_b((PAGE_SIZE, 2 * NSA_KV)), const((PAGE_SIZE, lanes)),
                pl.BlockSpec((1, PAGE_SIZE, 2 * NSA_GROUPS, dh),
                             lambda b, s, pt: (pt[b, jnp.maximum(s - 1, 0)], 0, 1, 0)),
                const((PAGE_SIZE, lanes)), const((1, lanes)),
                per_b((n_win, 2 * NSA_KV)), const((n_win, lanes)), per_b((8, lanes)),
            ],
            out_specs=pl.BlockSpec((1, n_tok, NSA_Q), lambda b, s, pt: (b, 0, 0)),
            scratch_shapes=[
                pltpu.VMEM((n_cmp, gd), BF16), pltpu.VMEM((gd, n_cmp), BF16), pltpu.VMEM((gd, lanes), F32),
                pltpu.VMEM((n_rows, lanes), F32), pltpu.VMEM((n_rows, lanes), F32),
                pltpu.VMEM((1, lanes), F32), pltpu.VMEM((1, lanes), F32), pltpu.VMEM((gd, lanes), F32),
            ],
        ),
        out_shape=jax.ShapeDtypeStruct((bsz, n_tok, NSA_Q), F32),
        compiler_params=pltpu.CompilerParams(
            dimension_semantics=("parallel", "arbitrary"), vmem_limit_bytes=VMEM_LIMIT_BYTES),
        name="nsa_decode",
    )(page_table, qbd, pooled, pw['cmp_w'][0].astype(BF16), pw['cmp_w'][1].astype(BF16),
      pw['g_kcmp'].reshape(1, dh), cbias, jnp.asarray(a_t, BF16), jnp.asarray(gsum, BF16), jnp.asarray(gexp, BF16),
      new_kv, nbias, cache, lbias, far, win_all, wbias, gates)


def split_cols(h, sizes):
    offs = np.cumsum(sizes)[:-1].tolist()
    return jnp.split(h, offs, axis=-1)


def even_prompt(x, bsz, seq, wb, pw):
    main, tail = in_proj(x, pw['g_mix'], pw['w_main'], pw['flags'], pw['post_gain'], pw['post_scale'], pw['w_tail'])
    y_a, h_t = ssd_mix(main, tail, jnp.zeros((bsz, CONV_HALO, SSM_CONV_DIM), F32),
                       jnp.zeros((bsz, SSM_HEADS, SSM_STATE, SSM_HEAD_DIM), F32), pw, bsz, seq, seq)
    nb = seq // PAGE_SIZE
    idx = (jnp.arange(bsz, dtype=jnp.int32)[:, None] * nb + jnp.arange(nb, dtype=jnp.int32)[None, :])
    pooled = cmp_pool(main.reshape(bsz * nb, PAGE_SIZE, EVEN_MAIN), idx, OFF_KV // CMP_COLS, pw['cmp_alpha'])
    kcmp, vcmpt = cmp_project(pooled, pw['cmp_w'], pw['g_kcmp'])
    o_b = nsa_prompt(main, tail, kcmp, vcmpt, pw['nsa_tables'], bsz, seq)
    y = out_proj(x, y_a, o_b, pw['w_out_a'], pw['w_out_b'])
    m3 = main.reshape(bsz, seq, EVEN_MAIN)
    kv = m3[..., OFF_KV:OFF_WIN].reshape(bsz, seq, NSA_KV_PARTS, NSA_GROUPS, NSA_HEAD_DIM)
    new_win = m3[:, seq - wb:, OFF_WIN:].reshape(bsz, wb, 2, NSA_GROUPS, NSA_HEAD_DIM)
    new_conv = m3[:, seq - (SSM_CONV - 1):, OFF_XBC:OFF_Q]
    return y, kv, new_win, new_conv, h_t.transpose(0, 1, 3, 2)


def even_decode(x, bsz, n_tok, cache, page_idx, win_state, conv_state, ssm_state, pw):
    wb = win_state.shape[1]
    main, tail = in_proj(x, pw['g_mix'], pw['w_main'], pw['flags'], pw['post_gain'], pw['post_scale'], pw['w_tail'])
    m3 = main.reshape(bsz, n_tok, EVEN_MAIN)
    pad_rows = ((0, 0), (0, SSM_CHUNK - n_tok), (0, 0))
    main_p = jnp.pad(m3[..., :OFF_Q], pad_rows).reshape(bsz * SSM_CHUNK, OFF_Q)
    tail_p = jnp.pad(tail.reshape(bsz, n_tok, EVEN_TAIL), pad_rows).reshape(bsz * SSM_CHUNK, EVEN_TAIL)
    halo = jnp.pad(conv_state, ((0, 0), (CONV_HALO - (SSM_CONV - 1), 0), (0, 0)))
    y_a, h_t = ssd_mix(main_p, tail_p, halo, ssm_state.transpose(0, 1, 3, 2), pw, bsz, SSM_CHUNK, n_tok)
    y_a = y_a.reshape(bsz, SSM_CHUNK, SSM_D_INNER)[:, :n_tok].reshape(bsz * n_tok, SSM_D_INNER)
    pooled = cmp_pool_pages(cache, page_idx, pw['cmp_alpha'])
    win2 = win_state.reshape(bsz, wb, 2 * NSA_KV)
    o_b = nsa_decode(main, tail, pooled, cache, page_idx, win2, pw, bsz, n_tok)
    y = out_proj(x, y_a, o_b.reshape(bsz * n_tok, NSA_Q), pw['w_out_a'], pw['w_out_b'])
    kv = m3[..., OFF_KV:OFF_WIN].reshape(bsz, n_tok, NSA_KV_PARTS, NSA_GROUPS, NSA_HEAD_DIM)
    new_win = jnp.concatenate([win2, m3[..., OFF_WIN:]], axis=1)[:, -wb:]
    new_win = new_win.reshape(bsz, wb, 2, NSA_GROUPS, NSA_HEAD_DIM)
    new_conv = jnp.concatenate([conv_state, m3[..., OFF_XBC:OFF_Q]], axis=1)[:, -(SSM_CONV - 1):]
    return y, kv, new_win, new_conv, h_t.transpose(0, 1, 3, 2)


def odd_prompt(x, bsz, seq, pw):
    h = in_proj(x, pw['g_mix'], pw['w_in'], pw['flags'], pw['post_gain'], pw['post_scale'])
    y_c = pool_mix_pallas(h, jnp.zeros((bsz, POOL_HALO, POOL_DIM), F32), pw['pool_w'], pw['pool_scale'], bsz, seq, 0)
    o = sb_prompt(h, bsz, seq)
    y = out_proj(x, y_c, o, pw['w_out_a'], pw['w_out_b'])
    h3 = h.reshape(bsz, seq, -1)
    kv = h3[..., POOL_DIM + SB_DIM:].reshape(bsz, seq, 2, SB_HEADS, SB_HEAD_DIM)
    return y, kv, h3[:, seq - POOL_BUF:, :POOL_DIM]


def odd_decode(x, bsz, n_tok, cache, page_idx, pool_state, pw):
    pos0 = page_idx.shape[1] * PAGE_SIZE
    h = in_proj(x, pw['g_mix'], pw['w_in'], pw['flags'], pw['post_gain'], pw['post_scale'])
    buf16 = jnp.pad(pool_state, ((0, 0), (POOL_HALO - POOL_BUF, 0), (0, 0)))
    y_c = pool_mix_pallas(h, buf16, pw['pool_w'], pw['pool_scale'], bsz, n_tok, pos0)
    o = sb_decode(h, cache, page_idx, bsz, n_tok)
    y = out_proj(x, y_c, o.reshape(bsz * n_tok, SB_DIM), pw['w_out_a'], pw['w_out_b'])
    h3 = h.reshape(bsz, n_tok, -1)
    kv = h3[..., POOL_DIM + SB_DIM:].reshape(bsz, n_tok, 2, SB_HEADS, SB_HEAD_DIM)
    new_pool = jnp.concatenate([pool_state, h3[..., :POOL_DIM]], axis=1)[:, -POOL_BUF:]
    return y, kv, new_pool


def _even_weights(l, e, mix_norm, w_in_even, w_out_even, ssm_conv_w, ssm_conv_b, ssm_dt_bias, ssm_a_log, ssm_d,
                  ssm_norm, nsa_cmp_alpha, nsa_cmp_w, nsa_qk_gain, rel_bias, nsa_tables, tn):
    w = w_in_even[e]
    sizes = (SSM_D_INNER, SSM_CONV_DIM, SSM_HEADS, NSA_Q) + (NSA_KV,) * 6 + (3 * NSA_HEADS,)
    wz, wxbc, wdt, wq, wkc, wvc, wks, wvs, wkw, wvw, wg = split_cols(w, sizes)
    w_main = jnp.concatenate([wz, wxbc, wq, wkc, wvc, wks, wvs, wkw, wvw], axis=1).astype(BF16)
    pad = EVEN_TAIL - SSM_HEADS - 3 * NSA_HEADS
    w_tail = jnp.concatenate([wdt, wg, jnp.zeros((D_MODEL, pad), F32)], axis=1).astype(BF16)
    gain = nsa_qk_gain[e]
    ones = jnp.ones((NSA_KV,), F32)
    post_gain = jnp.concatenate([
        jnp.ones((OFF_Q,), F32), jnp.tile(gain[0], NSA_HEADS), ones, ones, jnp.tile(gain[2], NSA_GROUPS), ones,
        jnp.tile(gain[3], NSA_GROUPS), ones]).reshape(1, EVEN_MAIN)
    post_scale = jnp.concatenate([
        jnp.ones((OFF_Q,), F32), jnp.full((NSA_Q,), NSA_HEAD_DIM ** -0.5, F32),
        jnp.ones((6 * NSA_KV,), F32)]).reshape(1, EVEN_MAIN)
    flags = np.zeros((EVEN_MAIN // tn,), np.int32)
    for lo_, hi_ in ((OFF_Q, OFF_KV), (OFF_KV + 2 * NSA_KV, OFF_KV + 3 * NSA_KV), (OFF_WIN, OFF_WIN + NSA_KV)):
        assert lo_ % tn == 0 and hi_ % tn == 0
        flags[lo_ // tn:hi_ // tn] = 1
    wo = w_out_even[e].astype(BF16)
    return dict(g_mix=mix_norm[l], w_main=w_main, w_tail=w_tail, flags=jnp.asarray(flags), post_gain=post_gain,
                post_scale=post_scale, conv_w=ssm_conv_w[e], conv_b=ssm_conv_b[e], dt_bias=ssm_dt_bias[e],
                a_log=ssm_a_log[e], d_skip=ssm_d[e], ssm_g=ssm_norm[e], cmp_alpha=nsa_cmp_alpha[e],
                cmp_w=nsa_cmp_w[e], g_kcmp=gain[1], rel_table=rel_bias, nsa_tables=nsa_tables,
                w_out_a=wo[:SSM_D_INNER], w_out_b=wo[SSM_D_INNER:])


def _odd_weights(l, o, mix_norm, w_in_odd, w_out_odd, pool_w, pool_scale, tn):
    n = w_in_odd.shape[2]
    wo = w_out_odd[o].astype(BF16)
    return dict(g_mix=mix_norm[l], w_in=w_in_odd[o].astype(BF16), flags=jnp.zeros((n // tn,), jnp.int32),
                post_gain=jnp.ones((1, n), F32), post_scale=jnp.ones((1, n), F32),
                pool_w=pool_w[o], pool_scale=pool_scale[o], w_out_a=wo[:POOL_DIM], w_out_b=wo[POOL_DIM:])


def kernel(x_prompt, x_sample, cache_nsa_kv, cache_sb_kv, state_nsa_win, state_ssm, state_conv, state_pool, page_table, ffn_norm, ffn_w_gate, ffn_w_up, ffn_w_down, mix_norm, w_in_even, w_out_even, ssm_conv_w, ssm_conv_b, ssm_dt_bias, ssm_a_log, ssm_d, ssm_norm, nsa_cmp_alpha, nsa_cmp_w, nsa_qk_gain, rel_bias, w_in_odd, w_out_odd, pool_w, pool_scale):
    bp, lp, _ = x_prompt.shape
    bs, ls, _ = x_sample.shape
    wb = state_nsa_win.shape[2]
    n_phys = cache_nsa_kv.shape[1]
    tn = 512
    xp = x_prompt.reshape(bp * lp, D_MODEL)
    xs = x_sample.reshape(bs * ls, D_MODEL)
    wg_all = ffn_w_gate.astype(BF16)
    wu_all = ffn_w_up.astype(BF16)
    wd_all = ffn_w_down.astype(BF16)
    nsa_tables = nsa_bias_tables(rel_bias, lp)
    nsa_pages = cache_nsa_kv.reshape(-1, PAGE_SIZE, NSA_KV_PARTS * NSA_GROUPS, NSA_HEAD_DIM)
    sb_pages = cache_sb_kv.reshape(-1, PAGE_SIZE, 2 * SB_HEADS, SB_HEAD_DIM)
    kv_p, kv_s, sb_p, sb_s, win_p, win_s = [], [], [], [], [], []
    ssm_p, ssm_s, conv_p, conv_s, pool_p, pool_s = [], [], [], [], [], []
    for l in range(DEPTH):
        fa = (ffn_norm[l, 0], wg_all[l, 0], wu_all[l, 0], wd_all[l, 0])
        xp = ffn_half(xp, *fa)
        xs = ffn_half(xs, *fa)
        if l % 2 == 0:
            e = l // 2
            pw = _even_weights(l, e, mix_norm, w_in_even, w_out_even, ssm_conv_w, ssm_conv_b, ssm_dt_bias,
                               ssm_a_log, ssm_d, ssm_norm, nsa_cmp_alpha, nsa_cmp_w, nsa_qk_gain, rel_bias, nsa_tables, tn)
            xp, a_kv, a_win, a_conv, a_h = even_prompt(xp, bp, lp, wb, pw)
            xs, b_kv, b_win, b_conv, b_h = even_decode(xs, bs, ls, nsa_pages, page_table + e * n_phys,
                                                       state_nsa_win[e], state_conv[e], state_ssm[e], pw)
            kv_p.append(a_kv); kv_s.append(b_kv)
            win_p.append(a_win); win_s.append(b_win)
            conv_p.append(a_conv); conv_s.append(b_conv)
            ssm_p.append(a_h); ssm_s.append(b_h)
        else:
            o = l // 2
            pw = _odd_weights(l, o, mix_norm, w_in_odd, w_out_odd, pool_w, pool_scale, tn)
            xp, a_kv, a_pool = odd_prompt(xp, bp, lp, pw)
            xs, b_kv, b_pool = odd_decode(xs, bs, ls, sb_pages, page_table + o * n_phys, state_pool[o], pw)
            sb_p.append(a_kv); sb_s.append(b_kv)
            pool_p.append(a_pool); pool_s.append(b_pool)
        fb = (ffn_norm[l, 1], wg_all[l, 1], wu_all[l, 1], wd_all[l, 1])
        xp = ffn_half(xp, *fb)
        xs = ffn_half(xs, *fb)
    return (xp.reshape(bp, lp, D_MODEL), xs.reshape(bs, ls, D_MODEL), jnp.stack(kv_p), jnp.stack(kv_s),
            jnp.stack(sb_p), jnp.stack(sb_s), jnp.stack(win_p), jnp.stack(win_s), jnp.stack(ssm_p), jnp.stack(ssm_s),
            jnp.stack(conv_p), jnp.stack(conv_s), jnp.stack(pool_p), jnp.stack(pool_s))
```

```python
import functools
import math

import jax
import jax.numpy as jnp
import numpy as np
from jax import lax
from jax.experimental import pallas as pl
from jax.experimental.pallas import tpu as pltpu

F32 = jnp.float32
BF16 = jnp.bfloat16

D_MODEL = 2048
DEPTH = 4
PAGE_SIZE = 128
EPS = 1e-6
NEG_INF = -1e30
Q_BLOCK = 128
FFN_RESIDUAL = 0.5
D_FF = 5632

SSM_HEADS = 32
SSM_HEAD_DIM = 64
SSM_D_INNER = SSM_HEADS * SSM_HEAD_DIM
SSM_GROUPS = 4
SSM_STATE = 128
SSM_CONV = 4
SSM_CONV_DIM = SSM_D_INNER + 2 * SSM_GROUPS * SSM_STATE
SSM_CHUNK = 128

NSA_HEADS = 16
NSA_GROUPS = 4
NSA_HPG = NSA_HEADS // NSA_GROUPS
NSA_HEAD_DIM = 128
NSA_Q = NSA_HEADS * NSA_HEAD_DIM
NSA_KV = NSA_GROUPS * NSA_HEAD_DIM
NSA_KV_PARTS = 4
CMP_BLOCK = 32
CMP_STRIDE = 16
SLC_BLOCK = 64
SLC_TOPN = 16
WINDOW = 512
FORCE_SCORE = 1e4
REL_BUCKETS = 32
REL_MAX_DIST = 128
CMP_BAND = 2 * (Q_BLOCK // CMP_STRIDE)
NSA_FAR_BLOCKS = 4

POOL_WINDOWS = (2, 4, 8, 16)
POOL_GROUPS = len(POOL_WINDOWS)
POOL_GROUP_DIM = 256
POOL_DIM = POOL_GROUPS * POOL_GROUP_DIM
POOL_BUF = max(POOL_WINDOWS) - 1

SB_HEADS = 16
SB_HEAD_DIM = 128
SB_DIM = SB_HEADS * SB_HEAD_DIM

LANES = 128
VMEM_LIMIT_BYTES = 56 * 1024 * 1024

EVEN_MAIN = SSM_D_INNER + SSM_CONV_DIM + NSA_Q + 6 * NSA_KV
EVEN_TAIL = LANES
OFF_Z = 0
OFF_XBC = SSM_D_INNER
OFF_Q = OFF_XBC + SSM_CONV_DIM
OFF_KV = OFF_Q + NSA_Q
OFF_WIN = OFF_KV + 4 * NSA_KV


def _row_tile(m, want):
    t = min(m, want)
    assert m % t == 0
    return t


def _ffn_body(x_ref, g_ref, wg_ref, wu_ref, wd_ref, o_ref, h_ref, acc_ref):
    j = pl.program_id(1)

    @pl.when(j == 0)
    def _():
        x = x_ref[...]
        y = x * lax.rsqrt(jnp.mean(x * x, axis=-1, keepdims=True) + EPS)
        h_ref[...] = (y * g_ref[...]).astype(BF16)
        acc_ref[...] = jnp.zeros_like(acc_ref)

    h = h_ref[...]
    a = jnp.dot(h, wg_ref[...], preferred_element_type=F32)
    b = jnp.dot(h, wu_ref[...], preferred_element_type=F32)
    t = (a * jax.nn.sigmoid(a)) * b
    acc_ref[...] += jnp.dot(t.astype(BF16), wd_ref[...], preferred_element_type=F32)

    @pl.when(j == pl.num_programs(1) - 1)
    def _():
        o_ref[...] = x_ref[...] + FFN_RESIDUAL * acc_ref[...]


def ffn_half(x, g, wg, wu, wd, *, tm=512, tf=512):
    m, d = x.shape
    f = wg.shape[1]
    tm = _row_tile(m, tm)
    assert f % tf == 0
    return pl.pallas_call(
        _ffn_body,
        grid=(m // tm, f // tf),
        in_specs=[
            pl.BlockSpec((tm, d), lambda i, j: (i, 0)),
            pl.BlockSpec((1, d), lambda i, j: (0, 0)),
            pl.BlockSpec((d, tf), lambda i, j: (0, j)),
            pl.BlockSpec((d, tf), lambda i, j: (0, j)),
            pl.BlockSpec((tf, d), lambda i, j: (j, 0)),
        ],
        out_specs=pl.BlockSpec((tm, d), lambda i, j: (i, 0)),
        out_shape=jax.ShapeDtypeStruct((m, d), F32),
        scratch_shapes=[pltpu.VMEM((tm, d), BF16), pltpu.VMEM((tm, d), F32)],
        compiler_params=pltpu.CompilerParams(
            dimension_semantics=("parallel", "arbitrary"), vmem_limit_bytes=VMEM_LIMIT_BYTES),
        name="ffn_half",
    )(x, g.reshape(1, d), wg, wu, wd)


def _proj_body(flag_ref, x_ref, g_ref, w_ref, pg_ref, ps_ref, *rest, has_tail):
    if has_tail:
        wt_ref, o_ref, ot_ref, h_ref = rest
    else:
        o_ref, h_ref = rest
    j = pl.program_id(1)

    @pl.when(j == 0)
    def _():
        x = x_ref[...]
        y = x * lax.rsqrt(jnp.mean(x * x, axis=-1, keepdims=True) + EPS)
        h = (y * g_ref[...]).astype(BF16)
        h_ref[...] = h
        if has_tail:
            ot_ref[...] = jnp.dot(h, wt_ref[...], preferred_element_type=F32)

    r = jnp.dot(h_ref[...], w_ref[...], preferred_element_type=F32)

    @pl.when(flag_ref[j] == 0)
    def _():
        o_ref[...] = r

    @pl.when(flag_ref[j] != 0)
    def _():
        tm, tn = r.shape
        for c in range(tn // LANES):
            rc = r[:, c * LANES:(c + 1) * LANES]
            yc = rc * lax.rsqrt(jnp.mean(rc * rc, axis=-1, keepdims=True) + EPS)
            yc = (yc * pg_ref[:, c * LANES:(c + 1) * LANES]) * ps_ref[:, c * LANES:(c + 1) * LANES]
            o_ref[:, c * LANES:(c + 1) * LANES] = yc


def in_proj(x, g, w, flags, post_gain, post_scale, w_tail=None, *, tm=1024, tn=512):
    m, d = x.shape
    n = w.shape[1]
    tm = _row_tile(m, tm)
    assert n % tn == 0
    has_tail = w_tail is not None
    in_specs = [
        pl.BlockSpec((tm, d), lambda i, j, fl: (i, 0)),
        pl.BlockSpec((1, d), lambda i, j, fl: (0, 0)),
        pl.BlockSpec((d, tn), lambda i, j, fl: (0, j)),
        pl.BlockSpec((1, tn), lambda i, j, fl: (0, j)),
        pl.BlockSpec((1, tn), lambda i, j, fl: (0, j)),
    ]
    out_specs = [pl.BlockSpec((tm, tn), lambda i, j, fl: (i, j))]
    out_shape = [jax.ShapeDtypeStruct((m, n), F32)]
    args = [x, g.reshape(1, d), w, post_gain, post_scale]
    if has_tail:
        nt = w_tail.shape[1]
        in_specs.append(pl.BlockSpec((d, nt), lambda i, j, fl: (0, 0)))
        out_specs.append(pl.BlockSpec((tm, nt), lambda i, j, fl: (i, 0)))
        out_shape.append(jax.ShapeDtypeStruct((m, nt), F32))
        args.append(w_tail)
    res = pl.pallas_call(
        functools.partial(_proj_body, has_tail=has_tail),
        grid_spec=pltpu.PrefetchScalarGridSpec(
            num_scalar_prefetch=1,
            grid=(m // tm, n // tn),
            in_specs=in_specs,
            out_specs=out_specs,
            scratch_shapes=[pltpu.VMEM((tm, d), BF16)],
        ),
        out_shape=out_shape,
        compiler_params=pltpu.CompilerParams(
            dimension_semantics=("parallel", "arbitrary"), vmem_limit_bytes=VMEM_LIMIT_BYTES),
        name="in_proj",
    )(flags, *args)
    return res if has_tail else res[0]


def _out_proj_body(x_ref, a1_ref, a2_ref, w1_ref, w2_ref, o_ref):
    r = jnp.dot(a1_ref[...].astype(BF16), w1_ref[...], preferred_element_type=F32)
    r = r + jnp.dot(a2_ref[...].astype(BF16), w2_ref[...], preferred_element_type=F32)
    o_ref[...] = x_ref[...] + r


def out_proj(x, a1, a2, w1, w2, *, tm=1024, tn=512):
    m, d = x.shape
    k1, k2 = a1.shape[1], a2.shape[1]
    tm = _row_tile(m, tm)
    return pl.pallas_call(
        _out_proj_body,
        grid=(m // tm, d // tn),
        in_specs=[
            pl.BlockSpec((tm, tn), lambda i, j: (i, j)),
            pl.BlockSpec((tm, k1), lambda i, j: (i, 0)),
            pl.BlockSpec((tm, k2), lambda i, j: (i, 0)),
            pl.BlockSpec((k1, tn), lambda i, j: (0, j)),
            pl.BlockSpec((k2, tn), lambda i, j: (0, j)),
        ],
        out_specs=pl.BlockSpec((tm, tn), lambda i, j: (i, j)),
        out_shape=jax.ShapeDtypeStruct((m, d), F32),
        compiler_params=pltpu.CompilerParams(
            dimension_semantics=("parallel", "arbitrary"), vmem_limit_bytes=VMEM_LIMIT_BYTES),
        name="out_proj",
    )(x, a1, a2, w1, w2)


def _dot_nt(a, b):
    return lax.dot_general(a, b, (((1,), (1,)), ((), ())), preferred_element_type=F32)


def _dot(a, b):
    return jnp.dot(a, b, preferred_element_type=F32)


def _softplus(z):
    return jnp.maximum(z, 0.0) + jnp.log(1.0 + jnp.exp(-jnp.abs(z)))


SB_KEY_CHUNK = 4 * Q_BLOCK
SB_HEADS_PER_STEP = 4
SB_DEAD_RUN = 110.0


def _sb_prompt_body(q_ref, k_ref, v_ref, tri_ref, o_ref):
    i = pl.program_id(2)
    blk = Q_BLOCK
    ck = SB_KEY_CHUNK
    nsub = ck // blk
    dh = SB_HEAD_DIM
    heads = range(SB_HEADS_PER_STEP)
    qs = [(q_ref[:, h * dh:(h + 1) * dh] * (dh ** -0.5)).astype(BF16) for h in heads]
    tri = tri_ref[...]

    def chunk_head(h, c, run, acc, diag):
        r0 = pl.multiple_of(c * ck, ck)
        z = _dot_nt(qs[h], k_ref[pl.ds(r0, ck), h * dh:(h + 1) * dh].astype(BF16))
        sp = _softplus(z)
        if diag:
            t_pos = i * blk + lax.broadcasted_iota(jnp.int32, (blk, ck), 0)
            s_pos = c * ck + lax.broadcasted_iota(jnp.int32, (blk, ck), 1)
            strict = s_pos < t_pos
            go = jnp.where(strict, sp, 0.0)
        else:
            go = sp
        hi = go.astype(BF16)
        lo = (go - hi.astype(F32)).astype(BF16)
        suffix = [None] * nsub
        for j in reversed(range(nsub)):
            sl = slice(j * blk, (j + 1) * blk)
            ct = _dot(hi[:, sl], tri) + _dot(lo[:, sl], tri)
            suffix[j] = (ct[:, :blk] - go[:, sl]) + run
            run = run + ct[:, blk:]
        att = jnp.exp((z - sp) - jnp.concatenate(suffix, axis=1))
        if diag:
            att = jnp.where(strict, att, 0.0)
        acc = acc + _dot(att.astype(BF16), v_ref[pl.ds(r0, ck), h * dh:(h + 1) * dh].astype(BF16))
        return run, acc

    def chunk(c, carry, diag):
        return tuple(chunk_head(h, c, carry[h][0], carry[h][1], diag) for h in heads)

    cd = i // nsub
    init = tuple((jnp.zeros((blk, blk), F32), jnp.zeros((blk, dh), F32)) for _ in heads)
    carry = chunk(cd, init, True)

    def live(state):
        r, cr = state
        lowest = functools.reduce(jnp.minimum, [jnp.min(cr[h][0]) for h in heads])
        return jnp.logical_and(r < cd, lowest <= SB_DEAD_RUN)

    _, carry = lax.while_loop(live, lambda st: (st[0] + 1, chunk(cd - 1 - st[0], st[1], False)), (jnp.int32(0), carry))
    for h in heads:
        o_ref[:, h * dh:(h + 1) * dh] = carry[h][1]


def sb_prompt(h, bsz, seq):
    qb = seq // Q_BLOCK
    hw = SB_HEADS_PER_STEP * SB_HEAD_DIM
    q_off = POOL_DIM // hw
    k_off = q_off + SB_DIM // hw
    v_off = k_off + SB_DIM // hw
    jj = np.arange(Q_BLOCK)
    tri = np.concatenate([(jj[:, None] >= jj[None, :]).astype(np.float32),
                          np.ones((Q_BLOCK, Q_BLOCK), np.float32)], axis=1)
    return pl.pallas_call(
        _sb_prompt_body,
        grid=(bsz, SB_DIM // hw, qb),
        in_specs=[
            pl.BlockSpec((Q_BLOCK, hw), lambda b, hh, i: (b * qb + i, q_off + hh)),
            pl.BlockSpec((seq, hw), lambda b, hh, i: (b, k_off + hh)),
            pl.BlockSpec((seq, hw), lambda b, hh, i: (b, v_off + hh)),
            pl.BlockSpec((Q_BLOCK, 2 * Q_BLOCK), lambda b, hh, i: (0, 0)),
        ],
        out_specs=pl.BlockSpec((Q_BLOCK, hw), lambda b, hh, i: (b * qb + i, hh)),
        out_shape=jax.ShapeDtypeStruct((bsz * seq, SB_DIM), F32),
        compiler_params=pltpu.CompilerParams(
            dimension_semantics=("parallel", "parallel", "arbitrary"), vmem_limit_bytes=VMEM_LIMIT_BYTES),
        name="sb_prompt",
    )(h, h, h, jnp.asarray(tri, BF16))


CONV_HALO = 8


def _split3(x):
    x1 = x.astype(BF16)
    r = x - x1.astype(F32)
    x2 = r.astype(BF16)
    x3 = (r - x2.astype(F32)).astype(BF16)
    return x1, x2, x3


def _ssd_body(z_ref, x_ref, bc_ref, xh_ref, bch_ref, cv_ref, tail_ref, h0_ref, cw_ref, cb_ref, dtb_r_ref, alog_r_ref,
              dtb_c_ref, alog_c_ref, dsk_ref, ng_ref, tril_ref, y_ref, ht_ref, st_ref, *, n_valid):
    c = pl.program_id(1)
    q = SSM_CHUNK
    p_dim = SSM_HEAD_DIM
    hpg = SSM_HEADS // SSM_GROUPS
    gn = SSM_GROUPS * SSM_STATE

    @pl.when(c == 0)
    def _():
        for h in range(SSM_HEADS):
            st_ref[h] = h0_ref[0, h].T

    first = c == 0
    cv = cv_ref[0]
    xe = jnp.concatenate([jnp.where(first, cv[:, :SSM_D_INNER], xh_ref[...]), x_ref[...]], axis=0)
    bce = jnp.concatenate([jnp.where(first, cv[:, SSM_D_INNER:], bch_ref[...]), bc_ref[...]], axis=0)

    def conv(e, lo, hi):
        acc = cb_ref[:, lo:hi]
        for k in range(SSM_CONV):
            r0 = CONV_HALO - (SSM_CONV - 1) + k
            acc = acc + e[r0:r0 + q, :] * cw_ref[k:k + 1, lo:hi]
        return acc * jax.nn.sigmoid(acc)

    xs = conv(xe, 0, SSM_D_INNER)
    bcm = conv(bce, SSM_D_INNER, SSM_CONV_DIM)

    tail = tail_ref[...]
    t_row = lax.broadcasted_iota(jnp.int32, (q, LANES), 0)
    t_lane = lax.broadcasted_iota(jnp.int32, (LANES, q), 1)
    dt_r = jnp.where(t_row < n_valid, _softplus1p(tail + dtb_r_ref[...]), 0.0)
    dt_c = jnp.where(t_lane < n_valid, _softplus1p(tail.T + dtb_c_ref[...]), 0.0)
    da_r = dt_r * (-jnp.exp(alog_r_ref[...]))
    da_c = dt_c * (-jnp.exp(alog_c_ref[...]))
    tril = tril_ref[...]
    a1, a2, a3 = _split3(da_r)
    acum = (_dot(tril, a1) + _dot(tril, a2)) + _dot(tril, a3)
    c1, c2, c3 = _split3(da_c)
    acum_t = (_dot_nt(c1, tril) + _dot_nt(c2, tril)) + _dot_nt(c3, tril)
    last = acum[q - 1:q, :]
    causal = lax.broadcasted_iota(jnp.int32, (q, q), 1) <= lax.broadcasted_iota(jnp.int32, (q, q), 0)

    for g in range(SSM_GROUPS):
        bg = bcm[:, g * SSM_STATE:(g + 1) * SSM_STATE]
        cg = bcm[:, gn + g * SSM_STATE:gn + (g + 1) * SSM_STATE].astype(BF16)
        cbg = _dot_nt(cg, bg.astype(BF16))
        bg_t = bg.T.astype(BF16)
        for hh in range(hpg):
            h = g * hpg + hh
            col = acum[:, h:h + 1]
            decay = jnp.exp(jnp.where(causal, col - acum_t[h:h + 1, :], NEG_INF))
            xh = xs[:, h * p_dim:(h + 1) * p_dim]
            xdt = xh * dt_r[:, h:h + 1]
            y_diag = _dot((cbg * decay).astype(BF16), xdt.astype(BF16))
            s_t = st_ref[h]
            y_off = _dot(cg, s_t.astype(BF16)) * jnp.exp(col)
            end = last[:, h:h + 1]
            st_ref[h] = s_t * jnp.exp(end) + _dot(bg_t, (xdt * jnp.exp(end - col)).astype(BF16))
            y_ref[:, h * p_dim:(h + 1) * p_dim] = (y_diag + y_off) + dsk_ref[:, h * p_dim:(h + 1) * p_dim] * xh

    z = z_ref[...]
    y = y_ref[...] * (z * jax.nn.sigmoid(z))
    gw = SSM_D_INNER // SSM_GROUPS
    for g in range(SSM_GROUPS):
        yg = y[:, g * gw:(g + 1) * gw]
        yg = yg * lax.rsqrt(jnp.mean(yg * yg, axis=-1, keepdims=True) + EPS)
        y_ref[:, g * gw:(g + 1) * gw] = yg * ng_ref[:, g * gw:(g + 1) * gw]

    @pl.when(c == pl.num_programs(1) - 1)
    def _():
        for h in range(SSM_HEADS):
            ht_ref[0, h] = st_ref[h].T


def _softplus1p(x):
    return jnp.maximum(x, 0.0) + jnp.log1p(jnp.exp(-jnp.abs(x)))


def ssd_mix(main, tail, conv_halo, h0_t, pw, bsz, seq, n_valid):
    q = SSM_CHUNK
    nc = seq // q
    per = q // CONV_HALO
    bcw = SSM_CONV_DIM - SSM_D_INNER
    pad_h = LANES - SSM_HEADS
    dtb = jnp.pad(pw['dt_bias'], (0, pad_h))
    alog = jnp.pad(pw['a_log'], (0, pad_h))
    tt = np.arange(q)
    tril = (tt[None, :] <= tt[:, None]).astype(np.float32)
    halo = lambda b, c: (jnp.maximum((b * nc + c) * per - 1, 0), 0)
    st_shape = (SSM_HEADS, SSM_STATE, SSM_HEAD_DIM)
    io_shape = (SSM_HEADS, SSM_HEAD_DIM, SSM_STATE)
    vec = lambda n: pl.BlockSpec((1, n), lambda b, c: (0, 0))
    colv = pl.BlockSpec((LANES, 1), lambda b, c: (0, 0))
    y, ht = pl.pallas_call(
        functools.partial(_ssd_body, n_valid=n_valid),
        grid=(bsz, nc),
        in_specs=[
            pl.BlockSpec((q, SSM_D_INNER), lambda b, c: (b * nc + c, 0)),
            pl.BlockSpec((q, SSM_D_INNER), lambda b, c: (b * nc + c, 1)),
            pl.BlockSpec((q, bcw), lambda b, c: (b * nc + c, 2 * SSM_D_INNER // bcw)),
            pl.BlockSpec((CONV_HALO, SSM_D_INNER), lambda b, c: (halo(b, c)[0], 1)),
            pl.BlockSpec((CONV_HALO, bcw), lambda b, c: (halo(b, c)[0], 2 * SSM_D_INNER // bcw)),
            pl.BlockSpec((1, CONV_HALO, SSM_CONV_DIM), lambda b, c: (b, 0, 0)),
            pl.BlockSpec((q, LANES), lambda b, c: (b * nc + c, 0)),
            pl.BlockSpec((1,) + io_shape, lambda b, c: (b, 0, 0, 0)),
            pl.BlockSpec((SSM_CONV, SSM_CONV_DIM), lambda b, c: (0, 0)),
            vec(SSM_CONV_DIM), vec(LANES), vec(LANES), colv, colv, vec(SSM_D_INNER), vec(SSM_D_INNER),
            pl.BlockSpec((q, q), lambda b, c: (0, 0)),
        ],
        out_specs=[pl.BlockSpec((q, SSM_D_INNER), lambda b, c: (b * nc + c, 0)),
                   pl.BlockSpec((1,) + io_shape, lambda b, c: (b, 0, 0, 0))],
        out_shape=[jax.ShapeDtypeStruct((bsz * seq, SSM_D_INNER), F32),
                   jax.ShapeDtypeStruct((bsz,) + io_shape, F32)],
        scratch_shapes=[pltpu.VMEM(st_shape, F32)],
        compiler_params=pltpu.CompilerParams(
            dimension_semantics=("parallel", "arbitrary"), vmem_limit_bytes=VMEM_LIMIT_BYTES),
        name="ssd_mix",
    )(main, main, main, main, main, conv_halo, tail, h0_t, pw['conv_w'], pw['conv_b'].reshape(1, -1),
      dtb.reshape(1, LANES), alog.reshape(1, LANES), dtb.reshape(LANES, 1), alog.reshape(LANES, 1),
      jnp.repeat(pw['d_skip'], SSM_HEAD_DIM).reshape(1, -1), pw['ssm_g'].reshape(1, -1), jnp.asarray(tril, BF16))
    return y, ht


POOL_HALO = 16


def _pool_body(u_ref, halo_ref, buf_ref, w_ref, sc_ref, o_ref, *, pos0):
    i = pl.program_id(1)
    tl = u_ref.shape[0]
    u = u_ref[...]
    prev = jnp.where(i == 0, buf_ref[0], halo_ref[...])
    x = jnp.concatenate([prev, u], axis=0)
    pos = pos0 + i * tl + lax.broadcasted_iota(jnp.int32, (tl, 1), 0)
    for g, w in enumerate(POOL_WINDOWS):
        cols = slice(g * POOL_GROUP_DIM, (g + 1) * POOL_GROUP_DIM)
        s = x[:, cols]
        span = 1
        while span < w:
            s = s[span:, :] + s[:-span, :]
            span *= 2
        win = s[POOL_HALO - (w - 1):, :]
        cnt = jnp.minimum(pos + 1, w).astype(F32)
        diff = win / cnt - u[:, cols]
        y = _dot(diff.astype(BF16), w_ref[g])
        o_ref[:, cols] = y * sc_ref[:, cols]


def pool_mix_pallas(h, buf16, pool_w, pool_scale, bsz, seq, pos0, *, tl=512):
    tl = min(tl, seq)
    nt = seq // tl
    per = max(tl // POOL_HALO, 1)
    if tl < POOL_HALO:
        assert nt == 1
        halo_src = buf16.reshape(bsz * POOL_HALO, POOL_DIM)
        halo_map = lambda b, i: (b, 0)
    else:
        halo_src = h
        halo_map = lambda b, i: (jnp.maximum((b * nt + i) * per - 1, 0), 0)
    return pl.pallas_call(
        functools.partial(_pool_body, pos0=pos0),
        grid=(bsz, nt),
        in_specs=[
            pl.BlockSpec((tl, POOL_DIM), lambda b, i: (b * nt + i, 0)),
            pl.BlockSpec((POOL_HALO, POOL_DIM), halo_map),
            pl.BlockSpec((1, POOL_HALO, POOL_DIM), lambda b, i: (b, 0, 0)),
            pl.BlockSpec((POOL_GROUPS, POOL_GROUP_DIM, POOL_GROUP_DIM), lambda b, i: (0, 0, 0)),
            pl.BlockSpec((1, POOL_DIM), lambda b, i: (0, 0)),
        ],
        out_specs=pl.BlockSpec((tl, POOL_DIM), lambda b, i: (b * nt + i, 0)),
        out_shape=jax.ShapeDtypeStruct((bsz * seq, POOL_DIM), F32),
        compiler_params=pltpu.CompilerParams(
            dimension_semantics=("parallel", "arbitrary"), vmem_limit_bytes=VMEM_LIMIT_BYTES),
        name="pool_mix",
    )(h, halo_src, buf16, pool_w.astype(BF16), pool_scale.reshape(1, POOL_DIM))


def _sb_decode_body(pt_ref, qbd_ref, new_ref, page_ref, tri_ref, run0_ref, acc0_ref, o_ref, run_ref, acc_ref, *,
                    n_tok, fresh):
    p = pl.program_id(1)
    rows = SB_HEADS * n_tok
    tri = tri_ref[...]
    qbd = qbd_ref[0]

    def slabs(ref, part):
        return jnp.concatenate([ref[0, :, part * SB_HEADS + h, :] for h in range(SB_HEADS)], axis=1).astype(BF16)

    def page_update(k_all, v_all, masked):
        z = _dot_nt(qbd, k_all)
        sp = _softplus(z)
        if masked:
            tok = lax.broadcasted_iota(jnp.int32, (rows, PAGE_SIZE), 0) % n_tok
            key = lax.broadcasted_iota(jnp.int32, (rows, PAGE_SIZE), 1)
            strict = key < tok
            go = jnp.where(strict, sp, 0.0)
        else:
            go = sp
        hi = go.astype(BF16)
        lo = (go - hi.astype(F32)).astype(BF16)
        ct = _dot(hi, tri) + _dot(lo, tri)
        run = run_ref[0]
        att = jnp.exp((z - sp) - ((ct[:, :PAGE_SIZE] - go) + run))
        if masked:
            att = jnp.where(strict, att, 0.0)
        run_ref[0] = run + ct[:, PAGE_SIZE:]
        acc_ref[0] += _dot(att.astype(BF16), v_all)

    @pl.when(p == 0)
    def _():
        run_ref[...] = run0_ref[...]
        acc_ref[...] = acc0_ref[...]
        if fresh:
            page_update(new_ref[0, :, :SB_DIM].astype(BF16), new_ref[0, :, SB_DIM:].astype(BF16), True)

    @pl.when(jnp.min(run_ref[...]) <= SB_DEAD_RUN)
    def _():
        page_update(slabs(page_ref, 0), slabs(page_ref, 1), False)

    @pl.when(p == pl.num_programs(1) - 1)
    def _():
        for h in range(SB_HEADS):
            o_ref[0, :, h * SB_HEAD_DIM:(h + 1) * SB_HEAD_DIM] = (
                acc_ref[0, h * n_tok:(h + 1) * n_tok, h * SB_HEAD_DIM:(h + 1) * SB_HEAD_DIM])


SB_FIRST_PAGES = 4


def sb_decode(h, cache, page_table, bsz, n_tok):
    n_pages = page_table.shape[1]
    rows = SB_HEADS * n_tok
    assert rows == PAGE_SIZE and n_tok <= PAGE_SIZE
    h3 = h.reshape(bsz, n_tok, -1)
    q = h3[..., POOL_DIM:POOL_DIM + SB_DIM].reshape(bsz, n_tok, SB_HEADS, SB_HEAD_DIM) * (SB_HEAD_DIM ** -0.5)
    eye = jnp.eye(SB_HEADS, dtype=F32)
    qbd = jnp.einsum('bthd,hg->bhtgd', q, eye).reshape(bsz, rows, SB_DIM).astype(BF16)
    new_kv = jnp.pad(h3[..., POOL_DIM + SB_DIM:], ((0, 0), (0, PAGE_SIZE - n_tok), (0, 0)))
    jj = np.arange(PAGE_SIZE)
    tri = jnp.asarray(np.concatenate([(jj[:, None] >= jj[None, :]).astype(np.float32),
                                      np.ones((PAGE_SIZE, PAGE_SIZE), np.float32)], axis=1), BF16)
    newest_first = page_table[:, ::-1]

    def walk(pages, run0, acc0, fresh):
        n = pages.shape[1]
        state = lambda w: pl.BlockSpec((1, rows, w), lambda b, p, pt: (b, 0, 0))
        return pl.pallas_call(
            functools.partial(_sb_decode_body, n_tok=n_tok, fresh=fresh),
            grid_spec=pltpu.PrefetchScalarGridSpec(
                num_scalar_prefetch=1,
                grid=(bsz, n),
                in_specs=[
                    state(SB_DIM),
                    pl.BlockSpec((1, PAGE_SIZE, 2 * SB_DIM), lambda b, p, pt: (b, 0, 0)),
                    pl.BlockSpec((1, PAGE_SIZE, 2 * SB_HEADS, SB_HEAD_DIM), lambda b, p, pt: (pt[b, p], 0, 0, 0)),
                    pl.BlockSpec((PAGE_SIZE, 2 * PAGE_SIZE), lambda b, p, pt: (0, 0)),
                    state(PAGE_SIZE), state(SB_DIM),
                ],
                out_specs=[pl.BlockSpec((1, n_tok, SB_DIM), lambda b, p, pt: (b, 0, 0)),
                           state(PAGE_SIZE), state(SB_DIM)],
            ),
            out_shape=[jax.ShapeDtypeStruct((bsz, n_tok, SB_DIM), F32),
                       jax.ShapeDtypeStruct((bsz, rows, PAGE_SIZE), F32),
                       jax.ShapeDtypeStruct((bsz, rows, SB_DIM), F32)],
            compiler_params=pltpu.CompilerParams(
                dimension_semantics=("parallel", "arbitrary"), vmem_limit_bytes=VMEM_LIMIT_BYTES),
            name="sb_decode",
        )(pages, qbd, new_kv, cache, tri, run0, acc0)

    k0 = min(SB_FIRST_PAGES, n_pages)
    o, run, acc = walk(newest_first[:, :k0], jnp.zeros((bsz, rows, PAGE_SIZE), F32),
                       jnp.zeros((bsz, rows, SB_DIM), F32), True)
    if k0 == n_pages:
        return o
    return lax.cond(jnp.min(run) > SB_DEAD_RUN, lambda: o,
                    lambda: walk(newest_first[:, k0:], run, acc, False)[0])


def _rel_bucket_np(dist):
    n = np.maximum(dist, 0)
    exact = REL_BUCKETS // 2
    nf = np.maximum(n, 1).astype(np.float32)
    log_b = exact + (np.log(nf / np.float32(exact)) / np.float32(math.log(REL_MAX_DIST / exact))
                     * np.float32(REL_BUCKETS - exact)).astype(np.int32)
    return np.where(n < exact, n, np.minimum(log_b, REL_BUCKETS - 1)).astype(np.int32)


def _bias_select(rows, bucket):
    ids = jnp.asarray(bucket)
    out = jnp.broadcast_to(rows[REL_BUCKETS - 1], bucket.shape)
    for k in range(REL_BUCKETS - 2, -1, -1):
        out = jnp.where(ids == k, rows[k], out)
    return out


def nsa_bias_tables(rel_bias, seq):
    s = np.arange(Q_BLOCK)
    rows = jnp.repeat(rel_bias.reshape(REL_BUCKETS, NSA_GROUPS, NSA_HPG).transpose(1, 0, 2), Q_BLOCK, axis=-1)
    far_from = (CMP_BAND // 2 + 1) * CMP_STRIDE - (CMP_BLOCK - 1)
    assert (_rel_bucket_np(np.arange(far_from, 2 * seq)) == REL_BUCKETS - 1).all()
    near_idx = _rel_bucket_np(np.arange(2)[:, None, None] * Q_BLOCK + s[None, None, :] - s[None, :, None])
    near_idx = np.tile(near_idx, (1, 1, NSA_HPG))
    near = jnp.stack([_bias_select(rows[g], near_idx) for g in range(NSA_GROUPS)])
    far = rows[:, REL_BUCKETS - 1:, :]
    m_rel = np.arange(-CMP_BAND // 2, CMP_BAND // 2).reshape(CMP_BAND, 1)
    dist_c = s.reshape(1, Q_BLOCK) - (m_rel * CMP_STRIDE + CMP_BLOCK - 1)
    cb_idx = np.tile(_rel_bucket_np(dist_c), (1, NSA_HPG))
    cb_ok = jnp.asarray(np.tile(dist_c >= 0, (1, NSA_HPG)))
    cb = jnp.stack([jnp.where(cb_ok, _bias_select(rows[g], cb_idx), NEG_INF) for g in range(NSA_GROUPS)])
    return near, far, cb


def _nsa_prompt_body(q_ref, tail_ref, ks_ref, vs_ref, kw_ref, vw_ref, kcmp_ref, vcmpt_ref, cb_ref, nb_ref, fb_ref,
                     at_ref, o_ref, vst_ref, vwt_ref, sel_ref, gt_ref, cbs_ref, kwp_ref, *, seq):
    g = pl.program_id(1)
    i = pl.program_id(2)
    blk = Q_BLOCK
    hw = NSA_HPG * blk
    n_slc = seq // SLC_BLOCK

    @pl.when(i == 0)
    def _():
        kwp_ref[0:WINDOW, :] = jnp.zeros((WINDOW, NSA_HEAD_DIM), BF16)
        vwt_ref[:, 0:WINDOW] = jnp.zeros((NSA_HEAD_DIM, WINDOW), BF16)
        for c in range(seq // blk):
            vst_ref[:, c * blk:(c + 1) * blk] = vs_ref[c * blk:(c + 1) * blk, :].T.astype(BF16)
            kwp_ref[WINDOW + c * blk:WINDOW + (c + 1) * blk, :] = kw_ref[c * blk:(c + 1) * blk, :].astype(BF16)
            vwt_ref[:, WINDOW + c * blk:WINDOW + (c + 1) * blk] = vw_ref[c * blk:(c + 1) * blk, :].T.astype(BF16)

    q4 = jnp.concatenate([q_ref[:, h * blk:(h + 1) * blk] for h in range(NSA_HPG)], axis=0).astype(BF16)

    n_pad = seq // CMP_STRIDE
    half = CMP_BAND // 2
    rows = lax.broadcasted_iota(jnp.int32, (n_pad + half, hw), 0)
    cbs_ref[...] = jnp.where(rows < half * i, jnp.broadcast_to(fb_ref[0], (n_pad + half, hw)), NEG_INF)
    cbs_ref[pl.ds(pl.multiple_of(half * i, half), CMP_BAND), :] = cb_ref[0]
    cb = cbs_ref[half:, :]
    sc = _dot_nt(kcmp_ref[0, 0].astype(BF16), q4) + cb
    mc = jnp.max(sc, axis=0, keepdims=True)
    pc = jnp.exp(sc - mc)
    pc = pc / jnp.sum(pc, axis=0, keepdims=True)
    pc = pc * jnp.where(cb > 0.5 * NEG_INF, 1.0, 0.0)
    o_c = _dot(vcmpt_ref[0, 0].astype(BF16), pc.astype(BF16))
    psum = pc[:, 0:blk]
    for h in range(1, NSA_HPG):
        psum = psum + pc[:, h * blk:(h + 1) * blk]
    p1 = psum.astype(BF16)
    r1 = psum - p1.astype(F32)
    p2 = r1.astype(BF16)
    p3 = (r1 - p2.astype(F32)).astype(BF16)
    at = at_ref[...]
    imp = (_dot(at, p1) + _dot(at, p2)) + _dot(at, p3)

    jdx = lax.broadcasted_iota(jnp.int32, (n_slc, blk), 0)
    tpos = i * blk + lax.broadcasted_iota(jnp.int32, (n_slc, blk), 1)
    cur = tpos // SLC_BLOCK
    forced = jnp.where(jdx == 0, 1.0, 0.0) + jnp.where(jdx == cur, 1.0, 0.0) + jnp.where(jdx == cur - 1, 1.0, 0.0)
    score = jnp.where(forced > 0.0, FORCE_SCORE, jnp.where(jdx <= cur, imp, -1.0))
    rank = jnp.zeros((n_slc, blk), F32)
    for r in range(n_slc):
        row = jnp.broadcast_to(score[r:r + 1, :], (n_slc, blk))
        gt = jnp.where(row > score, 1.0, 0.0)
        ge = jnp.where(row >= score, 1.0, 0.0)
        rank = rank + jnp.where(jdx > r, ge, gt)
    sel_ref[...] = jnp.where(rank < float(min(SLC_TOPN, n_slc)), 1.0, 0.0)

    ss = lax.broadcasted_iota(jnp.int32, (blk, blk), 0)
    tt = lax.broadcasted_iota(jnp.int32, (blk, blk), 1)
    causal = jnp.where(ss <= tt, 1.0, 0.0)
    anti = jnp.where(ss >= tt, 1.0, 0.0)
    bias_far = fb_ref[0]

    def tile4(mk):
        return jnp.concatenate([mk] * NSA_HPG, axis=1)

    def scores(k_ref, kb, bias):
        r0 = pl.multiple_of(kb * blk, blk)
        return _dot_nt(k_ref[pl.ds(r0, blk), :].astype(BF16), q4) + bias

    def pv(vt_ref, kb, p):
        r0 = pl.multiple_of(kb * blk, blk)
        return _dot(vt_ref[:, pl.ds(r0, blk)], p.astype(BF16))

    def first(s, mk):
        s = jnp.where(tile4(mk) > 0.0, s, NEG_INF)
        m = jnp.max(s, axis=0, keepdims=True)
        p = jnp.exp(s - m)
        return m, jnp.sum(p, axis=0, keepdims=True), p

    def update(carry, s, mk, vt_ref, kb):
        m, l, acc = carry
        s = jnp.where(tile4(mk) > 0.0, s, NEG_INF)
        m_new = jnp.maximum(m, jnp.max(s, axis=0, keepdims=True))
        alpha = jnp.exp(m - m_new)
        p = jnp.exp(s - m_new)
        return m_new, alpha * l + jnp.sum(p, axis=0, keepdims=True), alpha * acc + pv(vt_ref, kb, p)

    def sel_mask(kb):
        r0 = sel_ref[pl.ds(2 * kb, 1), :]
        r1 = sel_ref[pl.ds(2 * kb + 1, 1), :]
        half = blk // 2
        return jnp.concatenate([jnp.broadcast_to(r0, (half, blk)), jnp.broadcast_to(r1, (half, blk))], axis=0)

    m, l, p = first(scores(ks_ref, i, nb_ref[0, 0]), sel_mask(i) * causal)
    carry = (m, l, pv(vst_ref, i, p))
    kb1 = jnp.maximum(i - 1, 0)
    ok1 = jnp.where(i >= 1, 1.0, 0.0)
    carry = update(carry, scores(ks_ref, kb1, nb_ref[0, 1]), sel_mask(kb1) * ok1, vst_ref, kb1)

    nfar = NSA_FAR_BLOCKS

    def far_body(r, c):
        m, l, acc = c
        r0 = pl.multiple_of(r * nfar * blk, nfar * blk)
        s = _dot_nt(ks_ref[pl.ds(r0, nfar * blk), :].astype(BF16), q4) + bias_far
        mk = jnp.concatenate([sel_mask(nfar * r + j) * jnp.where(nfar * r + j <= i - 2, 1.0, 0.0)
                              for j in range(nfar)], axis=0)
        s = jnp.where(tile4(mk) > 0.0, s, NEG_INF)
        m_new = jnp.maximum(m, jnp.max(s, axis=0, keepdims=True))
        alpha = jnp.exp(m - m_new)
        p = jnp.exp(s - m_new)
        pvv = _dot(vst_ref[:, pl.ds(r0, nfar * blk)], p.astype(BF16))
        return m_new, alpha * l + jnp.sum(p, axis=0, keepdims=True), alpha * acc + pvv

    m, l, acc = lax.fori_loop(0, (i + nfar - 2) // nfar, far_body, carry)
    o_s = acc / l

    nwb = WINDOW // blk
    w0 = pl.multiple_of(i * blk, blk)
    ones = jnp.ones((blk, blk), F32)
    bias_w = jnp.concatenate([jnp.broadcast_to(bias_far, ((nwb - 1) * blk, hw)), nb_ref[0, 1], nb_ref[0, 0]], axis=0)
    mask_w = jnp.concatenate(
        [(anti if d == nwb else ones) * jnp.where(i >= d, 1.0, 0.0) for d in range(nwb, 0, -1)] + [causal], axis=0)
    sw = _dot_nt(kwp_ref[pl.ds(w0, WINDOW + blk), :], q4) + bias_w
    sw = jnp.where(tile4(mask_w) > 0.0, sw, NEG_INF)
    pw = jnp.exp(sw - jnp.max(sw, axis=0, keepdims=True))
    o_w = _dot(vwt_ref[:, pl.ds(w0, WINDOW + blk)], pw.astype(BF16)) / jnp.sum(pw, axis=0, keepdims=True)

    gt_ref[...] = tail_ref[...].T

    def gate(c):
        rows = [gt_ref[pl.ds(SSM_HEADS + 3 * (NSA_HPG * g + h) + c, 1), :] for h in range(NSA_HPG)]
        return jax.nn.sigmoid(jnp.concatenate(rows, axis=1))

    o_t = (gate(0) * o_c + gate(1) * o_s) + gate(2) * o_w
    for h in range(NSA_HPG):
        o_ref[:, h * blk:(h + 1) * blk] = o_t[:, h * blk:(h + 1) * blk].T


def nsa_prompt(main, tail, kcmp, vcmpt, tables, bsz, seq):
    near, far, cb = tables
    qb = seq // Q_BLOCK
    n_pad = seq // CMP_STRIDE
    n_slc = seq // SLC_BLOCK
    hw = NSA_HPG * Q_BLOCK
    ratio = SLC_BLOCK // CMP_STRIDE
    lo = CMP_BLOCK // CMP_STRIDE - 1
    jj = np.arange(n_slc)[:, None]
    nn = np.arange(n_pad)[None, :]
    a_t = ((nn >= ratio * jj - lo) & (nn <= ratio * jj + ratio - 1) & (nn < n_pad - 1)).astype(np.float32)
    dh = NSA_HEAD_DIM
    col = lambda off: off // dh
    return pl.pallas_call(
        functools.partial(_nsa_prompt_body, seq=seq),
        grid=(bsz, NSA_GROUPS, qb),
        in_specs=[
            pl.BlockSpec((Q_BLOCK, hw), lambda b, g, i: (b * qb + i, OFF_Q // hw + g)),
            pl.BlockSpec((Q_BLOCK, EVEN_TAIL), lambda b, g, i: (b * qb + i, 0)),
            pl.BlockSpec((seq, dh), lambda b, g, i: (b, col(OFF_KV + 2 * NSA_KV) + g)),
            pl.BlockSpec((seq, dh), lambda b, g, i: (b, col(OFF_KV + 3 * NSA_KV) + g)),
            pl.BlockSpec((seq, dh), lambda b, g, i: (b, col(OFF_WIN) + g)),
            pl.BlockSpec((seq, dh), lambda b, g, i: (b, col(OFF_WIN + NSA_KV) + g)),
            pl.BlockSpec((1, 1, n_pad, dh), lambda b, g, i: (b, g, 0, 0)),
            pl.BlockSpec((1, 1, dh, n_pad), lambda b, g, i: (b, g, 0, 0)),
            pl.BlockSpec((1, CMP_BAND, hw), lambda b, g, i: (g, 0, 0)),
            pl.BlockSpec((1, 2, Q_BLOCK, hw), lambda b, g, i: (g, 0, 0, 0)),
            pl.BlockSpec((1, 1, hw), lambda b, g, i: (g, 0, 0)),
            pl.BlockSpec((n_slc, n_pad), lambda b, g, i: (0, 0)),
        ],
        out_specs=pl.BlockSpec((Q_BLOCK, hw), lambda b, g, i: (b * qb + i, g)),
        out_shape=jax.ShapeDtypeStruct((bsz * seq, NSA_Q), F32),
        scratch_shapes=[pltpu.VMEM((dh, seq), BF16), pltpu.VMEM((dh, seq + WINDOW), BF16),
                        pltpu.VMEM((n_slc, Q_BLOCK), F32), pltpu.VMEM((EVEN_TAIL, Q_BLOCK), F32),
                        pltpu.VMEM((n_pad + CMP_BAND // 2, hw), F32), pltpu.VMEM((seq + WINDOW, dh), BF16)],
        compiler_params=pltpu.CompilerParams(
            dimension_semantics=("parallel", "parallel", "arbitrary"), vmem_limit_bytes=VMEM_LIMIT_BYTES),
        name="nsa_prompt",
    )(main, tail, main, main, main, main, kcmp, vcmpt, cb, near, far, jnp.asarray(a_t, BF16))


CMP_COLS = 2 * NSA_KV
SUBS_PER_PAGE = PAGE_SIZE // CMP_STRIDE


def _cmp_pool_body(idx_ref, src_ref, a1_ref, a2_ref, o_ref, prev_ref):
    p = pl.program_id(1)
    x = src_ref[0]
    first = (x * a1_ref[...]).reshape(SUBS_PER_PAGE, CMP_STRIDE, CMP_COLS).sum(axis=1)
    second = (x * a2_ref[...]).reshape(SUBS_PER_PAGE, CMP_STRIDE, CMP_COLS).sum(axis=1)
    prev = jnp.where(p == 0, 0.0, prev_ref[...])
    shifted = jnp.concatenate([prev[SUBS_PER_PAGE - 1:, :], first[:SUBS_PER_PAGE - 1, :]], axis=0)
    o_ref[0] = shifted + second
    prev_ref[...] = first


CMP_PAGES_PER_STEP = 4


def _cmp_pool_slab_body(idx_ref, *refs):
    src_refs = refs[:CMP_PAGES_PER_STEP]
    a1_ref, a2_ref, o_ref, prev_ref = refs[CMP_PAGES_PER_STEP:]
    p = pl.program_id(1)
    prev = jnp.where(p == 0, 0.0, prev_ref[...])
    for j, src_ref in enumerate(src_refs):
        x = src_ref[0].reshape(SUBS_PER_PAGE, CMP_STRIDE, 2 * NSA_GROUPS, NSA_HEAD_DIM)
        first = (x * a1_ref[...][None]).sum(axis=1)
        second = (x * a2_ref[...][None]).sum(axis=1)
        shifted = jnp.concatenate([prev[SUBS_PER_PAGE - 1:], first[:SUBS_PER_PAGE - 1]], axis=0)
        o_ref[0, j * SUBS_PER_PAGE:(j + 1) * SUBS_PER_PAGE] = shifted + second
        prev = first
    prev_ref[...] = prev


def cmp_pool_pages(cache, page_idx, cmp_alpha):
    bsz, n = page_idx.shape
    pps = CMP_PAGES_PER_STEP
    assert n % pps == 0
    slabs = 2 * NSA_GROUPS
    dh = NSA_HEAD_DIM

    def tiled(half):
        return jnp.concatenate([jnp.repeat(cmp_alpha[0][half][:, None, :], NSA_GROUPS, axis=1),
                                jnp.repeat(cmp_alpha[1][half][:, None, :], NSA_GROUPS, axis=1)], axis=1)

    a1 = tiled(slice(0, CMP_STRIDE))
    a2 = tiled(slice(CMP_STRIDE, CMP_BLOCK))

    def page_spec(j):
        return pl.BlockSpec((1, PAGE_SIZE, slabs, dh), lambda b, p, idx: (idx[b, p * pps + j], 0, 0, 0))

    return pl.pallas_call(
        _cmp_pool_slab_body,
        grid_spec=pltpu.PrefetchScalarGridSpec(
            num_scalar_prefetch=1,
            grid=(bsz, n // pps),
            in_specs=[page_spec(j) for j in range(pps)] + [
                pl.BlockSpec((CMP_STRIDE, slabs, dh), lambda b, p, idx: (0, 0, 0)),
                pl.BlockSpec((CMP_STRIDE, slabs, dh), lambda b, p, idx: (0, 0, 0)),
            ],
            out_specs=pl.BlockSpec((1, pps * SUBS_PER_PAGE, slabs, dh), lambda b, p, idx: (b, p, 0, 0)),
            scratch_shapes=[pltpu.VMEM((SUBS_PER_PAGE, slabs, dh), F32)],
        ),
        out_shape=jax.ShapeDtypeStruct((bsz, n * SUBS_PER_PAGE, slabs, dh), F32),
        compiler_params=pltpu.CompilerParams(
            dimension_semantics=("parallel", "arbitrary"), vmem_limit_bytes=VMEM_LIMIT_BYTES),
        name="cmp_pool_pages",
    )(page_idx, *([cache] * pps), a1, a2)


def cmp_pool(src, block_idx, col_block, cmp_alpha):
    bsz, n = block_idx.shape

    def tiled(half):
        a = jnp.concatenate([jnp.tile(cmp_alpha[0][half], (1, NSA_GROUPS)),
                             jnp.tile(cmp_alpha[1][half], (1, NSA_GROUPS))], axis=1)
        return jnp.tile(a, (SUBS_PER_PAGE, 1))

    a1 = tiled(slice(0, CMP_STRIDE))
    a2 = tiled(slice(CMP_STRIDE, CMP_BLOCK))
    return pl.pallas_call(
        _cmp_pool_body,
        grid_spec=pltpu.PrefetchScalarGridSpec(
            num_scalar_prefetch=1,
            grid=(bsz, n),
            in_specs=[
                pl.BlockSpec((1, PAGE_SIZE, CMP_COLS), lambda b, p, idx: (idx[b, p], 0, col_block)),
                pl.BlockSpec((PAGE_SIZE, CMP_COLS), lambda b, p, idx: (0, 0)),
                pl.BlockSpec((PAGE_SIZE, CMP_COLS), lambda b, p, idx: (0, 0)),
            ],
            out_specs=pl.BlockSpec((1, SUBS_PER_PAGE, CMP_COLS), lambda b, p, idx: (b, p, 0)),
            scratch_shapes=[pltpu.VMEM((SUBS_PER_PAGE, CMP_COLS), F32)],
        ),
        out_shape=jax.ShapeDtypeStruct((bsz, n * SUBS_PER_PAGE, CMP_COLS), F32),
        compiler_params=pltpu.CompilerParams(
            dimension_semantics=("parallel", "arbitrary"), vmem_limit_bytes=VMEM_LIMIT_BYTES),
        name="cmp_pool",
    )(block_idx, src, a1, a2)


def _cmp_project_body(p_ref, wk_ref, wv_ref, gk_ref, k_ref, vt_ref):
    n = p_ref.shape[1]
    dh = NSA_HEAD_DIM
    zero = jnp.zeros((1, dh), F32)
    pk = jnp.concatenate([p_ref[0, 1:, 0:dh], zero], axis=0)
    pv = jnp.concatenate([p_ref[0, 1:, dh:2 * dh], zero], axis=0)
    kp = _dot(pk.astype(BF16), wk_ref[...])
    k_ref[0, 0] = (kp * lax.rsqrt(jnp.mean(kp * kp, axis=-1, keepdims=True) + EPS)) * gk_ref[...]
    vp = _dot(pv.astype(BF16), wv_ref[...])
    for c in range(n // LANES):
        vt_ref[0, 0, :, c * LANES:(c + 1) * LANES] = vp[c * LANES:(c + 1) * LANES, :].T


def cmp_project(pooled, cmp_w, g_kcmp):
    bsz, n, _ = pooled.shape
    dh = NSA_HEAD_DIM
    pg = pooled.reshape(bsz, n, 2, NSA_GROUPS, dh).transpose(0, 3, 1, 2, 4).reshape(bsz * NSA_GROUPS, n, 2 * dh)
    k, vt = pl.pallas_call(
        _cmp_project_body,
        grid=(bsz, NSA_GROUPS),
        in_specs=[
            pl.BlockSpec((1, n, 2 * dh), lambda b, g: (b * NSA_GROUPS + g, 0, 0)),
            pl.BlockSpec((dh, dh), lambda b, g: (0, 0)),
            pl.BlockSpec((dh, dh), lambda b, g: (0, 0)),
            pl.BlockSpec((1, dh), lambda b, g: (0, 0)),
        ],
        out_specs=[pl.BlockSpec((1, 1, n, dh), lambda b, g: (b, g, 0, 0)),
                   pl.BlockSpec((1, 1, dh, n), lambda b, g: (b, g, 0, 0))],
        out_shape=[jax.ShapeDtypeStruct((bsz, NSA_GROUPS, n, dh), F32),
                   jax.ShapeDtypeStruct((bsz, NSA_GROUPS, dh, n), F32)],
        compiler_params=pltpu.CompilerParams(
            dimension_semantics=("parallel", "parallel"), vmem_limit_bytes=VMEM_LIMIT_BYTES),
        name="cmp_project",
    )(pg, cmp_w[0].astype(BF16), cmp_w[1].astype(BF16), g_kcmp.reshape(1, dh))
    return k, vt


def _nsa_decode_body(pt_ref, qbd_ref, pool_ref, wk_ref, wv_ref, gk_ref, cbias_ref, at_ref, gsum_ref, gexp_ref,
                     new_ref, nbias_ref, page_a_ref, page_b_ref, lbias_ref, far_ref, win_ref, wbias_ref, gate_ref, o_ref,
                     kc_ref, vct_ref, oc_ref, score_ref, sel_ref, m_ref, l_ref, acc_ref, *, n_tok, pos0):
    s = pl.program_id(1)
    n_steps = pl.num_programs(1) - 1
    dh = NSA_HEAD_DIM
    gd = NSA_GROUPS * dh
    lanes = NSA_HEADS * n_tok
    n_cmp = pool_ref.shape[1]
    n_rows = score_ref.shape[0]
    qbd = qbd_ref[0]

    def kv_t(v):
        return jnp.concatenate([v[:, g * dh:(g + 1) * dh].T for g in range(NSA_GROUPS)], axis=0).astype(BF16)

    def masked_update(sc, mask_rows, v, init):
        sc = jnp.where(mask_rows > 0.0, sc, NEG_INF)
        m_old = jnp.full((1, lanes), NEG_INF, F32) if init else m_ref[...]
        m_new = jnp.maximum(m_old, jnp.max(sc, axis=0, keepdims=True))
        p = jnp.exp(sc - m_new)
        pv = _dot(kv_t(v), p.astype(BF16))
        if init:
            l_ref[...] = jnp.sum(p, axis=0, keepdims=True)
            acc_ref[...] = pv
        else:
            alpha = jnp.exp(m_old - m_new)
            l_ref[...] = alpha * l_ref[...] + jnp.sum(p, axis=0, keepdims=True)
            acc_ref[...] = alpha * acc_ref[...] + pv
        m_ref[...] = m_new

    @pl.when(s == 0)
    def _():
        for g in range(NSA_GROUPS):
            kp = _dot(pool_ref[0, :, g, :].astype(BF16), wk_ref[...])
            kp = (kp * lax.rsqrt(jnp.mean(kp * kp, axis=-1, keepdims=True) + EPS)) * gk_ref[...]
            kc_ref[:, g * dh:(g + 1) * dh] = kp.astype(BF16)
            vp = _dot(pool_ref[0, :, NSA_GROUPS + g, :].astype(BF16), wv_ref[...])
            for c in range(n_cmp // LANES):
                vct_ref[g * dh:(g + 1) * dh, c * LANES:(c + 1) * LANES] = vp[c * LANES:(c + 1) * LANES, :].T.astype(BF16)
        cb = cbias_ref[...]
        sc = _dot(kc_ref[...], qbd) + cb
        pc = jnp.exp(sc - jnp.max(sc, axis=0, keepdims=True))
        pc = pc / jnp.sum(pc, axis=0, keepdims=True)
        pc = pc * jnp.where(cb > 0.5 * NEG_INF, 1.0, 0.0)
        oc_ref[...] = _dot(vct_ref[...], pc.astype(BF16))
        at = at_ref[...]
        p1, p2, p3 = _split3(pc)
        u = (_dot(at, p1) + _dot(at, p2)) + _dot(at, p3)
        gs = gsum_ref[...]
        u1, u2, u3 = _split3(u)
        imp = (_dot(u1, gs) + _dot(u2, gs)) + _dot(u3, gs)
        jdx = lax.broadcasted_iota(jnp.int32, (n_rows, lanes), 0)
        tok = lax.broadcasted_iota(jnp.int32, (n_rows, lanes), 1) % n_tok
        cur = (pos0 + tok) // SLC_BLOCK
        n_blocks = (pos0 + n_tok + SLC_BLOCK - 1) // SLC_BLOCK
        forced = (jnp.where(jdx == 0, 1.0, 0.0) + jnp.where(jdx == cur, 1.0, 0.0)
                  + jnp.where(jdx == cur - 1, 1.0, 0.0))
        score = jnp.where(forced > 0.0, FORCE_SCORE, jnp.where(jdx <= cur, imp, -1.0))
        score = jnp.where(jdx < n_blocks, score, -2.0)
        score_ref[...] = score

        def rank_body(r, rank):
            row = jnp.broadcast_to(score_ref[pl.ds(r, 1), :], (n_rows, lanes))
            gt = jnp.where(row > score, 1.0, 0.0)
            ge = jnp.where(row >= score, 1.0, 0.0)
            return rank + jnp.where(jdx > r, ge, gt)

        rank = lax.fori_loop(0, n_blocks, rank_body, jnp.zeros((n_rows, lanes), F32))
        sel = jnp.where(rank < float(SLC_TOPN), 1.0, 0.0).astype(BF16)
        sel_ref[...] = _dot(sel, gexp_ref[...])
        new = new_ref[0]
        sc = _dot(new[:, :gd].astype(BF16), qbd) + nbias_ref[...]
        rows = jnp.broadcast_to(sel_ref[pl.ds(n_blocks - 1, 1), :], (PAGE_SIZE, lanes))
        masked_update(sc, rows, new[:, gd:], True)

    @pl.when(s > 0)
    def _():
        pp = s - 1

        def slabs(part):
            return jnp.concatenate(
                [jnp.concatenate([ref[0, :, part * NSA_GROUPS + g, :] for g in range(NSA_GROUPS)], axis=1)
                 for ref in (page_a_ref, page_b_ref)], axis=0)

        far = jnp.broadcast_to(far_ref[...], (PAGE_SIZE, lanes))
        bias = jnp.concatenate([far, jnp.where(pp == n_steps - 1, lbias_ref[...], far)], axis=0)
        sc = _dot(slabs(0).astype(BF16), qbd) + bias
        per_page = PAGE_SIZE // SLC_BLOCK
        rows = jnp.concatenate([jnp.broadcast_to(sel_ref[pl.ds(2 * per_page * pp + j, 1), :], (SLC_BLOCK, lanes))
                                for j in range(2 * per_page)], axis=0)
        masked_update(sc, rows, slabs(1), False)

    @pl.when(s == n_steps)
    def _():
        o_s = acc_ref[...] / l_ref[...]
        win = win_ref[0]
        sw = _dot(win[:, :gd].astype(BF16), qbd) + wbias_ref[...]
        pw = jnp.exp(sw - jnp.max(sw, axis=0, keepdims=True))
        pw = pw / jnp.sum(pw, axis=0, keepdims=True)
        o_w = _dot(kv_t(win[:, gd:]), pw.astype(BF16))
        gates = jax.nn.sigmoid(gate_ref[0])
        o_t = (gates[0:1, :] * oc_ref[...] + gates[1:2, :] * o_s) + gates[2:3, :] * o_w
        for g in range(NSA_GROUPS):
            blk = o_t[g * dh:(g + 1) * dh, :].T
            for hh in range(NSA_HPG):
                h = g * NSA_HPG + hh
                o_ref[0, :, h * dh:(h + 1) * dh] = blk[h * n_tok:(h + 1) * n_tok, :]


def nsa_decode(main, tail, pooled, cache, page_table, win_state, pw, bsz, n_tok):
    n_pages = page_table.shape[1]
    pos0 = n_pages * PAGE_SIZE
    wb = win_state.shape[1]
    dh = NSA_HEAD_DIM
    gd = NSA_GROUPS * dh
    lanes = NSA_HEADS * n_tok
    assert lanes == LANES and pos0 % PAGE_SIZE == 0 and n_tok <= SLC_BLOCK and n_pages % 2 == 0
    n_cmp = pooled.shape[1]
    n_blocks = (pos0 + n_tok + SLC_BLOCK - 1) // SLC_BLOCK
    n_rows = -(-n_blocks // 8) * 8
    rel = pw['rel_table']
    m3 = main.reshape(bsz, n_tok, -1)
    q = m3[..., OFF_Q:OFF_KV].reshape(bsz, n_tok, NSA_HEADS, dh)
    grp = np.repeat(np.eye(NSA_GROUPS, dtype=np.float32), NSA_HPG, axis=0)
    qbd = jnp.einsum('bthd,hg->bgdht', q, jnp.asarray(grp)).reshape(bsz, gd, lanes).astype(BF16)
    t_l = np.tile(np.arange(n_tok), NSA_HEADS)[None, :]
    h_l = np.repeat(np.arange(NSA_HEADS), n_tok)

    rows = jnp.repeat(rel, n_tok, axis=1)

    def bias_table(dist, ok):
        return jnp.where(jnp.asarray(ok), _bias_select(rows, _rel_bucket_np(dist)), NEG_INF)

    r_c = np.arange(n_cmp)[:, None]
    dist_c = pos0 + t_l - ((r_c - 1) * CMP_STRIDE + CMP_BLOCK - 1)
    cbias = bias_table(dist_c, (r_c >= 1) & (dist_c >= 0))
    rr = np.arange(PAGE_SIZE)[:, None]
    nbias = bias_table(t_l - rr, (t_l - rr) >= 0)
    lbias = bias_table(PAGE_SIZE + t_l - rr, np.ones((PAGE_SIZE, lanes), bool))
    assert (_rel_bucket_np(np.arange(PAGE_SIZE + 1, 2 * PAGE_SIZE)) == REL_BUCKETS - 1).all()
    far = rows[REL_BUCKETS - 1:, :]
    n_win = -(-(wb + n_tok) // PAGE_SIZE) * PAGE_SIZE
    w_r = np.arange(n_win)[:, None]
    dist_w = pos0 + t_l - (pos0 - wb + w_r)
    wbias = bias_table(dist_w, (dist_w >= 0) & (dist_w <= WINDOW) & (pos0 - wb + w_r >= 0) & (w_r < wb + n_tok))
    ratio = SLC_BLOCK // CMP_STRIDE
    lo = CMP_BLOCK // CMP_STRIDE - 1
    jj = np.arange(n_rows)[:, None]
    nn = np.arange(n_cmp)[None, :] - 1
    a_t = ((nn >= ratio * jj - lo) & (nn <= ratio * jj + ratio - 1) & (nn >= 0) & (jj < n_blocks)).astype(np.float32)
    g_l = h_l // NSA_HPG
    gsum = np.zeros((lanes, lanes), np.float32)
    gsum[np.arange(lanes), g_l * n_tok + t_l[0]] = 1.0
    gexp = gsum.T.copy()
    new_kv = jnp.pad(m3[..., OFF_KV + 2 * NSA_KV:OFF_KV + 4 * NSA_KV], ((0, 0), (0, PAGE_SIZE - n_tok), (0, 0)))
    win_all = jnp.concatenate([win_state, m3[..., OFF_WIN:OFF_WIN + 2 * NSA_KV]], axis=1)
    win_all = jnp.pad(win_all, ((0, 0), (0, n_win - wb - n_tok), (0, 0)))
    gates = tail.reshape(bsz, n_tok, -1)[..., SSM_HEADS:SSM_HEADS + 3 * NSA_HEADS]
    gates = gates.reshape(bsz, n_tok, NSA_HEADS, 3).transpose(0, 3, 2, 1).reshape(bsz, 3, lanes)
    gates = jnp.pad(gates, ((0, 0), (0, 5), (0, 0)))
    const = lambda shape: pl.BlockSpec(shape, lambda b, s, pt: (0,) * len(shape))
    per_b = lambda shape: pl.BlockSpec((1,) + shape, lambda b, s, pt: (b,) + (0,) * len(shape))
    return pl.pallas_call(
        functools.partial(_nsa_decode_body, n_tok=n_tok, pos0=pos0),
        grid_spec=pltpu.PrefetchScalarGridSpec(
            num_scalar_prefetch=1,
            grid=(bsz, n_pages // 2 + 1),
            in_specs=[
                per_b((gd, lanes)), per_b((n_cmp, 2 * NSA_GROUPS, dh)), const((dh, dh)), const((dh, dh)),
                const((1, dh)),
                const((n_cmp, lanes)), const((n_rows, n_cmp)), const((lanes, lanes)), const((lanes, lanes)),
                per_b((PAGE_SIZE, 2 * NSA_KV)), const((PAGE_SIZE, lanes)),
                pl.BlockSpec((1, PAGE_SIZE, 2 * NSA_GROUPS, dh),
                             lambda b, s, pt: (pt[b, 2 * jnp.maximum(s - 1, 0)], 0, 1, 0)),
                pl.BlockSpec((1, PAGE_SIZE, 2 * NSA_GROUPS, dh),
                             lambda b, s, pt: (pt[b, 2 * jnp.maximum(s - 1, 0) + 1], 0, 1, 0)),
                const((PAGE_SIZE, lanes)), const((1, lanes)),
                per_b((n_win, 2 * NSA_KV)), const((n_win, lanes)), per_b((8, lanes)),
            ],
            out_specs=pl.BlockSpec((1, n_tok, NSA_Q), lambda b, s, pt: (b, 0, 0)),
            scratch_shapes=[
                pltpu.VMEM((n_cmp, gd), BF16), pltpu.VMEM((gd, n_cmp), BF16), pltpu.VMEM((gd, lanes), F32),
                pltpu.VMEM((n_rows, lanes), F32), pltpu.VMEM((n_rows, lanes), F32),
                pltpu.VMEM((1, lanes), F32), pltpu.VMEM((1, lanes), F32), pltpu.VMEM((gd, lanes), F32),
            ],
        ),
        out_shape=jax.ShapeDtypeStruct((bsz, n_tok, NSA_Q), F32),
        compiler_params=pltpu.CompilerParams(
            dimension_semantics=("parallel", "arbitrary"), vmem_limit_bytes=VMEM_LIMIT_BYTES),
        name="nsa_decode",
    )(page_table, qbd, pooled, pw['cmp_w'][0].astype(BF16), pw['cmp_w'][1].astype(BF16),
      pw['g_kcmp'].reshape(1, dh), cbias, jnp.asarray(a_t, BF16), jnp.asarray(gsum, BF16), jnp.asarray(gexp, BF16),
      new_kv, nbias, cache, cache, lbias, far, win_all, wbias, gates)


def split_cols(h, sizes):
    offs = np.cumsum(sizes)[:-1].tolist()
    return jnp.split(h, offs, axis=-1)


def even_prompt(x, bsz, seq, wb, pw):
    main, tail = in_proj(x, pw['g_mix'], pw['w_main'], pw['flags'], pw['post_gain'], pw['post_scale'], pw['w_tail'])
    y_a, h_t = ssd_mix(main, tail, jnp.zeros((bsz, CONV_HALO, SSM_CONV_DIM), F32),
                       jnp.zeros((bsz, SSM_HEADS, SSM_HEAD_DIM, SSM_STATE), F32), pw, bsz, seq, seq)
    nb = seq // PAGE_SIZE
    idx = (jnp.arange(bsz, dtype=jnp.int32)[:, None] * nb + jnp.arange(nb, dtype=jnp.int32)[None, :])
    pooled = cmp_pool(main.reshape(bsz * nb, PAGE_SIZE, EVEN_MAIN), idx, OFF_KV // CMP_COLS, pw['cmp_alpha'])
    kcmp, vcmpt = cmp_project(pooled, pw['cmp_w'], pw['g_kcmp'])
    o_b = nsa_prompt(main, tail, kcmp, vcmpt, pw['nsa_tables'], bsz, seq)
    y = out_proj(x, y_a, o_b, pw['w_out_a'], pw['w_out_b'])
    m3 = main.reshape(bsz, seq, EVEN_MAIN)
    kv = m3[..., OFF_KV:OFF_WIN].reshape(bsz, seq, NSA_KV_PARTS, NSA_GROUPS, NSA_HEAD_DIM)
    new_win = m3[:, seq - wb:, OFF_WIN:].reshape(bsz, wb, 2, NSA_GROUPS, NSA_HEAD_DIM)
    new_conv = m3[:, seq - (SSM_CONV - 1):, OFF_XBC:OFF_Q]
    return y, kv, new_win, new_conv, h_t


def even_decode(x, bsz, n_tok, cache, page_idx, win_state, conv_state, ssm_state, pw):
    wb = win_state.shape[1]
    main, tail = in_proj(x, pw['g_mix'], pw['w_main'], pw['flags'], pw['post_gain'], pw['post_scale'], pw['w_tail'])
    m3 = main.reshape(bsz, n_tok, EVEN_MAIN)
    pad_rows = ((0, 0), (0, SSM_CHUNK - n_tok), (0, 0))
    main_p = jnp.pad(m3[..., :OFF_Q], pad_rows).reshape(bsz * SSM_CHUNK, OFF_Q)
    tail_p = jnp.pad(tail.reshape(bsz, n_tok, EVEN_TAIL), pad_rows).reshape(bsz * SSM_CHUNK, EVEN_TAIL)
    halo = jnp.pad(conv_state, ((0, 0), (CONV_HALO - (SSM_CONV - 1), 0), (0, 0)))
    y_a, h_t = ssd_mix(main_p, tail_p, halo, ssm_state, pw, bsz, SSM_CHUNK, n_tok)
    y_a = y_a.reshape(bsz, SSM_CHUNK, SSM_D_INNER)[:, :n_tok].reshape(bsz * n_tok, SSM_D_INNER)
    pooled = cmp_pool_pages(cache, page_idx, pw['cmp_alpha'])
    win2 = win_state.reshape(bsz, wb, 2 * NSA_KV)
    o_b = nsa_decode(main, tail, pooled, cache, page_idx, win2, pw, bsz, n_tok)
    y = out_proj(x, y_a, o_b.reshape(bsz * n_tok, NSA_Q), pw['w_out_a'], pw['w_out_b'])
    kv = m3[..., OFF_KV:OFF_WIN].reshape(bsz, n_tok, NSA_KV_PARTS, NSA_GROUPS, NSA_HEAD_DIM)
    new_win = jnp.concatenate([win2, m3[..., OFF_WIN:]], axis=1)[:, -wb:]
    new_win = new_win.reshape(bsz, wb, 2, NSA_GROUPS, NSA_HEAD_DIM)
    new_conv = jnp.concatenate([conv_state, m3[..., OFF_XBC:OFF_Q]], axis=1)[:, -(SSM_CONV - 1):]
    return y, kv, new_win, new_conv, h_t


def odd_prompt(x, bsz, seq, pw):
    h = in_proj(x, pw['g_mix'], pw['w_in'], pw['flags'], pw['post_gain'], pw['post_scale'])
    y_c = pool_mix_pallas(h, jnp.zeros((bsz, POOL_HALO, POOL_DIM), F32), pw['pool_w'], pw['pool_scale'], bsz, seq, 0)
    o = sb_prompt(h, bsz, seq)
    y = out_proj(x, y_c, o, pw['w_out_a'], pw['w_out_b'])
    h3 = h.reshape(bsz, seq, -1)
    kv = h3[..., POOL_DIM + SB_DIM:].reshape(bsz, seq, 2, SB_HEADS, SB_HEAD_DIM)
    return y, kv, h3[:, seq - POOL_BUF:, :POOL_DIM]


def odd_decode(x, bsz, n_tok, cache, page_idx, pool_state, pw):
    pos0 = page_idx.shape[1] * PAGE_SIZE
    h = in_proj(x, pw['g_mix'], pw['w_in'], pw['flags'], pw['post_gain'], pw['post_scale'])
    buf16 = jnp.pad(pool_state, ((0, 0), (POOL_HALO - POOL_BUF, 0), (0, 0)))
    y_c = pool_mix_pallas(h, buf16, pw['pool_w'], pw['pool_scale'], bsz, n_tok, pos0)
    o = sb_decode(h, cache, page_idx, bsz, n_tok)
    y = out_proj(x, y_c, o.reshape(bsz * n_tok, SB_DIM), pw['w_out_a'], pw['w_out_b'])
    h3 = h.reshape(bsz, n_tok, -1)
    kv = h3[..., POOL_DIM + SB_DIM:].reshape(bsz, n_tok, 2, SB_HEADS, SB_HEAD_DIM)
    new_pool = jnp.concatenate([pool_state, h3[..., :POOL_DIM]], axis=1)[:, -POOL_BUF:]
    return y, kv, new_pool


def _even_weights(l, e, mix_norm, w_in_even, w_out_even, ssm_conv_w, ssm_conv_b, ssm_dt_bias, ssm_a_log, ssm_d,
                  ssm_norm, nsa_cmp_alpha, nsa_cmp_w, nsa_qk_gain, rel_bias, nsa_tables, tn):
    w = w_in_even[e]
    sizes = (SSM_D_INNER, SSM_CONV_DIM, SSM_HEADS, NSA_Q) + (NSA_KV,) * 6 + (3 * NSA_HEADS,)
    wz, wxbc, wdt, wq, wkc, wvc, wks, wvs, wkw, wvw, wg = split_cols(w, sizes)
    w_main = jnp.concatenate([wz, wxbc, wq, wkc, wvc, wks, wvs, wkw, wvw], axis=1).astype(BF16)
    pad = EVEN_TAIL - SSM_HEADS - 3 * NSA_HEADS
    w_tail = jnp.concatenate([wdt, wg, jnp.zeros((D_MODEL, pad), F32)], axis=1).astype(BF16)
    gain = nsa_qk_gain[e]
    ones = jnp.ones((NSA_KV,), F32)
    post_gain = jnp.concatenate([
        jnp.ones((OFF_Q,), F32), jnp.tile(gain[0], NSA_HEADS), ones, ones, jnp.tile(gain[2], NSA_GROUPS), ones,
        jnp.tile(gain[3], NSA_GROUPS), ones]).reshape(1, EVEN_MAIN)
    post_scale = jnp.concatenate([
        jnp.ones((OFF_Q,), F32), jnp.full((NSA_Q,), NSA_HEAD_DIM ** -0.5, F32),
        jnp.ones((6 * NSA_KV,), F32)]).reshape(1, EVEN_MAIN)
    flags = np.zeros((EVEN_MAIN // tn,), np.int32)
    for lo_, hi_ in ((OFF_Q, OFF_KV), (OFF_KV + 2 * NSA_KV, OFF_KV + 3 * NSA_KV), (OFF_WIN, OFF_WIN + NSA_KV)):
        assert lo_ % tn == 0 and hi_ % tn == 0
        flags[lo_ // tn:hi_ // tn] = 1
    wo = w_out_even[e].astype(BF16)
    return dict(g_mix=mix_norm[l], w_main=w_main, w_tail=w_tail, flags=jnp.asarray(flags), post_gain=post_gain,
                post_scale=post_scale, conv_w=ssm_conv_w[e], conv_b=ssm_conv_b[e], dt_bias=ssm_dt_bias[e],
                a_log=ssm_a_log[e], d_skip=ssm_d[e], ssm_g=ssm_norm[e], cmp_alpha=nsa_cmp_alpha[e],
                cmp_w=nsa_cmp_w[e], g_kcmp=gain[1], rel_table=rel_bias, nsa_tables=nsa_tables,
                w_out_a=wo[:SSM_D_INNER], w_out_b=wo[SSM_D_INNER:])


def _odd_weights(l, o, mix_norm, w_in_odd, w_out_odd, pool_w, pool_scale, tn):
    n = w_in_odd.shape[2]
    wo = w_out_odd[o].astype(BF16)
    return dict(g_mix=mix_norm[l], w_in=w_in_odd[o].astype(BF16), flags=jnp.zeros((n // tn,), jnp.int32),
                post_gain=jnp.ones((1, n), F32), post_scale=jnp.ones((1, n), F32),
                pool_w=pool_w[o], pool_scale=pool_scale[o], w_out_a=wo[:POOL_DIM], w_out_b=wo[POOL_DIM:])


def kernel(x_prompt, x_sample, cache_nsa_kv, cache_sb_kv, state_nsa_win, state_ssm, state_conv, state_pool, page_table, ffn_norm, ffn_w_gate, ffn_w_up, ffn_w_down, mix_norm, w_in_even, w_out_even, ssm_conv_w, ssm_conv_b, ssm_dt_bias, ssm_a_log, ssm_d, ssm_norm, nsa_cmp_alpha, nsa_cmp_w, nsa_qk_gain, rel_bias, w_in_odd, w_out_odd, pool_w, pool_scale):
    bp, lp, _ = x_prompt.shape
    bs, ls, _ = x_sample.shape
    wb = state_nsa_win.shape[2]
    n_phys = cache_nsa_kv.shape[1]
    tn = 512
    xp = x_prompt.reshape(bp * lp, D_MODEL)
    xs = x_sample.reshape(bs * ls, D_MODEL)
    wg_all = ffn_w_gate.astype(BF16)
    wu_all = ffn_w_up.astype(BF16)
    wd_all = ffn_w_down.astype(BF16)
    nsa_tables = nsa_bias_tables(rel_bias, lp)
    nsa_pages = cache_nsa_kv.reshape(-1, PAGE_SIZE, NSA_KV_PARTS * NSA_GROUPS, NSA_HEAD_DIM)
    sb_pages = cache_sb_kv.reshape(-1, PAGE_SIZE, 2 * SB_HEADS, SB_HEAD_DIM)
    kv_p, kv_s, sb_p, sb_s, win_p, win_s = [], [], [], [], [], []
    ssm_p, ssm_s, conv_p, conv_s, pool_p, pool_s = [], [], [], [], [], []
    for l in range(DEPTH):
        fa = (ffn_norm[l, 0], wg_all[l, 0], wu_all[l, 0], wd_all[l, 0])
        xp = ffn_half(xp, *fa)
        xs = ffn_half(xs, *fa)
        if l % 2 == 0:
            e = l // 2
            pw = _even_weights(l, e, mix_norm, w_in_even, w_out_even, ssm_conv_w, ssm_conv_b, ssm_dt_bias,
                               ssm_a_log, ssm_d, ssm_norm, nsa_cmp_alpha, nsa_cmp_w, nsa_qk_gain, rel_bias, nsa_tables, tn)
            xp, a_kv, a_win, a_conv, a_h = even_prompt(xp, bp, lp, wb, pw)
            xs, b_kv, b_win, b_conv, b_h = even_decode(xs, bs, ls, nsa_pages, page_table + e * n_phys,
                                                       state_nsa_win[e], state_conv[e], state_ssm[e], pw)
            kv_p.append(a_kv); kv_s.append(b_kv)
            win_p.append(a_win); win_s.append(b_win)
            conv_p.append(a_conv); conv_s.append(b_conv)
            ssm_p.append(a_h); ssm_s.append(b_h)
        else:
            o = l // 2
            pw = _odd_weights(l, o, mix_norm, w_in_odd, w_out_odd, pool_w, pool_scale, tn)
            xp, a_kv, a_pool = odd_prompt(xp, bp, lp, pw)
            xs, b_kv, b_pool = odd_decode(xs, bs, ls, sb_pages, page_table + o * n_phys, state_pool[o], pw)
            sb_p.append(a_kv); sb_s.append(b_kv)
            pool_p.append(a_pool); pool_s.append(b_pool)
        fb = (ffn_norm[l, 1], wg_all[l, 1], wu_all[l, 1], wd_all[l, 1])
        xp = ffn_half(xp, *fb)
        xs = ffn_half(xs, *fb)
    return (xp.reshape(bp, lp, D_MODEL), xs.reshape(bs, ls, D_MODEL), jnp.stack(kv_p), jnp.stack(kv_s),
            jnp.stack(sb_p), jnp.stack(sb_s), jnp.stack(win_p), jnp.stack(win_s), jnp.stack(ssm_p), jnp.stack(ssm_s),
            jnp.stack(conv_p), jnp.stack(conv_s), jnp.stack(pool_p), jnp.stack(pool_s))
```

```python
import functools
import math

import jax
import jax.numpy as jnp
import numpy as np
from jax import lax
from jax.experimental import pallas as pl
from jax.experimental.pallas import tpu as pltpu

F32 = jnp.float32
BF16 = jnp.bfloat16

D_MODEL = 2048
DEPTH = 4
PAGE_SIZE = 128
EPS = 1e-6
NEG_INF = -1e30
Q_BLOCK = 128
FFN_RESIDUAL = 0.5
D_FF = 5632

SSM_HEADS = 32
SSM_HEAD_DIM = 64
SSM_D_INNER = SSM_HEADS * SSM_HEAD_DIM
SSM_GROUPS = 4
SSM_STATE = 128
SSM_CONV = 4
SSM_CONV_DIM = SSM_D_INNER + 2 * SSM_GROUPS * SSM_STATE
SSM_CHUNK = 128

NSA_HEADS = 16
NSA_GROUPS = 4
NSA_HPG = NSA_HEADS // NSA_GROUPS
NSA_HEAD_DIM = 128
NSA_Q = NSA_HEADS * NSA_HEAD_DIM
NSA_KV = NSA_GROUPS * NSA_HEAD_DIM
NSA_KV_PARTS = 4
CMP_BLOCK = 32
CMP_STRIDE = 16
SLC_BLOCK = 64
SLC_TOPN = 16
WINDOW = 512
FORCE_SCORE = 1e4
REL_BUCKETS = 32
REL_MAX_DIST = 128
CMP_BAND = 2 * (Q_BLOCK // CMP_STRIDE)
NSA_FAR_BLOCKS = 4

POOL_WINDOWS = (2, 4, 8, 16)
POOL_GROUPS = len(POOL_WINDOWS)
POOL_GROUP_DIM = 256
POOL_DIM = POOL_GROUPS * POOL_GROUP_DIM
POOL_BUF = max(POOL_WINDOWS) - 1

SB_HEADS = 16
SB_HEAD_DIM = 128
SB_DIM = SB_HEADS * SB_HEAD_DIM

LANES = 128
VMEM_LIMIT_BYTES = 56 * 1024 * 1024

EVEN_MAIN = SSM_D_INNER + SSM_CONV_DIM + NSA_Q + 6 * NSA_KV
EVEN_TAIL = LANES
OFF_Z = 0
OFF_XBC = SSM_D_INNER
OFF_Q = OFF_XBC + SSM_CONV_DIM
OFF_KV = OFF_Q + NSA_Q
OFF_WIN = OFF_KV + 4 * NSA_KV


def _row_tile(m, want):
    t = min(m, want)
    assert m % t == 0
    return t


def _ffn_body(x_ref, g_ref, wg_ref, wu_ref, wd_ref, o_ref, h_ref):
    j = pl.program_id(1)

    @pl.when(j == 0)
    def _():
        x = x_ref[...]
        y = x * lax.rsqrt(jnp.mean(x * x, axis=-1, keepdims=True) + EPS)
        h_ref[...] = (y * g_ref[...]).astype(BF16)
        o_ref[...] = jnp.zeros_like(o_ref)

    h = h_ref[...]
    a = jnp.dot(h, wg_ref[...], preferred_element_type=F32)
    b = jnp.dot(h, wu_ref[...], preferred_element_type=F32)
    t = (a * jax.nn.sigmoid(a)) * b
    o_ref[...] += jnp.dot(t.astype(BF16), wd_ref[...], preferred_element_type=F32)

    @pl.when(j == pl.num_programs(1) - 1)
    def _():
        o_ref[...] = x_ref[...] + FFN_RESIDUAL * o_ref[...]


def ffn_half(x, g, wg, wu, wd, *, tm=1024, tf=512):
    m, d = x.shape
    f = wg.shape[1]
    tm = _row_tile(m, tm)
    assert f % tf == 0
    return pl.pallas_call(
        _ffn_body,
        grid=(m // tm, f // tf),
        in_specs=[
            pl.BlockSpec((tm, d), lambda i, j: (i, 0)),
            pl.BlockSpec((1, d), lambda i, j: (0, 0)),
            pl.BlockSpec((d, tf), lambda i, j: (0, j)),
            pl.BlockSpec((d, tf), lambda i, j: (0, j)),
            pl.BlockSpec((tf, d), lambda i, j: (j, 0)),
        ],
        out_specs=pl.BlockSpec((tm, d), lambda i, j: (i, 0)),
        out_shape=jax.ShapeDtypeStruct((m, d), F32),
        scratch_shapes=[pltpu.VMEM((tm, d), BF16)],
        compiler_params=pltpu.CompilerParams(
            dimension_semantics=("parallel", "arbitrary"), vmem_limit_bytes=VMEM_LIMIT_BYTES),
        name="ffn_half",
    )(x, g.reshape(1, d), wg, wu, wd)


def _proj_body(flag_ref, x_ref, g_ref, w_ref, pg_ref, ps_ref, *rest, has_tail):
    if has_tail:
        wt_ref, o_ref, ot_ref, h_ref = rest
    else:
        o_ref, h_ref = rest
    j = pl.program_id(1)

    @pl.when(j == 0)
    def _():
        x = x_ref[...]
        y = x * lax.rsqrt(jnp.mean(x * x, axis=-1, keepdims=True) + EPS)
        h = (y * g_ref[...]).astype(BF16)
        h_ref[...] = h
        if has_tail:
            ot_ref[...] = jnp.dot(h, wt_ref[...], preferred_element_type=F32)

    r = jnp.dot(h_ref[...], w_ref[...], preferred_element_type=F32)

    @pl.when(flag_ref[j] == 0)
    def _():
        o_ref[...] = r

    @pl.when(flag_ref[j] != 0)
    def _():
        tm, tn = r.shape
        for c in range(tn // LANES):
            rc = r[:, c * LANES:(c + 1) * LANES]
            yc = rc * lax.rsqrt(jnp.mean(rc * rc, axis=-1, keepdims=True) + EPS)
            yc = (yc * pg_ref[:, c * LANES:(c + 1) * LANES]) * ps_ref[:, c * LANES:(c + 1) * LANES]
            o_ref[:, c * LANES:(c + 1) * LANES] = yc


def in_proj(x, g, w, flags, post_gain, post_scale, w_tail=None, *, tm=1024, tn=512):
    m, d = x.shape
    n = w.shape[1]
    tm = _row_tile(m, tm)
    assert n % tn == 0
    has_tail = w_tail is not None
    in_specs = [
        pl.BlockSpec((tm, d), lambda i, j, fl: (i, 0)),
        pl.BlockSpec((1, d), lambda i, j, fl: (0, 0)),
        pl.BlockSpec((d, tn), lambda i, j, fl: (0, j)),
        pl.BlockSpec((1, tn), lambda i, j, fl: (0, j)),
        pl.BlockSpec((1, tn), lambda i, j, fl: (0, j)),
    ]
    out_specs = [pl.BlockSpec((tm, tn), lambda i, j, fl: (i, j))]
    out_shape = [jax.ShapeDtypeStruct((m, n), F32)]
    args = [x, g.reshape(1, d), w, post_gain, post_scale]
    if has_tail:
        nt = w_tail.shape[1]
        in_specs.append(pl.BlockSpec((d, nt), lambda i, j, fl: (0, 0)))
        out_specs.append(pl.BlockSpec((tm, nt), lambda i, j, fl: (i, 0)))
        out_shape.append(jax.ShapeDtypeStruct((m, nt), F32))
        args.append(w_tail)
    res = pl.pallas_call(
        functools.partial(_proj_body, has_tail=has_tail),
        grid_spec=pltpu.PrefetchScalarGridSpec(
            num_scalar_prefetch=1,
            grid=(m // tm, n // tn),
            in_specs=in_specs,
            out_specs=out_specs,
            scratch_shapes=[pltpu.VMEM((tm, d), BF16)],
        ),
        out_shape=out_shape,
        compiler_params=pltpu.CompilerParams(
            dimension_semantics=("parallel", "arbitrary"), vmem_limit_bytes=VMEM_LIMIT_BYTES),
        name="in_proj",
    )(flags, *args)
    return res if has_tail else res[0]


def _out_proj_body(x_ref, a1_ref, a2_ref, w1_ref, w2_ref, o_ref):
    r = jnp.dot(a1_ref[...].astype(BF16), w1_ref[...], preferred_element_type=F32)
    r = r + jnp.dot(a2_ref[...].astype(BF16), w2_ref[...], preferred_element_type=F32)
    o_ref[...] = x_ref[...] + r


def out_proj(x, a1, a2, w1, w2, *, tm=1024, tn=512):
    m, d = x.shape
    k1, k2 = a1.shape[1], a2.shape[1]
    tm = _row_tile(m, tm)
    return pl.pallas_call(
        _out_proj_body,
        grid=(m // tm, d // tn),
        in_specs=[
            pl.BlockSpec((tm, tn), lambda i, j: (i, j)),
            pl.BlockSpec((tm, k1), lambda i, j: (i, 0)),
            pl.BlockSpec((tm, k2), lambda i, j: (i, 0)),
            pl.BlockSpec((k1, tn), lambda i, j: (0, j)),
            pl.BlockSpec((k2, tn), lambda i, j: (0, j)),
        ],
        out_specs=pl.BlockSpec((tm, tn), lambda i, j: (i, j)),
        out_shape=jax.ShapeDtypeStruct((m, d), F32),
        compiler_params=pltpu.CompilerParams(
            dimension_semantics=("parallel", "arbitrary"), vmem_limit_bytes=VMEM_LIMIT_BYTES),
        name="out_proj",
    )(x, a1, a2, w1, w2)


def _dot_nt(a, b):
    return lax.dot_general(a, b, (((1,), (1,)), ((), ())), preferred_element_type=F32)


def _dot(a, b):
    return jnp.dot(a, b, preferred_element_type=F32)


def _softplus(z):
    return jnp.maximum(z, 0.0) + jnp.log(1.0 + jnp.exp(-jnp.abs(z)))


SB_KEY_CHUNK = 4 * Q_BLOCK
SB_HEADS_PER_STEP = 4
SB_DEAD_RUN = 110.0


def _sb_prompt_body(q_ref, k_ref, v_ref, tri_ref, o_ref):
    i = pl.program_id(2)
    blk = Q_BLOCK
    ck = SB_KEY_CHUNK
    nsub = ck // blk
    dh = SB_HEAD_DIM
    heads = range(SB_HEADS_PER_STEP)
    qs = [(q_ref[:, h * dh:(h + 1) * dh] * (dh ** -0.5)).astype(BF16) for h in heads]
    tri = tri_ref[...]

    def chunk_head(h, c, run, acc, diag):
        r0 = pl.multiple_of(c * ck, ck)
        z = _dot_nt(qs[h], k_ref[pl.ds(r0, ck), h * dh:(h + 1) * dh].astype(BF16))
        sp = _softplus(z)
        if diag:
            t_pos = i * blk + lax.broadcasted_iota(jnp.int32, (blk, ck), 0)
            s_pos = c * ck + lax.broadcasted_iota(jnp.int32, (blk, ck), 1)
            strict = s_pos < t_pos
            go = jnp.where(strict, sp, 0.0)
        else:
            go = sp
        hi = go.astype(BF16)
        lo = (go - hi.astype(F32)).astype(BF16)
        suffix = [None] * nsub
        for j in reversed(range(nsub)):
            sl = slice(j * blk, (j + 1) * blk)
            ct = _dot(hi[:, sl], tri) + _dot(lo[:, sl], tri)
            suffix[j] = (ct[:, :blk] - go[:, sl]) + run
            run = run + ct[:, blk:]
        att = jnp.exp((z - sp) - jnp.concatenate(suffix, axis=1))
        if diag:
            att = jnp.where(strict, att, 0.0)
        acc = acc + _dot(att.astype(BF16), v_ref[pl.ds(r0, ck), h * dh:(h + 1) * dh].astype(BF16))
        return run, acc

    def chunk(c, carry, diag):
        return tuple(chunk_head(h, c, carry[h][0], carry[h][1], diag) for h in heads)

    cd = i // nsub
    init = tuple((jnp.zeros((blk, blk), F32), jnp.zeros((blk, dh), F32)) for _ in heads)
    carry = chunk(cd, init, True)

    def live(state):
        r, cr = state
        lowest = functools.reduce(jnp.minimum, [jnp.min(cr[h][0]) for h in heads])
        return jnp.logical_and(r < cd, lowest <= SB_DEAD_RUN)

    _, carry = lax.while_loop(live, lambda st: (st[0] + 1, chunk(cd - 1 - st[0], st[1], False)), (jnp.int32(0), carry))
    for h in heads:
        o_ref[:, h * dh:(h + 1) * dh] = carry[h][1]


def sb_prompt(h, bsz, seq):
    qb = seq // Q_BLOCK
    hw = SB_HEADS_PER_STEP * SB_HEAD_DIM
    q_off = POOL_DIM // hw
    k_off = q_off + SB_DIM // hw
    v_off = k_off + SB_DIM // hw
    jj = np.arange(Q_BLOCK)
    tri = np.concatenate([(jj[:, None] >= jj[None, :]).astype(np.float32),
                          np.ones((Q_BLOCK, Q_BLOCK), np.float32)], axis=1)
    return pl.pallas_call(
        _sb_prompt_body,
        grid=(bsz, SB_DIM // hw, qb),
        in_specs=[
            pl.BlockSpec((Q_BLOCK, hw), lambda b, hh, i: (b * qb + i, q_off + hh)),
            pl.BlockSpec((seq, hw), lambda b, hh, i: (b, k_off + hh)),
            pl.BlockSpec((seq, hw), lambda b, hh, i: (b, v_off + hh)),
            pl.BlockSpec((Q_BLOCK, 2 * Q_BLOCK), lambda b, hh, i: (0, 0)),
        ],
        out_specs=pl.BlockSpec((Q_BLOCK, hw), lambda b, hh, i: (b * qb + i, hh)),
        out_shape=jax.ShapeDtypeStruct((bsz * seq, SB_DIM), F32),
        compiler_params=pltpu.CompilerParams(
            dimension_semantics=("parallel", "parallel", "arbitrary"), vmem_limit_bytes=VMEM_LIMIT_BYTES),
        name="sb_prompt",
    )(h, h, h, jnp.asarray(tri, BF16))


CONV_HALO = 8


def _split3(x):
    x1 = x.astype(BF16)
    r = x - x1.astype(F32)
    x2 = r.astype(BF16)
    x3 = (r - x2.astype(F32)).astype(BF16)
    return x1, x2, x3


def _ssd_body(z_ref, x_ref, bc_ref, xh_ref, bch_ref, cv_ref, tail_ref, h0_ref, cw_ref, cb_ref, dtb_r_ref, alog_r_ref,
              dtb_c_ref, alog_c_ref, dsk_ref, ng_ref, tril_ref, y_ref, ht_ref, st_ref, *, n_valid):
    c = pl.program_id(1)
    q = SSM_CHUNK
    p_dim = SSM_HEAD_DIM
    hpg = SSM_HEADS // SSM_GROUPS
    gn = SSM_GROUPS * SSM_STATE

    @pl.when(c == 0)
    def _():
        for h in range(SSM_HEADS):
            st_ref[h] = h0_ref[0, h].T

    first = c == 0
    cv = cv_ref[0]
    xe = jnp.concatenate([jnp.where(first, cv[:, :SSM_D_INNER], xh_ref[...]), x_ref[...]], axis=0)
    bce = jnp.concatenate([jnp.where(first, cv[:, SSM_D_INNER:], bch_ref[...]), bc_ref[...]], axis=0)

    def conv(e, lo, hi):
        acc = cb_ref[:, lo:hi]
        for k in range(SSM_CONV):
            r0 = CONV_HALO - (SSM_CONV - 1) + k
            acc = acc + e[r0:r0 + q, :] * cw_ref[k:k + 1, lo:hi]
        return acc * jax.nn.sigmoid(acc)

    xs = conv(xe, 0, SSM_D_INNER)
    bcm = conv(bce, SSM_D_INNER, SSM_CONV_DIM)

    tail = tail_ref[...]
    t_row = lax.broadcasted_iota(jnp.int32, (q, LANES), 0)
    t_lane = lax.broadcasted_iota(jnp.int32, (LANES, q), 1)
    dt_r = jnp.where(t_row < n_valid, _softplus1p(tail + dtb_r_ref[...]), 0.0)
    dt_c = jnp.where(t_lane < n_valid, _softplus1p(tail.T + dtb_c_ref[...]), 0.0)
    da_r = dt_r * (-jnp.exp(alog_r_ref[...]))
    da_c = dt_c * (-jnp.exp(alog_c_ref[...]))
    tril = tril_ref[...]
    a1, a2, a3 = _split3(da_r)
    acum = (_dot(tril, a1) + _dot(tril, a2)) + _dot(tril, a3)
    c1, c2, c3 = _split3(da_c)
    acum_t = (_dot_nt(c1, tril) + _dot_nt(c2, tril)) + _dot_nt(c3, tril)
    last = acum[q - 1:q, :]
    causal = lax.broadcasted_iota(jnp.int32, (q, q), 1) <= lax.broadcasted_iota(jnp.int32, (q, q), 0)

    for g in range(SSM_GROUPS):
        bg = bcm[:, g * SSM_STATE:(g + 1) * SSM_STATE]
        cg = bcm[:, gn + g * SSM_STATE:gn + (g + 1) * SSM_STATE].astype(BF16)
        cbg = _dot_nt(cg, bg.astype(BF16))
        bg_t = bg.T.astype(BF16)
        for hh in range(hpg):
            h = g * hpg + hh
            col = acum[:, h:h + 1]
            decay = jnp.exp(jnp.where(causal, col - acum_t[h:h + 1, :], NEG_INF))
            xh = xs[:, h * p_dim:(h + 1) * p_dim]
            xdt = xh * dt_r[:, h:h + 1]
            y_diag = _dot((cbg * decay).astype(BF16), xdt.astype(BF16))
            s_t = st_ref[h]
            y_off = _dot(cg, s_t.astype(BF16)) * jnp.exp(col)
            end = last[:, h:h + 1]
            st_ref[h] = s_t * jnp.exp(end) + _dot(bg_t, (xdt * jnp.exp(end - col)).astype(BF16))
            y_ref[:, h * p_dim:(h + 1) * p_dim] = (y_diag + y_off) + dsk_ref[:, h * p_dim:(h + 1) * p_dim] * xh

    z = z_ref[...]
    y = y_ref[...] * (z * jax.nn.sigmoid(z))
    gw = SSM_D_INNER // SSM_GROUPS
    for g in range(SSM_GROUPS):
        yg = y[:, g * gw:(g + 1) * gw]
        yg = yg * lax.rsqrt(jnp.mean(yg * yg, axis=-1, keepdims=True) + EPS)
        y_ref[:, g * gw:(g + 1) * gw] = yg * ng_ref[:, g * gw:(g + 1) * gw]

    @pl.when(c == pl.num_programs(1) - 1)
    def _():
        for h in range(SSM_HEADS):
            ht_ref[0, h] = st_ref[h].T


def _softplus1p(x):
    return jnp.maximum(x, 0.0) + jnp.log1p(jnp.exp(-jnp.abs(x)))


def ssd_mix(main, tail, conv_halo, h0_t, pw, bsz, seq, n_valid):
    q = SSM_CHUNK
    nc = seq // q
    per = q // CONV_HALO
    bcw = SSM_CONV_DIM - SSM_D_INNER
    pad_h = LANES - SSM_HEADS
    dtb = jnp.pad(pw['dt_bias'], (0, pad_h))
    alog = jnp.pad(pw['a_log'], (0, pad_h))
    tt = np.arange(q)
    tril = (tt[None, :] <= tt[:, None]).astype(np.float32)
    halo = lambda b, c: (jnp.maximum((b * nc + c) * per - 1, 0), 0)
    st_shape = (SSM_HEADS, SSM_STATE, SSM_HEAD_DIM)
    io_shape = (SSM_HEADS, SSM_HEAD_DIM, SSM_STATE)
    vec = lambda n: pl.BlockSpec((1, n), lambda b, c: (0, 0))
    colv = pl.BlockSpec((LANES, 1), lambda b, c: (0, 0))
    y, ht = pl.pallas_call(
        functools.partial(_ssd_body, n_valid=n_valid),
        grid=(bsz, nc),
        in_specs=[
            pl.BlockSpec((q, SSM_D_INNER), lambda b, c: (b * nc + c, 0)),
            pl.BlockSpec((q, SSM_D_INNER), lambda b, c: (b * nc + c, 1)),
            pl.BlockSpec((q, bcw), lambda b, c: (b * nc + c, 2 * SSM_D_INNER // bcw)),
            pl.BlockSpec((CONV_HALO, SSM_D_INNER), lambda b, c: (halo(b, c)[0], 1)),
            pl.BlockSpec((CONV_HALO, bcw), lambda b, c: (halo(b, c)[0], 2 * SSM_D_INNER // bcw)),
            pl.BlockSpec((1, CONV_HALO, SSM_CONV_DIM), lambda b, c: (b, 0, 0)),
            pl.BlockSpec((q, LANES), lambda b, c: (b * nc + c, 0)),
            pl.BlockSpec((1,) + io_shape, lambda b, c: (b, 0, 0, 0)),
            pl.BlockSpec((SSM_CONV, SSM_CONV_DIM), lambda b, c: (0, 0)),
            vec(SSM_CONV_DIM), vec(LANES), vec(LANES), colv, colv, vec(SSM_D_INNER), vec(SSM_D_INNER),
            pl.BlockSpec((q, q), lambda b, c: (0, 0)),
        ],
        out_specs=[pl.BlockSpec((q, SSM_D_INNER), lambda b, c: (b * nc + c, 0)),
                   pl.BlockSpec((1,) + io_shape, lambda b, c: (b, 0, 0, 0))],
        out_shape=[jax.ShapeDtypeStruct((bsz * seq, SSM_D_INNER), F32),
                   jax.ShapeDtypeStruct((bsz,) + io_shape, F32)],
        scratch_shapes=[pltpu.VMEM(st_shape, F32)],
        compiler_params=pltpu.CompilerParams(
            dimension_semantics=("parallel", "arbitrary"), vmem_limit_bytes=VMEM_LIMIT_BYTES),
        name="ssd_mix",
    )(main, main, main, main, main, conv_halo, tail, h0_t, pw['conv_w'], pw['conv_b'].reshape(1, -1),
      dtb.reshape(1, LANES), alog.reshape(1, LANES), dtb.reshape(LANES, 1), alog.reshape(LANES, 1),
      jnp.repeat(pw['d_skip'], SSM_HEAD_DIM).reshape(1, -1), pw['ssm_g'].reshape(1, -1), jnp.asarray(tril, BF16))
    return y, ht


POOL_HALO = 16


def _pool_body(u_ref, halo_ref, buf_ref, w_ref, sc_ref, o_ref, *, pos0):
    i = pl.program_id(1)
    tl = u_ref.shape[0]
    u = u_ref[...]
    prev = jnp.where(i == 0, buf_ref[0], halo_ref[...])
    x = jnp.concatenate([prev, u], axis=0)
    pos = pos0 + i * tl + lax.broadcasted_iota(jnp.int32, (tl, 1), 0)
    for g, w in enumerate(POOL_WINDOWS):
        cols = slice(g * POOL_GROUP_DIM, (g + 1) * POOL_GROUP_DIM)
        s = x[:, cols]
        span = 1
        while span < w:
            s = s[span:, :] + s[:-span, :]
            span *= 2
        win = s[POOL_HALO - (w - 1):, :]
        cnt = jnp.minimum(pos + 1, w).astype(F32)
        diff = win / cnt - u[:, cols]
        y = _dot(diff.astype(BF16), w_ref[g])
        o_ref[:, cols] = y * sc_ref[:, cols]


def pool_mix_pallas(h, buf16, pool_w, pool_scale, bsz, seq, pos0, *, tl=512):
    tl = min(tl, seq)
    nt = seq // tl
    per = max(tl // POOL_HALO, 1)
    if tl < POOL_HALO:
        assert nt == 1
        halo_src = buf16.reshape(bsz * POOL_HALO, POOL_DIM)
        halo_map = lambda b, i: (b, 0)
    else:
        halo_src = h
        halo_map = lambda b, i: (jnp.maximum((b * nt + i) * per - 1, 0), 0)
    return pl.pallas_call(
        functools.partial(_pool_body, pos0=pos0),
        grid=(bsz, nt),
        in_specs=[
            pl.BlockSpec((tl, POOL_DIM), lambda b, i: (b * nt + i, 0)),
            pl.BlockSpec((POOL_HALO, POOL_DIM), halo_map),
            pl.BlockSpec((1, POOL_HALO, POOL_DIM), lambda b, i: (b, 0, 0)),
            pl.BlockSpec((POOL_GROUPS, POOL_GROUP_DIM, POOL_GROUP_DIM), lambda b, i: (0, 0, 0)),
            pl.BlockSpec((1, POOL_DIM), lambda b, i: (0, 0)),
        ],
        out_specs=pl.BlockSpec((tl, POOL_DIM), lambda b, i: (b * nt + i, 0)),
        out_shape=jax.ShapeDtypeStruct((bsz * seq, POOL_DIM), F32),
        compiler_params=pltpu.CompilerParams(
            dimension_semantics=("parallel", "arbitrary"), vmem_limit_bytes=VMEM_LIMIT_BYTES),
        name="pool_mix",
    )(h, halo_src, buf16, pool_w.astype(BF16), pool_scale.reshape(1, POOL_DIM))


def _sb_decode_body(pt_ref, qbd_ref, new_ref, page_ref, tri_ref, run0_ref, acc0_ref, o_ref, run_ref, acc_ref, *,
                    n_tok, fresh):
    p = pl.program_id(1)
    rows = SB_HEADS * n_tok
    tri = tri_ref[...]
    qbd = qbd_ref[0]

    def slabs(ref, part):
        return jnp.concatenate([ref[0, :, part * SB_HEADS + h, :] for h in range(SB_HEADS)], axis=1).astype(BF16)

    def page_update(k_all, v_all, masked):
        z = _dot_nt(qbd, k_all)
        sp = _softplus(z)
        if masked:
            tok = lax.broadcasted_iota(jnp.int32, (rows, PAGE_SIZE), 0) % n_tok
            key = lax.broadcasted_iota(jnp.int32, (rows, PAGE_SIZE), 1)
            strict = key < tok
            go = jnp.where(strict, sp, 0.0)
        else:
            go = sp
        hi = go.astype(BF16)
        lo = (go - hi.astype(F32)).astype(BF16)
        ct = _dot(hi, tri) + _dot(lo, tri)
        run = run_ref[0]
        att = jnp.exp((z - sp) - ((ct[:, :PAGE_SIZE] - go) + run))
        if masked:
            att = jnp.where(strict, att, 0.0)
        run_ref[0] = run + ct[:, PAGE_SIZE:]
        acc_ref[0] += _dot(att.astype(BF16), v_all)

    @pl.when(p == 0)
    def _():
        run_ref[...] = run0_ref[...]
        acc_ref[...] = acc0_ref[...]
        if fresh:
            page_update(new_ref[0, :, :SB_DIM].astype(BF16), new_ref[0, :, SB_DIM:].astype(BF16), True)

    @pl.when(jnp.min(run_ref[...]) <= SB_DEAD_RUN)
    def _():
        page_update(slabs(page_ref, 0), slabs(page_ref, 1), False)

    @pl.when(p == pl.num_programs(1) - 1)
    def _():
        for h in range(SB_HEADS):
            o_ref[0, :, h * SB_HEAD_DIM:(h + 1) * SB_HEAD_DIM] = (
                acc_ref[0, h * n_tok:(h + 1) * n_tok, h * SB_HEAD_DIM:(h + 1) * SB_HEAD_DIM])


SB_FIRST_PAGES = 4


def sb_decode(h, cache, page_table, bsz, n_tok):
    n_pages = page_table.shape[1]
    rows = SB_HEADS * n_tok
    assert rows == PAGE_SIZE and n_tok <= PAGE_SIZE
    h3 = h.reshape(bsz, n_tok, -1)
    q = h3[..., POOL_DIM:POOL_DIM + SB_DIM].reshape(bsz, n_tok, SB_HEADS, SB_HEAD_DIM) * (SB_HEAD_DIM ** -0.5)
    eye = jnp.eye(SB_HEADS, dtype=F32)
    qbd = jnp.einsum('bthd,hg->bhtgd', q, eye).reshape(bsz, rows, SB_DIM).astype(BF16)
    new_kv = jnp.pad(h3[..., POOL_DIM + SB_DIM:], ((0, 0), (0, PAGE_SIZE - n_tok), (0, 0)))
    jj = np.arange(PAGE_SIZE)
    tri = jnp.asarray(np.concatenate([(jj[:, None] >= jj[None, :]).astype(np.float32),
                                      np.ones((PAGE_SIZE, PAGE_SIZE), np.float32)], axis=1), BF16)
    newest_first = page_table[:, ::-1]

    def walk(pages, run0, acc0, fresh):
        n = pages.shape[1]
        state = lambda w: pl.BlockSpec((1, rows, w), lambda b, p, pt: (b, 0, 0))
        return pl.pallas_call(
            functools.partial(_sb_decode_body, n_tok=n_tok, fresh=fresh),
            grid_spec=pltpu.PrefetchScalarGridSpec(
                num_scalar_prefetch=1,
                grid=(bsz, n),
                in_specs=[
                    state(SB_DIM),
                    pl.BlockSpec((1, PAGE_SIZE, 2 * SB_DIM), lambda b, p, pt: (b, 0, 0)),
                    pl.BlockSpec((1, PAGE_SIZE, 2 * SB_HEADS, SB_HEAD_DIM), lambda b, p, pt: (pt[b, p], 0, 0, 0)),
                    pl.BlockSpec((PAGE_SIZE, 2 * PAGE_SIZE), lambda b, p, pt: (0, 0)),
                    state(PAGE_SIZE), state(SB_DIM),
                ],
                out_specs=[pl.BlockSpec((1, n_tok, SB_DIM), lambda b, p, pt: (b, 0, 0)),
                           state(PAGE_SIZE), state(SB_DIM)],
            ),
            out_shape=[jax.ShapeDtypeStruct((bsz, n_tok, SB_DIM), F32),
                       jax.ShapeDtypeStruct((bsz, rows, PAGE_SIZE), F32),
                       jax.ShapeDtypeStruct((bsz, rows, SB_DIM), F32)],
            compiler_params=pltpu.CompilerParams(
                dimension_semantics=("parallel", "arbitrary"), vmem_limit_bytes=VMEM_LIMIT_BYTES),
            name="sb_decode",
        )(pages, qbd, new_kv, cache, tri, run0, acc0)

    k0 = min(SB_FIRST_PAGES, n_pages)
    o, run, acc = walk(newest_first[:, :k0], jnp.zeros((bsz, rows, PAGE_SIZE), F32),
                       jnp.zeros((bsz, rows, SB_DIM), F32), True)
    if k0 == n_pages:
        return o
    return lax.cond(jnp.min(run) > SB_DEAD_RUN, lambda: o,
                    lambda: walk(newest_first[:, k0:], run, acc, False)[0])


def _rel_bucket_np(dist):
    n = np.maximum(dist, 0)
    exact = REL_BUCKETS // 2
    nf = np.maximum(n, 1).astype(np.float32)
    log_b = exact + (np.log(nf / np.float32(exact)) / np.float32(math.log(REL_MAX_DIST / exact))
                     * np.float32(REL_BUCKETS - exact)).astype(np.int32)
    return np.where(n < exact, n, np.minimum(log_b, REL_BUCKETS - 1)).astype(np.int32)


def _bias_select(rows, bucket):
    ids = jnp.asarray(bucket)
    out = jnp.broadcast_to(rows[REL_BUCKETS - 1], bucket.shape)
    for k in range(REL_BUCKETS - 2, -1, -1):
        out = jnp.where(ids == k, rows[k], out)
    return out


def nsa_bias_tables(rel_bias, seq):
    s = np.arange(Q_BLOCK)
    rows = jnp.repeat(rel_bias.reshape(REL_BUCKETS, NSA_GROUPS, NSA_HPG).transpose(1, 0, 2), Q_BLOCK, axis=-1)
    far_from = (CMP_BAND // 2 + 1) * CMP_STRIDE - (CMP_BLOCK - 1)
    assert (_rel_bucket_np(np.arange(far_from, 2 * seq)) == REL_BUCKETS - 1).all()
    near_idx = _rel_bucket_np(np.arange(2)[:, None, None] * Q_BLOCK + s[None, None, :] - s[None, :, None])
    near_idx = np.tile(near_idx, (1, 1, NSA_HPG))
    near = jnp.stack([_bias_select(rows[g], near_idx) for g in range(NSA_GROUPS)])
    far = rows[:, REL_BUCKETS - 1:, :]
    m_rel = np.arange(-CMP_BAND // 2, CMP_BAND // 2).reshape(CMP_BAND, 1)
    dist_c = s.reshape(1, Q_BLOCK) - (m_rel * CMP_STRIDE + CMP_BLOCK - 1)
    cb_idx = np.tile(_rel_bucket_np(dist_c), (1, NSA_HPG))
    cb_ok = jnp.asarray(np.tile(dist_c >= 0, (1, NSA_HPG)))
    cb = jnp.stack([jnp.where(cb_ok, _bias_select(rows[g], cb_idx), NEG_INF) for g in range(NSA_GROUPS)])
    return near, far, cb


def _nsa_prompt_body(q_ref, tail_ref, ks_ref, vs_ref, kw_ref, vw_ref, kcmp_ref, vcmpt_ref, cb_ref, nb_ref, fb_ref,
                     at_ref, o_ref, vst_ref, vwt_ref, sel_ref, gt_ref, cbs_ref, kwp_ref, *, seq):
    g = pl.program_id(1)
    i = pl.program_id(2)
    blk = Q_BLOCK
    hw = NSA_HPG * blk
    n_slc = seq // SLC_BLOCK

    @pl.when(i == 0)
    def _():
        kwp_ref[0:WINDOW, :] = jnp.zeros((WINDOW, NSA_HEAD_DIM), BF16)
        vwt_ref[:, 0:WINDOW] = jnp.zeros((NSA_HEAD_DIM, WINDOW), BF16)
        for c in range(seq // blk):
            vst_ref[:, c * blk:(c + 1) * blk] = vs_ref[c * blk:(c + 1) * blk, :].T.astype(BF16)
            kwp_ref[WINDOW + c * blk:WINDOW + (c + 1) * blk, :] = kw_ref[c * blk:(c + 1) * blk, :].astype(BF16)
            vwt_ref[:, WINDOW + c * blk:WINDOW + (c + 1) * blk] = vw_ref[c * blk:(c + 1) * blk, :].T.astype(BF16)

    q4 = jnp.concatenate([q_ref[:, h * blk:(h + 1) * blk] for h in range(NSA_HPG)], axis=0).astype(BF16)

    n_pad = seq // CMP_STRIDE
    half = CMP_BAND // 2
    rows = lax.broadcasted_iota(jnp.int32, (n_pad + half, hw), 0)
    cbs_ref[...] = jnp.where(rows < half * i, jnp.broadcast_to(fb_ref[0], (n_pad + half, hw)), NEG_INF)
    cbs_ref[pl.ds(pl.multiple_of(half * i, half), CMP_BAND), :] = cb_ref[0]
    cb = cbs_ref[half:, :]
    sc = _dot_nt(kcmp_ref[0, 0].astype(BF16), q4) + cb
    mc = jnp.max(sc, axis=0, keepdims=True)
    pc = jnp.exp(sc - mc)
    pc = pc / jnp.sum(pc, axis=0, keepdims=True)
    pc = pc * jnp.where(cb > 0.5 * NEG_INF, 1.0, 0.0)
    o_c = _dot(vcmpt_ref[0, 0].astype(BF16), pc.astype(BF16))
    psum = pc[:, 0:blk]
    for h in range(1, NSA_HPG):
        psum = psum + pc[:, h * blk:(h + 1) * blk]
    p1 = psum.astype(BF16)
    r1 = psum - p1.astype(F32)
    p2 = r1.astype(BF16)
    p3 = (r1 - p2.astype(F32)).astype(BF16)
    at = at_ref[...]
    imp = (_dot(at, p1) + _dot(at, p2)) + _dot(at, p3)

    jdx = lax.broadcasted_iota(jnp.int32, (n_slc, blk), 0)
    tpos = i * blk + lax.broadcasted_iota(jnp.int32, (n_slc, blk), 1)
    cur = tpos // SLC_BLOCK
    forced = jnp.where(jdx == 0, 1.0, 0.0) + jnp.where(jdx == cur, 1.0, 0.0) + jnp.where(jdx == cur - 1, 1.0, 0.0)
    score = jnp.where(forced > 0.0, FORCE_SCORE, jnp.where(jdx <= cur, imp, -1.0))
    rank = jnp.zeros((n_slc, blk), F32)
    for r in range(n_slc):
        row = jnp.broadcast_to(score[r:r + 1, :], (n_slc, blk))
        gt = jnp.where(row > score, 1.0, 0.0)
        ge = jnp.where(row >= score, 1.0, 0.0)
        rank = rank + jnp.where(jdx > r, ge, gt)
    sel_ref[...] = jnp.where(rank < float(min(SLC_TOPN, n_slc)), 1.0, 0.0)

    ss = lax.broadcasted_iota(jnp.int32, (blk, blk), 0)
    tt = lax.broadcasted_iota(jnp.int32, (blk, blk), 1)
    causal = jnp.where(ss <= tt, 1.0, 0.0)
    anti = jnp.where(ss >= tt, 1.0, 0.0)
    bias_far = fb_ref[0]

    def tile4(mk):
        return jnp.concatenate([mk] * NSA_HPG, axis=1)

    def scores(k_ref, kb, bias):
        r0 = pl.multiple_of(kb * blk, blk)
        return _dot_nt(k_ref[pl.ds(r0, blk), :].astype(BF16), q4) + bias

    def pv(vt_ref, kb, p):
        r0 = pl.multiple_of(kb * blk, blk)
        return _dot(vt_ref[:, pl.ds(r0, blk)], p.astype(BF16))

    def first(s, mk):
        s = jnp.where(tile4(mk) > 0.0, s, NEG_INF)
        m = jnp.max(s, axis=0, keepdims=True)
        p = jnp.exp(s - m)
        return m, jnp.sum(p, axis=0, keepdims=True), p

    def update(carry, s, mk, vt_ref, kb):
        m, l, acc = carry
        s = jnp.where(tile4(mk) > 0.0, s, NEG_INF)
        m_new = jnp.maximum(m, jnp.max(s, axis=0, keepdims=True))
        alpha = jnp.exp(m - m_new)
        p = jnp.exp(s - m_new)
        return m_new, alpha * l + jnp.sum(p, axis=0, keepdims=True), alpha * acc + pv(vt_ref, kb, p)

    def sel_mask(kb):
        r0 = sel_ref[pl.ds(2 * kb, 1), :]
        r1 = sel_ref[pl.ds(2 * kb + 1, 1), :]
        half = blk // 2
        return jnp.concatenate([jnp.broadcast_to(r0, (half, blk)), jnp.broadcast_to(r1, (half, blk))], axis=0)

    m, l, p = first(scores(ks_ref, i, nb_ref[0, 0]), sel_mask(i) * causal)
    carry = (m, l, pv(vst_ref, i, p))
    kb1 = jnp.maximum(i - 1, 0)
    ok1 = jnp.where(i >= 1, 1.0, 0.0)
    carry = update(carry, scores(ks_ref, kb1, nb_ref[0, 1]), sel_mask(kb1) * ok1, vst_ref, kb1)

    nfar = NSA_FAR_BLOCKS

    def far_body(r, c):
        m, l, acc = c
        r0 = pl.multiple_of(r * nfar * blk, nfar * blk)
        s = _dot_nt(ks_ref[pl.ds(r0, nfar * blk), :].astype(BF16), q4) + bias_far
        mk = jnp.concatenate([sel_mask(nfar * r + j) * jnp.where(nfar * r + j <= i - 2, 1.0, 0.0)
                              for j in range(nfar)], axis=0)
        s = jnp.where(tile4(mk) > 0.0, s, NEG_INF)
        m_new = jnp.maximum(m, jnp.max(s, axis=0, keepdims=True))
        alpha = jnp.exp(m - m_new)
        p = jnp.exp(s - m_new)
        pvv = _dot(vst_ref[:, pl.ds(r0, nfar * blk)], p.astype(BF16))
        return m_new, alpha * l + jnp.sum(p, axis=0, keepdims=True), alpha * acc + pvv

    m, l, acc = lax.fori_loop(0, (i + nfar - 2) // nfar, far_body, carry)
    o_s = acc / l

    nwb = WINDOW // blk
    w0 = pl.multiple_of(i * blk, blk)
    ones = jnp.ones((blk, blk), F32)
    bias_w = jnp.concatenate([jnp.broadcast_to(bias_far, ((nwb - 1) * blk, hw)), nb_ref[0, 1], nb_ref[0, 0]], axis=0)
    mask_w = jnp.concatenate(
        [(anti if d == nwb else ones) * jnp.where(i >= d, 1.0, 0.0) for d in range(nwb, 0, -1)] + [causal], axis=0)
    sw = _dot_nt(kwp_ref[pl.ds(w0, WINDOW + blk), :], q4) + bias_w
    sw = jnp.where(tile4(mask_w) > 0.0, sw, NEG_INF)
    pw = jnp.exp(sw - jnp.max(sw, axis=0, keepdims=True))
    o_w = _dot(vwt_ref[:, pl.ds(w0, WINDOW + blk)], pw.astype(BF16)) / jnp.sum(pw, axis=0, keepdims=True)

    gt_ref[...] = tail_ref[...].T

    def gate(c):
        rows = [gt_ref[pl.ds(SSM_HEADS + 3 * (NSA_HPG * g + h) + c, 1), :] for h in range(NSA_HPG)]
        return jax.nn.sigmoid(jnp.concatenate(rows, axis=1))

    o_t = (gate(0) * o_c + gate(1) * o_s) + gate(2) * o_w
    for h in range(NSA_HPG):
        o_ref[:, h * blk:(h + 1) * blk] = o_t[:, h * blk:(h + 1) * blk].T


def nsa_prompt(main, tail, kcmp, vcmpt, tables, bsz, seq):
    near, far, cb = tables
    qb = seq // Q_BLOCK
    n_pad = seq // CMP_STRIDE
    n_slc = seq // SLC_BLOCK
    hw = NSA_HPG * Q_BLOCK
    ratio = SLC_BLOCK // CMP_STRIDE
    lo = CMP_BLOCK // CMP_STRIDE - 1
    jj = np.arange(n_slc)[:, None]
    nn = np.arange(n_pad)[None, :]
    a_t = ((nn >= ratio * jj - lo) & (nn <= ratio * jj + ratio - 1) & (nn < n_pad - 1)).astype(np.float32)
    dh = NSA_HEAD_DIM
    col = lambda off: off // dh
    return pl.pallas_call(
        functools.partial(_nsa_prompt_body, seq=seq),
        grid=(bsz, NSA_GROUPS, qb),
        in_specs=[
            pl.BlockSpec((Q_BLOCK, hw), lambda b, g, i: (b * qb + i, OFF_Q // hw + g)),
            pl.BlockSpec((Q_BLOCK, EVEN_TAIL), lambda b, g, i: (b * qb + i, 0)),
            pl.BlockSpec((seq, dh), lambda b, g, i: (b, col(OFF_KV + 2 * NSA_KV) + g)),
            pl.BlockSpec((seq, dh), lambda b, g, i: (b, col(OFF_KV + 3 * NSA_KV) + g)),
            pl.BlockSpec((seq, dh), lambda b, g, i: (b, col(OFF_WIN) + g)),
            pl.BlockSpec((seq, dh), lambda b, g, i: (b, col(OFF_WIN + NSA_KV) + g)),
            pl.BlockSpec((1, 1, n_pad, dh), lambda b, g, i: (b, g, 0, 0)),
            pl.BlockSpec((1, 1, dh, n_pad), lambda b, g, i: (b, g, 0, 0)),
            pl.BlockSpec((1, CMP_BAND, hw), lambda b, g, i: (g, 0, 0)),
            pl.BlockSpec((1, 2, Q_BLOCK, hw), lambda b, g, i: (g, 0, 0, 0)),
            pl.BlockSpec((1, 1, hw), lambda b, g, i: (g, 0, 0)),
            pl.BlockSpec((n_slc, n_pad), lambda b, g, i: (0, 0)),
        ],
        out_specs=pl.BlockSpec((Q_BLOCK, hw), lambda b, g, i: (b * qb + i, g)),
        out_shape=jax.ShapeDtypeStruct((bsz * seq, NSA_Q), F32),
        scratch_shapes=[pltpu.VMEM((dh, seq), BF16), pltpu.VMEM((dh, seq + WINDOW), BF16),
                        pltpu.VMEM((n_slc, Q_BLOCK), F32), pltpu.VMEM((EVEN_TAIL, Q_BLOCK), F32),
                        pltpu.VMEM((n_pad + CMP_BAND // 2, hw), F32), pltpu.VMEM((seq + WINDOW, dh), BF16)],
        compiler_params=pltpu.CompilerParams(
            dimension_semantics=("parallel", "parallel", "arbitrary"), vmem_limit_bytes=VMEM_LIMIT_BYTES),
        name="nsa_prompt",
    )(main, tail, main, main, main, main, kcmp, vcmpt, cb, near, far, jnp.asarray(a_t, BF16))


CMP_COLS = 2 * NSA_KV
SUBS_PER_PAGE = PAGE_SIZE // CMP_STRIDE


def _cmp_pool_body(idx_ref, src_ref, a1_ref, a2_ref, o_ref, prev_ref):
    p = pl.program_id(1)
    x = src_ref[0]
    first = (x * a1_ref[...]).reshape(SUBS_PER_PAGE, CMP_STRIDE, CMP_COLS).sum(axis=1)
    second = (x * a2_ref[...]).reshape(SUBS_PER_PAGE, CMP_STRIDE, CMP_COLS).sum(axis=1)
    prev = jnp.where(p == 0, 0.0, prev_ref[...])
    shifted = jnp.concatenate([prev[SUBS_PER_PAGE - 1:, :], first[:SUBS_PER_PAGE - 1, :]], axis=0)
    o_ref[0] = shifted + second
    prev_ref[...] = first


CMP_PAGES_PER_STEP = 8


def _cmp_pool_slab_body(idx_ref, *refs):
    src_refs = refs[:CMP_PAGES_PER_STEP]
    a1_ref, a2_ref, o_ref, prev_ref = refs[CMP_PAGES_PER_STEP:]
    p = pl.program_id(1)
    prev = jnp.where(p == 0, 0.0, prev_ref[...])
    for j, src_ref in enumerate(src_refs):
        x = src_ref[0].reshape(SUBS_PER_PAGE, CMP_STRIDE, 2 * NSA_GROUPS, NSA_HEAD_DIM)
        first = (x * a1_ref[...][None]).sum(axis=1)
        second = (x * a2_ref[...][None]).sum(axis=1)
        shifted = jnp.concatenate([prev[SUBS_PER_PAGE - 1:], first[:SUBS_PER_PAGE - 1]], axis=0)
        o_ref[0, j * SUBS_PER_PAGE:(j + 1) * SUBS_PER_PAGE] = shifted + second
        prev = first
    prev_ref[...] = prev


def cmp_pool_pages(cache, page_idx, cmp_alpha):
    bsz, n = page_idx.shape
    pps = CMP_PAGES_PER_STEP
    assert n % pps == 0
    slabs = 2 * NSA_GROUPS
    dh = NSA_HEAD_DIM

    def tiled(half):
        return jnp.concatenate([jnp.repeat(cmp_alpha[0][half][:, None, :], NSA_GROUPS, axis=1),
                                jnp.repeat(cmp_alpha[1][half][:, None, :], NSA_GROUPS, axis=1)], axis=1)

    a1 = tiled(slice(0, CMP_STRIDE))
    a2 = tiled(slice(CMP_STRIDE, CMP_BLOCK))

    def page_spec(j):
        return pl.BlockSpec((1, PAGE_SIZE, slabs, dh), lambda b, p, idx: (idx[b, p * pps + j], 0, 0, 0))

    return pl.pallas_call(
        _cmp_pool_slab_body,
        grid_spec=pltpu.PrefetchScalarGridSpec(
            num_scalar_prefetch=1,
            grid=(bsz, n // pps),
            in_specs=[page_spec(j) for j in range(pps)] + [
                pl.BlockSpec((CMP_STRIDE, slabs, dh), lambda b, p, idx: (0, 0, 0)),
                pl.BlockSpec((CMP_STRIDE, slabs, dh), lambda b, p, idx: (0, 0, 0)),
            ],
            out_specs=pl.BlockSpec((1, pps * SUBS_PER_PAGE, slabs, dh), lambda b, p, idx: (b, p, 0, 0)),
            scratch_shapes=[pltpu.VMEM((SUBS_PER_PAGE, slabs, dh), F32)],
        ),
        out_shape=jax.ShapeDtypeStruct((bsz, n * SUBS_PER_PAGE, slabs, dh), F32),
        compiler_params=pltpu.CompilerParams(
            dimension_semantics=("parallel", "arbitrary"), vmem_limit_bytes=VMEM_LIMIT_BYTES),
        name="cmp_pool_pages",
    )(page_idx, *([cache] * pps), a1, a2)


def cmp_pool(src, block_idx, col_block, cmp_alpha):
    bsz, n = block_idx.shape

    def tiled(half):
        a = jnp.concatenate([jnp.tile(cmp_alpha[0][half], (1, NSA_GROUPS)),
                             jnp.tile(cmp_alpha[1][half], (1, NSA_GROUPS))], axis=1)
        return jnp.tile(a, (SUBS_PER_PAGE, 1))

    a1 = tiled(slice(0, CMP_STRIDE))
    a2 = tiled(slice(CMP_STRIDE, CMP_BLOCK))
    return pl.pallas_call(
        _cmp_pool_body,
        grid_spec=pltpu.PrefetchScalarGridSpec(
            num_scalar_prefetch=1,
            grid=(bsz, n),
            in_specs=[
                pl.BlockSpec((1, PAGE_SIZE, CMP_COLS), lambda b, p, idx: (idx[b, p], 0, col_block)),
                pl.BlockSpec((PAGE_SIZE, CMP_COLS), lambda b, p, idx: (0, 0)),
                pl.BlockSpec((PAGE_SIZE, CMP_COLS), lambda b, p, idx: (0, 0)),
            ],
            out_specs=pl.BlockSpec((1, SUBS_PER_PAGE, CMP_COLS), lambda b, p, idx: (b, p, 0)),
            scratch_shapes=[pltpu.VMEM((SUBS_PER_PAGE, CMP_COLS), F32)],
        ),
        out_shape=jax.ShapeDtypeStruct((bsz, n * SUBS_PER_PAGE, CMP_COLS), F32),
        compiler_params=pltpu.CompilerParams(
            dimension_semantics=("parallel", "arbitrary"), vmem_limit_bytes=VMEM_LIMIT_BYTES),
        name="cmp_pool",
    )(block_idx, src, a1, a2)


def _cmp_project_body(p_ref, wk_ref, wv_ref, gk_ref, k_ref, vt_ref):
    n = p_ref.shape[1]
    dh = NSA_HEAD_DIM
    zero = jnp.zeros((1, dh), F32)
    pk = jnp.concatenate([p_ref[0, 1:, 0:dh], zero], axis=0)
    pv = jnp.concatenate([p_ref[0, 1:, dh:2 * dh], zero], axis=0)
    kp = _dot(pk.astype(BF16), wk_ref[...])
    k_ref[0, 0] = (kp * lax.rsqrt(jnp.mean(kp * kp, axis=-1, keepdims=True) + EPS)) * gk_ref[...]
    vp = _dot(pv.astype(BF16), wv_ref[...])
    for c in range(n // LANES):
        vt_ref[0, 0, :, c * LANES:(c + 1) * LANES] = vp[c * LANES:(c + 1) * LANES, :].T


def cmp_project(pooled, cmp_w, g_kcmp):
    bsz, n, _ = pooled.shape
    dh = NSA_HEAD_DIM
    pg = pooled.reshape(bsz, n, 2, NSA_GROUPS, dh).transpose(0, 3, 1, 2, 4).reshape(bsz * NSA_GROUPS, n, 2 * dh)
    k, vt = pl.pallas_call(
        _cmp_project_body,
        grid=(bsz, NSA_GROUPS),
        in_specs=[
            pl.BlockSpec((1, n, 2 * dh), lambda b, g: (b * NSA_GROUPS + g, 0, 0)),
            pl.BlockSpec((dh, dh), lambda b, g: (0, 0)),
            pl.BlockSpec((dh, dh), lambda b, g: (0, 0)),
            pl.BlockSpec((1, dh), lambda b, g: (0, 0)),
        ],
        out_specs=[pl.BlockSpec((1, 1, n, dh), lambda b, g: (b, g, 0, 0)),
                   pl.BlockSpec((1, 1, dh, n), lambda b, g: (b, g, 0, 0))],
        out_shape=[jax.ShapeDtypeStruct((bsz, NSA_GROUPS, n, dh), F32),
                   jax.ShapeDtypeStruct((bsz, NSA_GROUPS, dh, n), F32)],
        compiler_params=pltpu.CompilerParams(
            dimension_semantics=("parallel", "parallel"), vmem_limit_bytes=VMEM_LIMIT_BYTES),
        name="cmp_project",
    )(pg, cmp_w[0].astype(BF16), cmp_w[1].astype(BF16), g_kcmp.reshape(1, dh))
    return k, vt


def _nsa_decode_body(pt_ref, qbd_ref, pool_ref, wk_ref, wv_ref, gk_ref, cbias_ref, at_ref, gsum_ref, gexp_ref,
                     new_ref, nbias_ref, page_a_ref, page_b_ref, lbias_ref, far_ref, win_ref, wbias_ref, gate_ref, o_ref,
                     kc_ref, vct_ref, oc_ref, score_ref, sel_ref, m_ref, l_ref, acc_ref, *, n_tok, pos0):
    s = pl.program_id(1)
    n_steps = pl.num_programs(1) - 1
    dh = NSA_HEAD_DIM
    gd = NSA_GROUPS * dh
    lanes = NSA_HEADS * n_tok
    n_cmp = pool_ref.shape[1]
    n_rows = score_ref.shape[0]
    qbd = qbd_ref[0]

    def kv_t(v):
        return jnp.concatenate([v[:, g * dh:(g + 1) * dh].T for g in range(NSA_GROUPS)], axis=0).astype(BF16)

    def masked_update(sc, mask_rows, v, init):
        sc = jnp.where(mask_rows > 0.0, sc, NEG_INF)
        m_old = jnp.full((1, lanes), NEG_INF, F32) if init else m_ref[...]
        m_new = jnp.maximum(m_old, jnp.max(sc, axis=0, keepdims=True))
        p = jnp.exp(sc - m_new)
        pv = _dot(kv_t(v), p.astype(BF16))
        if init:
            l_ref[...] = jnp.sum(p, axis=0, keepdims=True)
            acc_ref[...] = pv
        else:
            alpha = jnp.exp(m_old - m_new)
            l_ref[...] = alpha * l_ref[...] + jnp.sum(p, axis=0, keepdims=True)
            acc_ref[...] = alpha * acc_ref[...] + pv
        m_ref[...] = m_new

    @pl.when(s == 0)
    def _():
        for g in range(NSA_GROUPS):
            kp = _dot(pool_ref[0, :, g, :].astype(BF16), wk_ref[...])
            kp = (kp * lax.rsqrt(jnp.mean(kp * kp, axis=-1, keepdims=True) + EPS)) * gk_ref[...]
            kc_ref[:, g * dh:(g + 1) * dh] = kp.astype(BF16)
            vp = _dot(pool_ref[0, :, NSA_GROUPS + g, :].astype(BF16), wv_ref[...])
            for c in range(n_cmp // LANES):
                vct_ref[g * dh:(g + 1) * dh, c * LANES:(c + 1) * LANES] = vp[c * LANES:(c + 1) * LANES, :].T.astype(BF16)
        cb = cbias_ref[...]
        sc = _dot(kc_ref[...], qbd) + cb
        pc = jnp.exp(sc - jnp.max(sc, axis=0, keepdims=True))
        pc = pc / jnp.sum(pc, axis=0, keepdims=True)
        pc = pc * jnp.where(cb > 0.5 * NEG_INF, 1.0, 0.0)
        oc_ref[...] = _dot(vct_ref[...], pc.astype(BF16))
        at = at_ref[...]
        p1, p2, p3 = _split3(pc)
        u = (_dot(at, p1) + _dot(at, p2)) + _dot(at, p3)
        gs = gsum_ref[...]
        u1, u2, u3 = _split3(u)
        imp = (_dot(u1, gs) + _dot(u2, gs)) + _dot(u3, gs)
        jdx = lax.broadcasted_iota(jnp.int32, (n_rows, lanes), 0)
        tok = lax.broadcasted_iota(jnp.int32, (n_rows, lanes), 1) % n_tok
        cur = (pos0 + tok) // SLC_BLOCK
        n_blocks = (pos0 + n_tok + SLC_BLOCK - 1) // SLC_BLOCK
        forced = (jnp.where(jdx == 0, 1.0, 0.0) + jnp.where(jdx == cur, 1.0, 0.0)
                  + jnp.where(jdx == cur - 1, 1.0, 0.0))
        score = jnp.where(forced > 0.0, FORCE_SCORE, jnp.where(jdx <= cur, imp, -1.0))
        score = jnp.where(jdx < n_blocks, score, -2.0)
        score_ref[...] = score

        def rank_body(r, rank):
            row = jnp.broadcast_to(score_ref[pl.ds(r, 1), :], (n_rows, lanes))
            gt = jnp.where(row > score, 1.0, 0.0)
            ge = jnp.where(row >= score, 1.0, 0.0)
            return rank + jnp.where(jdx > r, ge, gt)

        rank = lax.fori_loop(0, n_blocks, rank_body, jnp.zeros((n_rows, lanes), F32))
        sel = jnp.where(rank < float(SLC_TOPN), 1.0, 0.0).astype(BF16)
        sel_ref[...] = _dot(sel, gexp_ref[...])
        new = new_ref[0]
        sc = _dot(new[:, :gd].astype(BF16), qbd) + nbias_ref[...]
        rows = jnp.broadcast_to(sel_ref[pl.ds(n_blocks - 1, 1), :], (PAGE_SIZE, lanes))
        masked_update(sc, rows, new[:, gd:], True)

    @pl.when(s > 0)
    def _():
        pp = s - 1

        def slabs(part):
            return jnp.concatenate(
                [jnp.concatenate([ref[0, :, part * NSA_GROUPS + g, :] for g in range(NSA_GROUPS)], axis=1)
                 for ref in (page_a_ref, page_b_ref)], axis=0)

        far = jnp.broadcast_to(far_ref[...], (PAGE_SIZE, lanes))
        bias = jnp.concatenate([far, jnp.where(pp == n_steps - 1, lbias_ref[...], far)], axis=0)
        sc = _dot(slabs(0).astype(BF16), qbd) + bias
        per_page = PAGE_SIZE // SLC_BLOCK
        rows = jnp.concatenate([jnp.broadcast_to(sel_ref[pl.ds(2 * per_page * pp + j, 1), :], (SLC_BLOCK, lanes))
                                for j in range(2 * per_page)], axis=0)
        masked_update(sc, rows, slabs(1), False)

    @pl.when(s == n_steps)
    def _():
        o_s = acc_ref[...] / l_ref[...]
        win = win_ref[0]
        sw = _dot(win[:, :gd].astype(BF16), qbd) + wbias_ref[...]
        pw = jnp.exp(sw - jnp.max(sw, axis=0, keepdims=True))
        pw = pw / jnp.sum(pw, axis=0, keepdims=True)
        o_w = _dot(kv_t(win[:, gd:]), pw.astype(BF16))
        gates = jax.nn.sigmoid(gate_ref[0])
        o_t = (gates[0:1, :] * oc_ref[...] + gates[1:2, :] * o_s) + gates[2:3, :] * o_w
        for g in range(NSA_GROUPS):
            blk = o_t[g * dh:(g + 1) * dh, :].T
            for hh in range(NSA_HPG):
                h = g * NSA_HPG + hh
                o_ref[0, :, h * dh:(h + 1) * dh] = blk[h * n_tok:(h + 1) * n_tok, :]


def nsa_decode(main, tail, pooled, cache, page_table, win_state, pw, bsz, n_tok):
    n_pages = page_table.shape[1]
    pos0 = n_pages * PAGE_SIZE
    wb = win_state.shape[1]
    dh = NSA_HEAD_DIM
    gd = NSA_GROUPS * dh
    lanes = NSA_HEADS * n_tok
    assert lanes == LANES and pos0 % PAGE_SIZE == 0 and n_tok <= SLC_BLOCK and n_pages % 2 == 0
    n_cmp = pooled.shape[1]
    n_blocks = (pos0 + n_tok + SLC_BLOCK - 1) // SLC_BLOCK
    n_rows = -(-n_blocks // 8) * 8
    rel = pw['rel_table']
    m3 = main.reshape(bsz, n_tok, -1)
    q = m3[..., OFF_Q:OFF_KV].reshape(bsz, n_tok, NSA_HEADS, dh)
    grp = np.repeat(np.eye(NSA_GROUPS, dtype=np.float32), NSA_HPG, axis=0)
    qbd = jnp.einsum('bthd,hg->bgdht', q, jnp.asarray(grp)).reshape(bsz, gd, lanes).astype(BF16)
    t_l = np.tile(np.arange(n_tok), NSA_HEADS)[None, :]
    h_l = np.repeat(np.arange(NSA_HEADS), n_tok)

    rows = jnp.repeat(rel, n_tok, axis=1)

    def bias_table(dist, ok):
        return jnp.where(jnp.asarray(ok), _bias_select(rows, _rel_bucket_np(dist)), NEG_INF)

    r_c = np.arange(n_cmp)[:, None]
    dist_c = pos0 + t_l - ((r_c - 1) * CMP_STRIDE + CMP_BLOCK - 1)
    cbias = bias_table(dist_c, (r_c >= 1) & (dist_c >= 0))
    rr = np.arange(PAGE_SIZE)[:, None]
    nbias = bias_table(t_l - rr, (t_l - rr) >= 0)
    lbias = bias_table(PAGE_SIZE + t_l - rr, np.ones((PAGE_SIZE, lanes), bool))
    assert (_rel_bucket_np(np.arange(PAGE_SIZE + 1, 2 * PAGE_SIZE)) == REL_BUCKETS - 1).all()
    far = rows[REL_BUCKETS - 1:, :]
    n_win = -(-(wb + n_tok) // PAGE_SIZE) * PAGE_SIZE
    w_r = np.arange(n_win)[:, None]
    dist_w = pos0 + t_l - (pos0 - wb + w_r)
    wbias = bias_table(dist_w, (dist_w >= 0) & (dist_w <= WINDOW) & (pos0 - wb + w_r >= 0) & (w_r < wb + n_tok))
    ratio = SLC_BLOCK // CMP_STRIDE
    lo = CMP_BLOCK // CMP_STRIDE - 1
    jj = np.arange(n_rows)[:, None]
    nn = np.arange(n_cmp)[None, :] - 1
    a_t = ((nn >= ratio * jj - lo) & (nn <= ratio * jj + ratio - 1) & (nn >= 0) & (jj < n_blocks)).astype(np.float32)
    g_l = h_l // NSA_HPG
    gsum = np.zeros((lanes, lanes), np.float32)
    gsum[np.arange(lanes), g_l * n_tok + t_l[0]] = 1.0
    gexp = gsum.T.copy()
    new_kv = jnp.pad(m3[..., OFF_KV + 2 * NSA_KV:OFF_KV + 4 * NSA_KV], ((0, 0), (0, PAGE_SIZE - n_tok), (0, 0)))
    win_all = jnp.concatenate([win_state, m3[..., OFF_WIN:OFF_WIN + 2 * NSA_KV]], axis=1)
    win_all = jnp.pad(win_all, ((0, 0), (0, n_win - wb - n_tok), (0, 0)))
    gates = tail.reshape(bsz, n_tok, -1)[..., SSM_HEADS:SSM_HEADS + 3 * NSA_HEADS]
    gates = gates.reshape(bsz, n_tok, NSA_HEADS, 3).transpose(0, 3, 2, 1).reshape(bsz, 3, lanes)
    gates = jnp.pad(gates, ((0, 0), (0, 5), (0, 0)))
    const = lambda shape: pl.BlockSpec(shape, lambda b, s, pt: (0,) * len(shape))
    per_b = lambda shape: pl.BlockSpec((1,) + shape, lambda b, s, pt: (b,) + (0,) * len(shape))
    return pl.pallas_call(
        functools.partial(_nsa_decode_body, n_tok=n_tok, pos0=pos0),
        grid_spec=pltpu.PrefetchScalarGridSpec(
            num_scalar_prefetch=1,
            grid=(bsz, n_pages // 2 + 1),
            in_specs=[
                per_b((gd, lanes)), per_b((n_cmp, 2 * NSA_GROUPS, dh)), const((dh, dh)), const((dh, dh)),
                const((1, dh)),
                const((n_cmp, lanes)), const((n_rows, n_cmp)), const((lanes, lanes)), const((lanes, lanes)),
                per_b((PAGE_SIZE, 2 * NSA_KV)), const((PAGE_SIZE, lanes)),
                pl.BlockSpec((1, PAGE_SIZE, 2 * NSA_GROUPS, dh),
                             lambda b, s, pt: (pt[b, 2 * jnp.maximum(s - 1, 0)], 0, 1, 0)),
                pl.BlockSpec((1, PAGE_SIZE, 2 * NSA_GROUPS, dh),
                             lambda b, s, pt: (pt[b, 2 * jnp.maximum(s - 1, 0) + 1], 0, 1, 0)),
                const((PAGE_SIZE, lanes)), const((1, lanes)),
                per_b((n_win, 2 * NSA_KV)), const((n_win, lanes)), per_b((8, lanes)),
            ],
            out_specs=pl.BlockSpec((1, n_tok, NSA_Q), lambda b, s, pt: (b, 0, 0)),
            scratch_shapes=[
                pltpu.VMEM((n_cmp, gd), BF16), pltpu.VMEM((gd, n_cmp), BF16), pltpu.VMEM((gd, lanes), F32),
                pltpu.VMEM((n_rows, lanes), F32), pltpu.VMEM((n_rows, lanes), F32),
                pltpu.VMEM((1, lanes), F32), pltpu.VMEM((1, lanes), F32), pltpu.VMEM((gd, lanes), F32),
            ],
        ),
        out_shape=jax.ShapeDtypeStruct((bsz, n_tok, NSA_Q), F32),
        compiler_params=pltpu.CompilerParams(
            dimension_semantics=("parallel", "arbitrary"), vmem_limit_bytes=VMEM_LIMIT_BYTES),
        name="nsa_decode",
    )(page_table, qbd, pooled, pw['cmp_w'][0].astype(BF16), pw['cmp_w'][1].astype(BF16),
      pw['g_kcmp'].reshape(1, dh), cbias, jnp.asarray(a_t, BF16), jnp.asarray(gsum, BF16), jnp.asarray(gexp, BF16),
      new_kv, nbias, cache, cache, lbias, far, win_all, wbias, gates)


def split_cols(h, sizes):
    offs = np.cumsum(sizes)[:-1].tolist()
    return jnp.split(h, offs, axis=-1)


def even_prompt(x, bsz, seq, wb, pw):
    main, tail = in_proj(x, pw['g_mix'], pw['w_main'], pw['flags'], pw['post_gain'], pw['post_scale'], pw['w_tail'])
    y_a, h_t = ssd_mix(main, tail, jnp.zeros((bsz, CONV_HALO, SSM_CONV_DIM), F32),
                       jnp.zeros((bsz, SSM_HEADS, SSM_HEAD_DIM, SSM_STATE), F32), pw, bsz, seq, seq)
    nb = seq // PAGE_SIZE
    idx = (jnp.arange(bsz, dtype=jnp.int32)[:, None] * nb + jnp.arange(nb, dtype=jnp.int32)[None, :])
    pooled = cmp_pool(main.reshape(bsz * nb, PAGE_SIZE, EVEN_MAIN), idx, OFF_KV // CMP_COLS, pw['cmp_alpha'])
    kcmp, vcmpt = cmp_project(pooled, pw['cmp_w'], pw['g_kcmp'])
    o_b = nsa_prompt(main, tail, kcmp, vcmpt, pw['nsa_tables'], bsz, seq)
    y = out_proj(x, y_a, o_b, pw['w_out_a'], pw['w_out_b'])
    m3 = main.reshape(bsz, seq, EVEN_MAIN)
    kv = m3[..., OFF_KV:OFF_WIN].reshape(bsz, seq, NSA_KV_PARTS, NSA_GROUPS, NSA_HEAD_DIM)
    new_win = m3[:, seq - wb:, OFF_WIN:].reshape(bsz, wb, 2, NSA_GROUPS, NSA_HEAD_DIM)
    new_conv = m3[:, seq - (SSM_CONV - 1):, OFF_XBC:OFF_Q]
    return y, kv, new_win, new_conv, h_t


def even_decode(x, bsz, n_tok, cache, page_idx, win_state, conv_state, ssm_state, pw):
    wb = win_state.shape[1]
    main, tail = in_proj(x, pw['g_mix'], pw['w_main'], pw['flags'], pw['post_gain'], pw['post_scale'], pw['w_tail'])
    m3 = main.reshape(bsz, n_tok, EVEN_MAIN)
    pad_rows = ((0, 0), (0, SSM_CHUNK - n_tok), (0, 0))
    main_p = jnp.pad(m3[..., :OFF_Q], pad_rows).reshape(bsz * SSM_CHUNK, OFF_Q)
    tail_p = jnp.pad(tail.reshape(bsz, n_tok, EVEN_TAIL), pad_rows).reshape(bsz * SSM_CHUNK, EVEN_TAIL)
    halo = jnp.pad(conv_state, ((0, 0), (CONV_HALO - (SSM_CONV - 1), 0), (0, 0)))
    y_a, h_t = ssd_mix(main_p, tail_p, halo, ssm_state, pw, bsz, SSM_CHUNK, n_tok)
    y_a = y_a.reshape(bsz, SSM_CHUNK, SSM_D_INNER)[:, :n_tok].reshape(bsz * n_tok, SSM_D_INNER)
    pooled = cmp_pool_pages(cache, page_idx, pw['cmp_alpha'])
    win2 = win_state.reshape(bsz, wb, 2 * NSA_KV)
    o_b = nsa_decode(main, tail, pooled, cache, page_idx, win2, pw, bsz, n_tok)
    y = out_proj(x, y_a, o_b.reshape(bsz * n_tok, NSA_Q), pw['w_out_a'], pw['w_out_b'])
    kv = m3[..., OFF_KV:OFF_WIN].reshape(bsz, n_tok, NSA_KV_PARTS, NSA_GROUPS, NSA_HEAD_DIM)
    new_win = jnp.concatenate([win2, m3[..., OFF_WIN:]], axis=1)[:, -wb:]
    new_win = new_win.reshape(bsz, wb, 2, NSA_GROUPS, NSA_HEAD_DIM)
    new_conv = jnp.concatenate([conv_state, m3[..., OFF_XBC:OFF_Q]], axis=1)[:, -(SSM_CONV - 1):]
    return y, kv, new_win, new_conv, h_t


def odd_prompt(x, bsz, seq, pw):
    h = in_proj(x, pw['g_mix'], pw['w_in'], pw['flags'], pw['post_gain'], pw['post_scale'])
    y_c = pool_mix_pallas(h, jnp.zeros((bsz, POOL_HALO, POOL_DIM), F32), pw['pool_w'], pw['pool_scale'], bsz, seq, 0)
    o = sb_prompt(h, bsz, seq)
    y = out_proj(x, y_c, o, pw['w_out_a'], pw['w_out_b'])
    h3 = h.reshape(bsz, seq, -1)
    kv = h3[..., POOL_DIM + SB_DIM:].reshape(bsz, seq, 2, SB_HEADS, SB_HEAD_DIM)
    return y, kv, h3[:, seq - POOL_BUF:, :POOL_DIM]


def odd_decode(x, bsz, n_tok, cache, page_idx, pool_state, pw):
    pos0 = page_idx.shape[1] * PAGE_SIZE
    h = in_proj(x, pw['g_mix'], pw['w_in'], pw['flags'], pw['post_gain'], pw['post_scale'])
    buf16 = jnp.pad(pool_state, ((0, 0), (POOL_HALO - POOL_BUF, 0), (0, 0)))
    y_c = pool_mix_pallas(h, buf16, pw['pool_w'], pw['pool_scale'], bsz, n_tok, pos0)
    o = sb_decode(h, cache, page_idx, bsz, n_tok)
    y = out_proj(x, y_c, o.reshape(bsz * n_tok, SB_DIM), pw['w_out_a'], pw['w_out_b'])
    h3 = h.reshape(bsz, n_tok, -1)
    kv = h3[..., POOL_DIM + SB_DIM:].reshape(bsz, n_tok, 2, SB_HEADS, SB_HEAD_DIM)
    new_pool = jnp.concatenate([pool_state, h3[..., :POOL_DIM]], axis=1)[:, -POOL_BUF:]
    return y, kv, new_pool


def _even_weights(l, e, mix_norm, w_in_even, w_out_even, ssm_conv_w, ssm_conv_b, ssm_dt_bias, ssm_a_log, ssm_d,
                  ssm_norm, nsa_cmp_alpha, nsa_cmp_w, nsa_qk_gain, rel_bias, nsa_tables, tn):
    w = w_in_even[e]
    sizes = (SSM_D_INNER, SSM_CONV_DIM, SSM_HEADS, NSA_Q) + (NSA_KV,) * 6 + (3 * NSA_HEADS,)
    wz, wxbc, wdt, wq, wkc, wvc, wks, wvs, wkw, wvw, wg = split_cols(w, sizes)
    w_main = jnp.concatenate([wz, wxbc, wq, wkc, wvc, wks, wvs, wkw, wvw], axis=1).astype(BF16)
    pad = EVEN_TAIL - SSM_HEADS - 3 * NSA_HEADS
    w_tail = jnp.concatenate([wdt, wg, jnp.zeros((D_MODEL, pad), F32)], axis=1).astype(BF16)
    gain = nsa_qk_gain[e]
    ones = jnp.ones((NSA_KV,), F32)
    post_gain = jnp.concatenate([
        jnp.ones((OFF_Q,), F32), jnp.tile(gain[0], NSA_HEADS), ones, ones, jnp.tile(gain[2], NSA_GROUPS), ones,
        jnp.tile(gain[3], NSA_GROUPS), ones]).reshape(1, EVEN_MAIN)
    post_scale = jnp.concatenate([
        jnp.ones((OFF_Q,), F32), jnp.full((NSA_Q,), NSA_HEAD_DIM ** -0.5, F32),
        jnp.ones((6 * NSA_KV,), F32)]).reshape(1, EVEN_MAIN)
    flags = np.zeros((EVEN_MAIN // tn,), np.int32)
    for lo_, hi_ in ((OFF_Q, OFF_KV), (OFF_KV + 2 * NSA_KV, OFF_KV + 3 * NSA_KV), (OFF_WIN, OFF_WIN + NSA_KV)):
        assert lo_ % tn == 0 and hi_ % tn == 0
        flags[lo_ // tn:hi_ // tn] = 1
    wo = w_out_even[e].astype(BF16)
    return dict(g_mix=mix_norm[l], w_main=w_main, w_tail=w_tail, flags=jnp.asarray(flags), post_gain=post_gain,
                post_scale=post_scale, conv_w=ssm_conv_w[e], conv_b=ssm_conv_b[e], dt_bias=ssm_dt_bias[e],
                a_log=ssm_a_log[e], d_skip=ssm_d[e], ssm_g=ssm_norm[e], cmp_alpha=nsa_cmp_alpha[e],
                cmp_w=nsa_cmp_w[e], g_kcmp=gain[1], rel_table=rel_bias, nsa_tables=nsa_tables,
                w_out_a=wo[:SSM_D_INNER], w_out_b=wo[SSM_D_INNER:])


def _odd_weights(l, o, mix_norm, w_in_odd, w_out_odd, pool_w, pool_scale, tn):
    n = w_in_odd.shape[2]
    wo = w_out_odd[o].astype(BF16)
    return dict(g_mix=mix_norm[l], w_in=w_in_odd[o].astype(BF16), flags=jnp.zeros((n // tn,), jnp.int32),
                post_gain=jnp.ones((1, n), F32), post_scale=jnp.ones((1, n), F32),
                pool_w=pool_w[o], pool_scale=pool_scale[o], w_out_a=wo[:POOL_DIM], w_out_b=wo[POOL_DIM:])


def kernel(x_prompt, x_sample, cache_nsa_kv, cache_sb_kv, state_nsa_win, state_ssm, state_conv, state_pool, page_table, ffn_norm, ffn_w_gate, ffn_w_up, ffn_w_down, mix_norm, w_in_even, w_out_even, ssm_conv_w, ssm_conv_b, ssm_dt_bias, ssm_a_log, ssm_d, ssm_norm, nsa_cmp_alpha, nsa_cmp_w, nsa_qk_gain, rel_bias, w_in_odd, w_out_odd, pool_w, pool_scale):
    bp, lp, _ = x_prompt.shape
    bs, ls, _ = x_sample.shape
    wb = state_nsa_win.shape[2]
    n_phys = cache_nsa_kv.shape[1]
    tn = 512
    xp = x_prompt.reshape(bp * lp, D_MODEL)
    xs = x_sample.reshape(bs * ls, D_MODEL)
    wg_all = ffn_w_gate.astype(BF16)
    wu_all = ffn_w_up.astype(BF16)
    wd_all = ffn_w_down.astype(BF16)
    nsa_tables = nsa_bias_tables(rel_bias, lp)
    nsa_pages = cache_nsa_kv.reshape(-1, PAGE_SIZE, NSA_KV_PARTS * NSA_GROUPS, NSA_HEAD_DIM)
    sb_pages = cache_sb_kv.reshape(-1, PAGE_SIZE, 2 * SB_HEADS, SB_HEAD_DIM)
    kv_p, kv_s, sb_p, sb_s, win_p, win_s = [], [], [], [], [], []
    ssm_p, ssm_s, conv_p, conv_s, pool_p, pool_s = [], [], [], [], [], []
    for l in range(DEPTH):
        fa = (ffn_norm[l, 0], wg_all[l, 0], wu_all[l, 0], wd_all[l, 0])
        xp = ffn_half(xp, *fa)
        xs = ffn_half(xs, *fa)
        if l % 2 == 0:
            e = l // 2
            pw = _even_weights(l, e, mix_norm, w_in_even, w_out_even, ssm_conv_w, ssm_conv_b, ssm_dt_bias,
                               ssm_a_log, ssm_d, ssm_norm, nsa_cmp_alpha, nsa_cmp_w, nsa_qk_gain, rel_bias, nsa_tables, tn)
            xp, a_kv, a_win, a_conv, a_h = even_prompt(xp, bp, lp, wb, pw)
            xs, b_kv, b_win, b_conv, b_h = even_decode(xs, bs, ls, nsa_pages, page_table + e * n_phys,
                                                       state_nsa_win[e], state_conv[e], state_ssm[e], pw)
            kv_p.append(a_kv); kv_s.append(b_kv)
            win_p.append(a_win); win_s.append(b_win)
            conv_p.append(a_conv); conv_s.append(b_conv)
            ssm_p.append(a_h); ssm_s.append(b_h)
        else:
            o = l // 2
            pw = _odd_weights(l, o, mix_norm, w_in_odd, w_out_odd, pool_w, pool_scale, tn)
            xp, a_kv, a_pool = odd_prompt(xp, bp, lp, pw)
            xs, b_kv, b_pool = odd_decode(xs, bs, ls, sb_pages, page_table + o * n_phys, state_pool[o], pw)
            sb_p.append(a_kv); sb_s.append(b_kv)
            pool_p.append(a_pool); pool_s.append(b_pool)
        fb = (ffn_norm[l, 1], wg_all[l, 1], wu_all[l, 1], wd_all[l, 1])
        xp = ffn_half(xp, *fb)
        xs = ffn_half(xs, *fb)
    return (xp.reshape(bp, lp, D_MODEL), xs.reshape(bs, ls, D_MODEL), jnp.stack(kv_p), jnp.stack(kv_s),
            jnp.stack(sb_p), jnp.stack(sb_s), jnp.stack(win_p), jnp.stack(win_s), jnp.stack(ssm_p), jnp.stack(ssm_s),
            jnp.stack(conv_p), jnp.stack(conv_s), jnp.stack(pool_p), jnp.stack(pool_s))
```
